```python
import math
import jax, jax.numpy as jnp
from jax import lax
import numpy as np

D_MODEL = 1024
BATCH = 2
SEQ = 16384
DEPTH = 2

D_MIX = D_MODEL
GLA_HEADS = 4
GLA_DK = D_MODEL // 8
GLA_DV = 3 * D_MODEL // 16
GLA_RANK = 16
GLA_TAU = 16.0
GLA_CHUNK = 64
S5_WIDTH = D_MIX - GLA_HEADS * GLA_DV
S5_GROUP = 16
S5_GROUPS = S5_WIDTH // S5_GROUP
S5_STATE = 64
S5_DT_MIN = 1e-3
S5_DT_MAX = 1e-1
RWKV_HEAD = 64
RWKV_WIDTH = D_MIX // 2
RWKV_HEADS = RWKV_WIDTH // RWKV_HEAD
RWKV_DECAY_RANK = 64
RWKV_A_RANK = 64
RWKV_GATE_RANK = 128
RWKV_GN_EPS = 64e-5
MB_WIDTH = D_MIX - RWKV_WIDTH
MB_HEADDIM = 64
MB_HEADS = MB_WIDTH // MB_HEADDIM
MB_GROUPS = 2
MB_STATE = 128
MB_CONV = 4
MB_CHUNK = 128
D_FF = 2816
FFN_CONV = 3
EPS = 1e-6

GLA_COLS = 2 * GLA_HEADS * GLA_DK + 2 * GLA_HEADS * GLA_DV + GLA_RANK
EVEN_COLS = GLA_COLS + S5_WIDTH
RWKV_COLS = 3 * RWKV_WIDTH + RWKV_DECAY_RANK + RWKV_A_RANK + RWKV_GATE_RANK
MB_CONV_DIM = MB_WIDTH + 2 * MB_GROUPS * MB_STATE
MB_COLS = MB_WIDTH + MB_CONV_DIM + MB_HEADS
ODD_COLS = RWKV_COLS + MB_COLS

kernel_name = 'hybrid_gla_s5_rwkv7_mamba2_trunk'


def rms_norm(x, w):
    x32 = x.astype(jnp.float32)
    return (x32 * lax.rsqrt(jnp.mean(x32 * x32, -1, keepdims=True) + EPS)).astype(x.dtype) * w


def token_shift(p):
    return jnp.pad(p, ((0, 0), (1, 0), (0, 0)))[:, :-1]


def causal_dwconv(x, w, b):
    K, C = w.shape
    y = lax.conv_general_dilated(x, w.astype(x.dtype)[:, None, :], window_strides=(1,),
                                 padding=[(K - 1, 0)], dimension_numbers=('NWC', 'WIO', 'NWC'),
                                 feature_group_count=C)
    return y + b


def gla_chunked(q, k, v, log_a):
    Bs, L, H, DK = q.shape
    DV = v.shape[-1]
    nc = L // GLA_CHUNK

    def chunks(t):
        return t.reshape(Bs, nc, GLA_CHUNK, H, t.shape[-1]).transpose(1, 0, 3, 2, 4)

    causal = jnp.tril(jnp.ones((GLA_CHUNK, GLA_CHUNK), dtype=bool))[:, :, None]

    def step(S, inp):
        qc, kc, vc, gc = inp
        b = jnp.cumsum(gc, axis=2)
        diff = jnp.where(causal, b[:, :, :, None, :] - b[:, :, None, :, :], -jnp.inf)
        att = jnp.einsum('bhtk,bhsk,bhtsk->bhts', qc, kc, jnp.exp(diff))
        o = jnp.einsum('bhts,bhsv->bhtv', att, vc) + jnp.einsum('bhtk,bhkv->bhtv', qc * jnp.exp(b), S)
        b_last = b[:, :, -1, :]
        S = jnp.exp(b_last)[..., None] * S + jnp.einsum(
            'bhsk,bhsv->bhkv', kc * jnp.exp(b_last[:, :, None, :] - b), vc)
        return S, o

    S0 = jnp.zeros((Bs, H, DK, DV), jnp.float32)
    _, o = lax.scan(step, S0, (chunks(q), chunks(k), chunks(v), chunks(log_a)))
    return o.transpose(1, 0, 3, 2, 4).reshape(Bs, L, H, DV)


def gla_mixer(p, wa2, ba, norm_w):
    Bs, L, _ = p.shape
    f32 = jnp.float32
    nk = GLA_HEADS * GLA_DK
    nv = GLA_HEADS * GLA_DV
    q = p[..., :nk]
    k = p[..., nk:2 * nk]
    v = p[..., 2 * nk:2 * nk + nv]
    g = p[..., 2 * nk + nv:2 * nk + 2 * nv]
    alo = p[..., 2 * nk + 2 * nv:]
    log_a = jax.nn.log_sigmoid((alo @ wa2 + ba).astype(f32)) / GLA_TAU

    def heads(t):
        return t.astype(f32).reshape(Bs, L, GLA_HEADS, -1)

    o = gla_chunked(heads(q) * GLA_DK ** -0.5, heads(k), heads(v), heads(log_a))
    o = o * lax.rsqrt(jnp.mean(o * o, -1, keepdims=True) + EPS) * norm_w
    o = o * jax.nn.silu(heads(g))
    return o.reshape(Bs, L, nv).astype(p.dtype)


def s5_mixer(u, lam_re, lam_im, log_dt, b_re, b_im, c_re, c_im, d, w_glu, b_glu):
    Bs, L, _ = u.shape
    f32 = jnp.float32
    u32 = u.astype(f32)
    ug = u32.reshape(Bs, L, S5_GROUPS, S5_GROUP)
    lam_re = lam_re.astype(f32)
    lam_im = lam_im.astype(f32)
    dt = jnp.exp(log_dt.astype(f32))[:, None]
    mag = jnp.exp(lam_re * dt)
    ang = lam_im * dt
    lb_re = mag * jnp.cos(ang)
    lb_im = mag * jnp.sin(ang)
    den = lam_re * lam_re + lam_im * lam_im
    nr = lb_re - 1.0
    f_re = (nr * lam_re + lb_im * lam_im) / den
    f_im = (lb_im * lam_re - nr * lam_im) / den
    b_re = b_re.astype(f32)
    b_im = b_im.astype(f32)
    bb_re = f_re[..., None] * b_re - f_im[..., None] * b_im
    bb_im = f_re[..., None] * b_im + f_im[..., None] * b_re
    bu_re = jnp.einsum('blgc,gpc->lbgp', ug, bb_re)
    bu_im = jnp.einsum('blgc,gpc->lbgp', ug, bb_im)
    a_re = jnp.broadcast_to(lb_re[None, None], bu_re.shape)
    a_im = jnp.broadcast_to(lb_im[None, None], bu_re.shape)

    def combine(e1, e2):
        a1r, a1i, b1r, b1i = e1
        a2r, a2i, b2r, b2i = e2
        return (a2r * a1r - a2i * a1i,
                a2r * a1i + a2i * a1r,
                a2r * b1r - a2i * b1i + b2r,
                a2r * b1i + a2i * b1r + b2i)

    _, _, s_re, s_im = lax.associative_scan(combine, (a_re, a_im, bu_re, bu_im), axis=0)
    y = (jnp.einsum('lbgp,gcp->blgc', s_re, c_re.astype(f32))
         - jnp.einsum('lbgp,gcp->blgc', s_im, c_im.astype(f32)))
    y = y.reshape(Bs, L, S5_WIDTH) + d * u32
    z = jax.nn.gelu(y)
    out = z * jax.nn.sigmoid(z @ w_glu + b_glu)
    return out.astype(u.dtype)


def wkv7(r, w, k, v, a, b):
    Bs, L, H, N = r.shape

    def step(S, inp):
        rt, wt, kt, vt, at, bt = inp
        sa = jnp.einsum('bhij,bhj->bhi', S, at)
        S = S * wt[:, :, None, :] + sa[..., None] * bt[:, :, None, :] + vt[..., None] * kt[:, :, None, :]
        return S, jnp.einsum('bhij,bhj->bhi', S, rt)

    S0 = jnp.zeros((Bs, H, N, N), jnp.float32)
    seq = tuple(jnp.moveaxis(t, 1, 0) for t in (r, w, k, v, a, b))
    _, y = lax.scan(step, S0, seq)
    return jnp.moveaxis(y, 0, 1)


def rwkv7_mixer(p, mu, w0, w2, a0, a2, g2, k_k, k_a, r_k, ln_w, ln_b):
    Bs, L, _ = p.shape
    dtype = p.dtype
    f32 = jnp.float32
    p = p.astype(f32)
    p = p + (token_shift(p) - p) * mu
    W = RWKV_WIDTH
    o = 3 * W
    r = p[..., :W]
    k = p[..., W:2 * W]
    v = p[..., 2 * W:3 * W]
    xw = p[..., o:o + RWKV_DECAY_RANK]
    xa = p[..., o + RWKV_DECAY_RANK:o + RWKV_DECAY_RANK + RWKV_A_RANK]
    xg = p[..., o + RWKV_DECAY_RANK + RWKV_A_RANK:]
    w = -jax.nn.softplus(-(w0 + jnp.tanh(xw) @ w2)) - 0.5
    decay = jnp.exp(-jnp.exp(w))
    a = jax.nn.sigmoid(a0 + xa @ a2)
    g = jax.nn.sigmoid(xg) @ g2

    def heads(t):
        return t.reshape(Bs, L, RWKV_HEADS, RWKV_HEAD)

    kk = heads(k * k_k)
    kk = kk / jnp.maximum(jnp.sqrt(jnp.sum(kk * kk, -1, keepdims=True)), 1e-12)
    k = k * (1.0 + (a - 1.0) * k_a)
    r, k, v, decay, a = heads(r), heads(k), heads(v), heads(decay), heads(a)
    y = wkv7(r, decay, k, v, -kk, kk * a)
    mean = jnp.mean(y, -1, keepdims=True)
    var = jnp.mean(jnp.square(y - mean), -1, keepdims=True)
    y = (y - mean) * lax.rsqrt(var + RWKV_GN_EPS)
    y = y * ln_w.reshape(RWKV_HEADS, RWKV_HEAD) + ln_b.reshape(RWKV_HEADS, RWKV_HEAD)
    y = y + jnp.sum(r * k * r_k, -1, keepdims=True) * v
    return (y.reshape(Bs, L, W) * g).astype(dtype)


def ssd_chunked(x, a, bm, cm):
    Bs, L, H, P = x.shape
    G, N = bm.shape[-2], bm.shape[-1]
    E = H // G
    nc = L // MB_CHUNK
    x = x.reshape(Bs, nc, MB_CHUNK, G, E, P)
    a = a.reshape(Bs, nc, MB_CHUNK, G, E).transpose(0, 1, 3, 4, 2)
    bm = bm.reshape(Bs, nc, MB_CHUNK, G, N)
    cm = cm.reshape(Bs, nc, MB_CHUNK, G, N)
    acs = jnp.cumsum(a, -1)
    causal = jnp.tril(jnp.ones((MB_CHUNK, MB_CHUNK), dtype=bool))
    lmat = jnp.exp(jnp.where(causal, acs[..., :, None] - acs[..., None, :], -jnp.inf))
    cb = jnp.einsum('bclgn,bcsgn->bcgls', cm, bm)
    y_diag = jnp.einsum('bcgls,bcgels,bcsgep->bclgep', cb, lmat, x)
    decay_states = jnp.exp(acs[..., -1:] - acs)
    states = jnp.einsum('bclgn,bcgel,bclgep->bcgepn', bm, decay_states, x)
    chunk_decay = jnp.exp(acs[..., -1])

    def step(h, inp):
        st, dec = inp
        return dec[..., None, None] * h + st, h

    h0 = jnp.zeros((Bs, G, E, P, N), jnp.float32)
    _, prev = lax.scan(step, h0, (jnp.moveaxis(states, 1, 0), jnp.moveaxis(chunk_decay, 1, 0)))
    prev = jnp.moveaxis(prev, 0, 1)
    y_off = jnp.einsum('bclgn,bcgepn,bcgel->bclgep', cm, prev, jnp.exp(acs))
    return (y_diag + y_off).reshape(Bs, L, H, P)


def mamba2_mixer(p, conv_w, conv_b, dt_bias, a_log, d_skip, norm_w):
    Bs, L, _ = p.shape
    f32 = jnp.float32
    z = p[..., :MB_WIDTH].astype(f32)
    xbc = jax.nn.silu(causal_dwconv(p[..., MB_WIDTH:MB_WIDTH + MB_CONV_DIM], conv_w, conv_b)).astype(f32)
    dt = jax.nn.softplus(p[..., MB_WIDTH + MB_CONV_DIM:].astype(f32) + dt_bias)
    xs = xbc[..., :MB_WIDTH].reshape(Bs, L, MB_HEADS, MB_HEADDIM)
    nb = MB_GROUPS * MB_STATE
    bm = xbc[..., MB_WIDTH:MB_WIDTH + nb].reshape(Bs, L, MB_GROUPS, MB_STATE)
    cm = xbc[..., MB_WIDTH + nb:].reshape(Bs, L, MB_GROUPS, MB_STATE)
    a = -jnp.exp(a_log.astype(f32))
    y = ssd_chunked(xs * dt[..., None], dt * a, bm, cm) + xs * d_skip[:, None]
    y = y.reshape(Bs, L, MB_WIDTH) * jax.nn.silu(z)
    yg = y.reshape(Bs, L, MB_GROUPS, -1)
    yg = yg * lax.rsqrt(jnp.mean(yg * yg, -1, keepdims=True) + EPS)
    return (yg.reshape(Bs, L, MB_WIDTH) * norm_w).astype(p.dtype)


def conv_glu_ffn(h, w_up, conv_w, conv_b, w_down):
    u = h @ w_up
    gate = causal_dwconv(u[..., :D_FF], conv_w, conv_b)
    return (jax.nn.silu(gate) * u[..., D_FF:]) @ w_down


def setup_inputs(seed: int = 0) -> dict:
    key = jax.random.key(seed)
    ks = iter(jax.random.split(key, 64))
    f32 = jnp.float32
    NE = (DEPTH + 1) // 2
    NO = DEPTH // 2

    def nrm(shape, scale):
        return jax.random.normal(next(ks), shape, f32) * scale

    def uni(shape, lo, hi):
        return jax.random.uniform(next(ks), shape, f32, lo, hi)

    def gain(shape):
        return 1.0 + nrm(shape, 0.01)

    inp = {}
    inp['x'] = nrm((BATCH, SEQ, D_MODEL), 1.0)
    inp['norm_mix'] = gain((DEPTH, D_MODEL))
    inp['norm_ffn'] = gain((DEPTH, D_MODEL))
    inp['norm_final'] = gain((D_MODEL,))
    inp['e_w_in'] = nrm((NE, D_MODEL, EVEN_COLS), D_MODEL ** -0.5)
    inp['e_gla_wa2'] = nrm((NE, GLA_RANK, GLA_HEADS * GLA_DK), GLA_RANK ** -0.5)
    inp['e_gla_ba'] = nrm((NE, GLA_HEADS * GLA_DK), 0.1)
    inp['e_gla_norm'] = gain((NE, GLA_DV))
    n_idx = jnp.arange(S5_STATE, dtype=f32)
    inp['e_s5_lambda_re'] = -0.5 + nrm((NE, S5_GROUPS, S5_STATE), 0.01)
    inp['e_s5_lambda_im'] = jnp.pi * n_idx + nrm((NE, S5_GROUPS, S5_STATE), 0.01)
    inp['e_s5_log_dt'] = uni((NE, S5_GROUPS), math.log(S5_DT_MIN), math.log(S5_DT_MAX))
    inp['e_s5_b_re'] = nrm((NE, S5_GROUPS, S5_STATE, S5_GROUP), (2 * S5_GROUP) ** -0.5)
    inp['e_s5_b_im'] = nrm((NE, S5_GROUPS, S5_STATE, S5_GROUP), (2 * S5_GROUP) ** -0.5)
    inp['e_s5_c_re'] = nrm((NE, S5_GROUPS, S5_GROUP, S5_STATE), S5_STATE ** -0.5)
    inp['e_s5_c_im'] = nrm((NE, S5_GROUPS, S5_GROUP, S5_STATE), S5_STATE ** -0.5)
    inp['e_s5_d'] = nrm((NE, S5_WIDTH), 0.5)
    inp['e_s5_w_glu'] = nrm((NE, S5_WIDTH, S5_WIDTH), S5_WIDTH ** -0.5)
    inp['e_s5_b_glu'] = nrm((NE, S5_WIDTH), 0.01)
    inp['e_w_out'] = nrm((NE, D_MIX, D_MODEL), D_MIX ** -0.5)
    inp['o_w_in'] = nrm((NO, D_MODEL, ODD_COLS), D_MODEL ** -0.5)
    inp['o_rw_mu'] = uni((NO, RWKV_COLS), 0.0, 1.0)
    inp['o_rw_w0'] = uni((NO, RWKV_WIDTH), -6.0, -1.0)
    inp['o_rw_w2'] = nrm((NO, RWKV_DECAY_RANK, RWKV_WIDTH), 0.1)
    inp['o_rw_a0'] = nrm((NO, RWKV_WIDTH), 0.1)
    inp['o_rw_a2'] = nrm((NO, RWKV_A_RANK, RWKV_WIDTH), 0.1)
    inp['o_rw_g2'] = nrm((NO, RWKV_GATE_RANK, RWKV_WIDTH), RWKV_GATE_RANK ** -0.5)
    inp['o_rw_k_k'] = 0.85 + nrm((NO, RWKV_WIDTH), 0.02)
    inp['o_rw_k_a'] = 1.0 + nrm((NO, RWKV_WIDTH), 0.02)
    inp['o_rw_r_k'] = nrm((NO, RWKV_HEADS, RWKV_HEAD), 0.1)
    inp['o_rw_ln_w'] = gain((NO, RWKV_WIDTH))
    inp['o_rw_ln_b'] = nrm((NO, RWKV_WIDTH), 0.01)
    inp['o_mb_conv_w'] = nrm((NO, MB_CONV, MB_CONV_DIM), 0.5)
    inp['o_mb_conv_b'] = nrm((NO, MB_CONV_DIM), 0.01)
    dt0 = jnp.exp(uni((NO, MB_HEADS), math.log(1e-3), math.log(1e-1)))
    inp['o_mb_dt_bias'] = dt0 + jnp.log(-jnp.expm1(-dt0))
    inp['o_mb_a_log'] = jnp.log(uni((NO, MB_HEADS), 1.0, 16.0))
    inp['o_mb_d'] = gain((NO, MB_HEADS))
    inp['o_mb_norm'] = gain((NO, MB_WIDTH))
    inp['o_w_out'] = nrm((NO, D_MIX, D_MODEL), D_MIX ** -0.5)
    inp['ffn_w_up'] = nrm((DEPTH, D_MODEL, 2 * D_FF), D_MODEL ** -0.5)
    inp['ffn_conv_w'] = nrm((DEPTH, FFN_CONV, D_FF), FFN_CONV ** -0.5)
    inp['ffn_conv_b'] = nrm((DEPTH, D_FF), 0.01)
    inp['ffn_w_down'] = nrm((DEPTH, D_FF, D_MODEL), D_FF ** -0.5)
    return inp


def reference(x, norm_mix, norm_ffn, norm_final,
              e_w_in, e_gla_wa2, e_gla_ba, e_gla_norm,
              e_s5_lambda_re, e_s5_lambda_im, e_s5_log_dt, e_s5_b_re, e_s5_b_im,
              e_s5_c_re, e_s5_c_im, e_s5_d, e_s5_w_glu, e_s5_b_glu, e_w_out,
              o_w_in, o_rw_mu, o_rw_w0, o_rw_w2, o_rw_a0, o_rw_a2, o_rw_g2,
              o_rw_k_k, o_rw_k_a, o_rw_r_k, o_rw_ln_w, o_rw_ln_b,
              o_mb_conv_w, o_mb_conv_b, o_mb_dt_bias, o_mb_a_log, o_mb_d, o_mb_norm, o_w_out,
              ffn_w_up, ffn_conv_w, ffn_conv_b, ffn_w_down):
    for i in range(DEPTH):
        j = i // 2
        h = rms_norm(x, norm_mix[i])
        if i % 2 == 0:
            p = h @ e_w_in[j]
            ya = gla_mixer(p[..., :GLA_COLS], e_gla_wa2[j], e_gla_ba[j], e_gla_norm[j])
            yb = s5_mixer(p[..., GLA_COLS:], e_s5_lambda_re[j], e_s5_lambda_im[j], e_s5_log_dt[j],
                          e_s5_b_re[j], e_s5_b_im[j], e_s5_c_re[j], e_s5_c_im[j], e_s5_d[j],
                          e_s5_w_glu[j], e_s5_b_glu[j])
            x = x + jnp.concatenate([ya, yb], axis=-1) @ e_w_out[j]
        else:
            p = h @ o_w_in[j]
            yc = rwkv7_mixer(p[..., :RWKV_COLS], o_rw_mu[j], o_rw_w0[j], o_rw_w2[j], o_rw_a0[j],
                             o_rw_a2[j], o_rw_g2[j], o_rw_k_k[j], o_rw_k_a[j], o_rw_r_k[j],
                             o_rw_ln_w[j], o_rw_ln_b[j])
            yd = mamba2_mixer(p[..., RWKV_COLS:], o_mb_conv_w[j], o_mb_conv_b[j], o_mb_dt_bias[j],
                              o_mb_a_log[j], o_mb_d[j], o_mb_norm[j])
            x = x + jnp.concatenate([yc, yd], axis=-1) @ o_w_out[j]
        h = rms_norm(x, norm_ffn[i])
        x = x + conv_glu_ffn(h, ffn_w_up[i], ffn_conv_w[i], ffn_conv_b[i], ffn_w_down[i])
    return rms_norm(x, norm_final)
```

```python
import functools
import math

import jax
import jax.numpy as jnp
from jax import lax
from jax.experimental import pallas as pl
from jax.experimental.pallas import tpu as pltpu

F32 = jnp.float32
BF16 = jnp.bfloat16
HI = lax.Precision.HIGHEST

D_MODEL = 1024
D_FF = 2816
EPS = 1e-6
LANE = 128
HALO = 8

GLA_HEADS = 4
GLA_DK = 128
GLA_DV = 192
GLA_DVP = 256
GLA_RANK = 16
GLA_TAU = 16.0
GLA_CHUNK = 128
S5_WIDTH = 256
S5_GROUPS = 16
S5_GROUP = 16
S5_STATE = 64
S5_NSTATE = S5_GROUPS * S5_STATE
S5_CHUNK = 256
RW_WIDTH = 512
RW_HEAD = 64
RW_COLS = 1792
RW_GN_EPS = 64e-5
RW_CHUNK = 128
MB_WIDTH = 512
MB_HEADS = 8
MB_HEADDIM = 64
MB_GROUPS = 2
MB_STATE = 128
MB_CONV = 4
MB_CHUNK = 128
MB_XBC = 1024
PROJ_COLS = 3456

VMEM_LIMIT = 56 * 1024 * 1024


def _cparams(*sem):
    return pltpu.CompilerParams(dimension_semantics=sem, vmem_limit_bytes=VMEM_LIMIT)


def _dot(a, b, precision=None):
    return jnp.dot(a, b, preferred_element_type=F32, precision=precision)


def _dot_nt(a, b, precision=None):
    return lax.dot_general(a, b, (((1,), (1,)), ((), ())), preferred_element_type=F32, precision=precision)


def _bdot(a, b):
    return _dot(a.astype(BF16), b.astype(BF16))


def _bdot_nt(a, b):
    return _dot_nt(a.astype(BF16), b.astype(BF16))


def _rms(x, w):
    return x * lax.rsqrt(jnp.mean(x * x, axis=-1, keepdims=True) + EPS) * w


def _sigmoid(x):
    return 1.0 / (1.0 + jnp.exp(-x))


def _silu(x):
    return x * _sigmoid(x)


def _softplus(x):
    return jnp.maximum(x, 0.0) + jnp.log(1.0 + jnp.exp(-jnp.abs(x)))


def _iota2(shape, dim):
    return lax.broadcasted_iota(jnp.int32, shape, dim)


def _const_spec(shape):
    nd = len(shape)
    return pl.BlockSpec(shape, lambda *_: (0,) * nd)


def _norm_proj_kernel(x_ref, nw_ref, w_ref, *o_refs):
    h = _rms(x_ref[...], nw_ref[...])
    y = _dot(h.astype(BF16), w_ref[...])
    off = 0
    for o_ref in o_refs:
        n = o_ref.shape[1]
        o_ref[...] = y[:, off:off + n]
        off += n


def norm_proj(x2, nw, w, splits, tm=512):
    T, D = x2.shape
    N = w.shape[1]
    assert sum(splits) == N
    return pl.pallas_call(
        _norm_proj_kernel,
        out_shape=[jax.ShapeDtypeStruct((T, n), F32) for n in splits],
        grid=(T // tm,),
        in_specs=[pl.BlockSpec((tm, D), lambda i: (i, 0)), _const_spec((1, D)), _const_spec((D, N))],
        out_specs=[pl.BlockSpec((tm, n), lambda i: (i, 0)) for n in splits],
        compiler_params=_cparams("parallel"),
        name="norm_proj",
    )(x2, nw, w)


def _out_proj_kernel(x_ref, ya_ref, yb_ref, wa_ref, wb_ref, o_ref):
    o_ref[...] = x_ref[...] + _bdot(ya_ref[...], wa_ref[...]) + _bdot(yb_ref[...], wb_ref[...])


def out_proj(x2, ya, yb, wa, wb, tm=512):
    T, D = x2.shape
    ka, kb = ya.shape[1], yb.shape[1]
    return pl.pallas_call(
        _out_proj_kernel,
        out_shape=jax.ShapeDtypeStruct((T, D), F32),
        grid=(T // tm,),
        in_specs=[pl.BlockSpec((tm, D), lambda i: (i, 0)),
                  pl.BlockSpec((tm, ka), lambda i: (i, 0)),
                  pl.BlockSpec((tm, kb), lambda i: (i, 0)),
                  _const_spec((ka, D)), _const_spec((kb, D))],
        out_specs=pl.BlockSpec((tm, D), lambda i: (i, 0)),
        compiler_params=_cparams("parallel"),
        name="out_proj",
    )(x2, ya, yb, wa, wb)


def _ffn_kernel(x_ref, xh_ref, nw_ref, wg_ref, wu_ref, cw_ref, cb_ref, wd_ref, fnw_ref, o_ref, *,
                tiles_per_seq, final_norm):
    tm = x_ref.shape[0]
    x = x_ref[...]
    xa = jnp.concatenate([xh_ref[...], x], axis=0)
    h = _rms(xa, nw_ref[...]).astype(BF16)
    gate = _dot(h, wg_ref[...])
    halo_dead = jnp.where((pl.program_id(0) % tiles_per_seq) == 0, HALO, 0)
    gate = jnp.where(_iota2((HALO + tm, 1), 0) < halo_dead, 0.0, gate)
    cw = cw_ref[...]
    conv = (cw[2:3] * gate[HALO:]
            + cw[1:2] * pltpu.roll(gate, 1, 0)[HALO:]
            + cw[0:1] * pltpu.roll(gate, 2, 0)[HALO:]) + cb_ref[...]
    up = _dot(h[HALO:], wu_ref[...])
    act = (_silu(conv) * up).astype(BF16)
    y = x + _dot(act, wd_ref[...])
    if final_norm:
        y = _rms(y, fnw_ref[...])
    o_ref[...] = y


def ffn(x2, seq_len, nw, wg, wu, cw, cb, wd, fnw, final_norm, tm=256):
    T, D = x2.shape
    FF = wg.shape[1]
    hb = tm // HALO
    kern = functools.partial(_ffn_kernel, tiles_per_seq=seq_len // tm, final_norm=final_norm)
    return pl.pallas_call(
        kern,
        out_shape=jax.ShapeDtypeStruct((T, D), F32),
        grid=(T // tm,),
        in_specs=[pl.BlockSpec((tm, D), lambda i: (i, 0)),
                  pl.BlockSpec((HALO, D), lambda i: (jnp.maximum(i * hb - 1, 0), 0)),
                  _const_spec((1, D)), _const_spec((D, FF)), _const_spec((D, FF)),
                  _const_spec((3, FF)), _const_spec((1, FF)), _const_spec((FF, D)), _const_spec((1, D))],
        out_specs=pl.BlockSpec((tm, D), lambda i: (i, 0)),
        compiler_params=_cparams("parallel"),
        name="ffn",
    )(x2, x2, nw, wg, wu, cw, cb, wd, fnw)


def _gla_kernel(q_ref, k_ref, v_ref, g_ref, alo_ref, wa_ref, ba_ref, nw_ref, o_ref, st_ref):
    @pl.when(pl.program_id(2) == 0)
    def _():
        st_ref[...] = jnp.zeros_like(st_ref)

    C = GLA_CHUNK
    tq = q_ref.shape[0]
    xg = _dot(alo_ref[...], wa_ref[0], HI) + ba_ref[0]
    la = (jnp.minimum(xg, 0.0) - jnp.log(1.0 + jnp.exp(-jnp.abs(xg)))) * (1.0 / GLA_TAU)
    ri = _iota2((C, C), 0)
    ci = _iota2((C, C), 1)
    tril = (ri >= ci)
    tril_f = tril.astype(F32)
    scale = GLA_DK ** -0.5
    for j in range(tq // C):
        sl = slice(j * C, (j + 1) * C)
        b = _dot(tril_f, la[sl], HI)
        b_mid = b[C // 2 - 1:C // 2]
        b_last = b[C - 1:C]
        q = q_ref[sl, :] * scale
        k = k_ref[sl, :]
        v = v_ref[sl, :]
        st = st_ref[...]
        qe = q * jnp.exp(b - b_mid)
        ke = k * jnp.exp(jnp.minimum(b_mid - b, 80.0))
        att = jnp.where(tril, _bdot_nt(qe, ke), 0.0)
        o = _bdot(att, v) + _bdot_nt(q * jnp.exp(b), st)
        kd = k * jnp.exp(b_last - b)
        st_ref[...] = st * jnp.exp(b_last) + _bdot(v.T, kd)
        ms = jnp.sum(o * o, axis=-1, keepdims=True) * (1.0 / GLA_DV)
        o = o * lax.rsqrt(ms + EPS) * nw_ref[...]
        o_ref[sl, :] = o * _silu(g_ref[sl, :])


def gla(q, k, v, g, alo, B, L, wa, ba, nw, tq=512):
    T = q.shape[0]
    nq = L // tq
    per_head = lambda w: pl.BlockSpec((tq, w), lambda b, h, c: (b * nq + c, h))
    return pl.pallas_call(
        _gla_kernel,
        out_shape=jax.ShapeDtypeStruct((T, GLA_HEADS * GLA_DVP), F32),
        grid=(B, GLA_HEADS, nq),
        in_specs=[per_head(GLA_DK), per_head(GLA_DK), per_head(GLA_DVP), per_head(GLA_DVP),
                  pl.BlockSpec((tq, LANE), lambda b, h, c: (b * nq + c, 0)),
                  pl.BlockSpec((1, LANE, GLA_DK), lambda b, h, c: (h, 0, 0)),
                  pl.BlockSpec((1, 1, GLA_DK), lambda b, h, c: (h, 0, 0)),
                  _const_spec((1, GLA_DVP))],
        out_specs=per_head(GLA_DVP),
        scratch_shapes=[pltpu.VMEM((GLA_DVP, GLA_DK), F32)],
        compiler_params=_cparams("parallel", "parallel", "arbitrary"),
        name="gla",
    )(q, k, v, g, alo, wa, ba, nw)


def _shift_rows(x, sh):
    n = x.shape[0]
    if sh % HALO == 0:
        return jnp.concatenate([jnp.zeros((sh, x.shape[1]), x.dtype), x[:n - sh]], axis=0)
    rolled = pltpu.roll(x, sh, 0)
    return jnp.where(_iota2((n, 1), 0) < sh, 0.0, rolled)


def _s5_kernel(u_ref, lr_ref, li_ref, br_ref, bi_ref, cr_ref, ci_ref, d_ref, wg_ref, bg_ref, o_ref,
               sr_ref, si_ref):
    @pl.when(pl.program_id(1) == 0)
    def _():
        sr_ref[...] = jnp.zeros_like(sr_ref)
        si_ref[...] = jnp.zeros_like(si_ref)

    tc = u_ref.shape[0]
    u = u_ref[...]
    ub = u.astype(BF16)
    lr = lr_ref[...]
    li = li_ref[...]
    xr = _dot(ub, br_ref[...])
    xi = _dot(ub, bi_ref[...])
    pr = sr_ref[...]
    pi = si_ref[...]
    first = _iota2((tc, 1), 0) == 0
    xr = xr + jnp.where(first, lr * pr - li * pi, 0.0)
    xi = xi + jnp.where(first, lr * pi + li * pr, 0.0)
    mr, mi = lr, li
    sh = 1
    while sh < tc:
        yr = _shift_rows(xr, sh)
        yi = _shift_rows(xi, sh)
        xr, xi = xr + mr * yr - mi * yi, xi + mr * yi + mi * yr
        mr, mi = mr * mr - mi * mi, 2.0 * mr * mi
        sh *= 2
    sr_ref[...] = xr[tc - 1:tc]
    si_ref[...] = xi[tc - 1:tc]
    y = _bdot(xr, cr_ref[...]) - _bdot(xi, ci_ref[...]) + d_ref[...] * u
    z = 0.5 * y * (1.0 + jnp.tanh(math.sqrt(2.0 / math.pi) * (y + 0.044715 * (y * y * y))))
    o_ref[...] = z * _sigmoid(_bdot(z, wg_ref[...]) + bg_ref[...])


def s5(u, B, L, lr, li, br, bi, cr, ci, d, wg, bg, tc=S5_CHUNK):
    T = u.shape[0]
    nc = L // tc
    NS = S5_NSTATE
    return pl.pallas_call(
        _s5_kernel,
        out_shape=jax.ShapeDtypeStruct((T, S5_WIDTH), F32),
        grid=(B, nc),
        in_specs=[pl.BlockSpec((tc, S5_WIDTH), lambda b, c: (b * nc + c, 0)),
                  _const_spec((1, NS)), _const_spec((1, NS)),
                  _const_spec((S5_WIDTH, NS)), _const_spec((S5_WIDTH, NS)),
                  _const_spec((NS, S5_WIDTH)), _const_spec((NS, S5_WIDTH)),
                  _const_spec((1, S5_WIDTH)), _const_spec((S5_WIDTH, S5_WIDTH)), _const_spec((1, S5_WIDTH))],
        out_specs=pl.BlockSpec((tc, S5_WIDTH), lambda b, c: (b * nc + c, 0)),
        scratch_shapes=[pltpu.VMEM((1, NS), F32), pltpu.VMEM((1, NS), F32)],
        compiler_params=_cparams("parallel", "arbitrary"),
        name="s5",
    )(u, lr, li, br, bi, cr, ci, d, wg, bg)


def _rwkv_prep_kernel(p_ref, ph_ref, mu_ref, wlr_ref, w0a0_ref, g2_ref, kk_ref, ka_ref, rk_ref, bd_ref,
                      r_ref, a_ref, v_ref, lw_ref, g_ref, bon_ref, kt_ref, bt_ref, lwt_ref, *, tiles_per_seq):
    tm = p_ref.shape[0]
    W = RW_WIDTH
    p = p_ref[...]
    pa = jnp.concatenate([ph_ref[...], p], axis=0)
    prev = pltpu.roll(pa, 1, 0)[HALO:]
    first_dead = jnp.where((pl.program_id(0) % tiles_per_seq) == 0, 1, 0)
    prev = jnp.where(_iota2((tm, 1), 0) < first_dead, 0.0, prev)
    pm = p + (prev - p) * mu_ref[...]
    r = pm[:, :W]
    k = pm[:, W:2 * W]
    v = pm[:, 2 * W:3 * W]
    xwa = pm[:, 3 * W:3 * W + LANE]
    xg = pm[:, 3 * W + LANE:]
    xwa = jnp.where(_iota2((tm, LANE), 1) < 64, jnp.tanh(xwa), xwa)
    wa = _dot(xwa, wlr_ref[...], HI) + w0a0_ref[...]
    wlog = -_softplus(-wa[:, :W]) - 0.5
    lw = -jnp.exp(wlog)
    a = _sigmoid(wa[:, W:])
    g = _dot(_sigmoid(xg), g2_ref[...], HI)
    bd = bd_ref[...]
    kk = k * kk_ref[...]
    kk = kk / jnp.maximum(jnp.sqrt(_dot(kk * kk, bd, HI)), 1e-12)
    k2 = k * (1.0 + (a - 1.0) * ka_ref[...])
    r_ref[...] = r
    a_ref[...] = -kk
    v_ref[...] = v
    lw_ref[...] = lw
    g_ref[...] = g
    bon_ref[...] = _dot(r * k2 * rk_ref[...], bd, HI) * v
    kt_ref[...] = k2.T
    bt_ref[...] = (kk * a).T
    lwt_ref[...] = lw.T


def rwkv_prep(p, L, mu, wlr, w0a0, g2, kk, ka, rk, bd, tm=256):
    T = p.shape[0]
    W = RW_WIDTH
    hb = tm // HALO
    tok = jax.ShapeDtypeStruct((T, W), F32)
    chn = jax.ShapeDtypeStruct((W, T), F32)
    tok_spec = pl.BlockSpec((tm, W), lambda i: (i, 0))
    chn_spec = pl.BlockSpec((W, tm), lambda i: (0, i))
    kern = functools.partial(_rwkv_prep_kernel, tiles_per_seq=L // tm)
    return pl.pallas_call(
        kern,
        out_shape=[tok] * 6 + [chn] * 3,
        grid=(T // tm,),
        in_specs=[pl.BlockSpec((tm, RW_COLS), lambda i: (i, 0)),
                  pl.BlockSpec((HALO, RW_COLS), lambda i: (jnp.maximum(i * hb - 1, 0), 0)),
                  _const_spec((1, RW_COLS)), _const_spec((LANE, 2 * W)), _const_spec((1, 2 * W)),
                  _const_spec((LANE, W)), _const_spec((1, W)), _const_spec((1, W)), _const_spec((1, W)),
                  _const_spec((W, W))],
        out_specs=[tok_spec] * 6 + [chn_spec] * 3,
        compiler_params=_cparams("parallel"),
        name="rwkv_prep",
    )(p, p, mu, wlr, w0a0, g2, kk, ka, rk, bd)


def _wkv_kernel(r_ref, a_ref, v_ref, lw_ref, g_ref, bon_ref, kt_ref, bt_ref, lwt_ref, lnw_ref, lnb_ref,
                o_ref, h_ref):
    @pl.when(pl.program_id(2) == 0)
    def _():
        h_ref[...] = jnp.zeros_like(h_ref)

    C = RW_CHUNK
    tq = r_ref.shape[0]
    ri = _iota2((C, C), 0)
    ci = _iota2((C, C), 1)
    incl = ri >= ci
    strict = ri > ci
    tril_f = incl.astype(F32)
    triu_f = (ri <= ci).astype(F32)
    m0 = ci < RW_HEAD
    bdm = (ri // RW_HEAD) == (ci // RW_HEAD)
    gn = jnp.where(bdm, 1.0 / RW_HEAD, 0.0)
    for j in range(tq // C):
        sl = slice(j * C, (j + 1) * C)
        r = r_ref[sl, :]
        a = a_ref[sl, :]
        v = v_ref[sl, :]
        lw = lw_ref[sl, :]
        kt = kt_ref[:, sl]
        bt = bt_ref[:, sl]
        lwt = lwt_ref[:, sl]
        h = h_ref[...]
        c = _dot(tril_f, lw, HI)
        ct = _dot(lwt, triu_f, HI)
        c_mid = c[C // 2 - 1:C // 2]
        ct_mid = ct[:, C // 2 - 1:C // 2]
        ct_last = ct[:, C - 1:C]
        at = a * jnp.exp(c - lw - c_mid)
        rt = r * jnp.exp(c - c_mid)
        e_mid = jnp.exp(ct_mid - ct)
        e_last = jnp.exp(ct_last - ct)
        lhs = jnp.concatenate([jnp.where(m0, at, 0.0), jnp.where(m0, 0.0, at),
                               jnp.where(m0, rt, 0.0), jnp.where(m0, 0.0, rt)], axis=0)
        rhs = jnp.concatenate([bt * e_mid, kt * e_mid], axis=1)
        aa = _dot(lhs, rhs, HI)
        n0 = jnp.where(strict, aa[0:C, :C], 0.0)
        n1 = jnp.where(strict, aa[C:2 * C, :C], 0.0)
        ak0 = jnp.where(strict, aa[0:C, C:], 0.0)
        ak1 = jnp.where(strict, aa[C:2 * C, C:], 0.0)
        rb0 = jnp.where(incl, aa[2 * C:3 * C, :C], 0.0)
        rb1 = jnp.where(incl, aa[3 * C:, :C], 0.0)
        rk0 = jnp.where(incl, aa[2 * C:3 * C, C:], 0.0)
        rk1 = jnp.where(incl, aa[3 * C:, C:], 0.0)
        hh = _dot(jnp.concatenate([a * jnp.exp(c - lw), r * jnp.exp(c)], axis=0), h, HI)
        x = hh[:C] + jnp.where(m0, _dot(ak0, v, HI), _dot(ak1, v, HI))
        sh = 1
        while sh < C:
            x = x + jnp.where(m0, _dot(n0, x, HI), _dot(n1, x, HI))
            sh *= 2
            if sh < C:
                n0 = _dot(n0, n0, HI)
                n1 = _dot(n1, n1, HI)
        y = hh[C:] + jnp.where(m0, _dot(rb0, x, HI) + _dot(rk0, v, HI), _dot(rb1, x, HI) + _dot(rk1, v, HI))
        upd = _dot(jnp.concatenate([bt * e_last, kt * e_last], axis=1), jnp.concatenate([x, v], axis=0), HI)
        h_ref[...] = h * jnp.exp(ct_last) + jnp.where(bdm, upd, 0.0)
        mean = _dot(y, gn, HI)
        yc = y - mean
        var = _dot(yc * yc, gn, HI)
        yn = yc * lax.rsqrt(var + RW_GN_EPS) * lnw_ref[...] + lnb_ref[...]
        o_ref[sl, :] = (yn + bon_ref[sl, :]) * g_ref[sl, :]


def wkv(r, a, v, lw, g, bon, kt, bt, lwt, lnw, lnb, B, L, tq=256):
    T = r.shape[0]
    nq = L // tq
    npair = RW_WIDTH // LANE
    tok = pl.BlockSpec((tq, LANE), lambda b, h, c: (b * nq + c, h))
    chn = pl.BlockSpec((LANE, tq), lambda b, h, c: (h, b * nq + c))
    vec = pl.BlockSpec((1, LANE), lambda b, h, c: (0, h))
    return pl.pallas_call(
        _wkv_kernel,
        out_shape=jax.ShapeDtypeStruct((T, RW_WIDTH), F32),
        grid=(B, npair, nq),
        in_specs=[tok] * 6 + [chn] * 3 + [vec, vec],
        out_specs=tok,
        scratch_shapes=[pltpu.VMEM((LANE, LANE), F32)],
        compiler_params=_cparams("parallel", "parallel", "arbitrary"),
        name="wkv7",
    )(r, a, v, lw, g, bon, kt, bt, lwt, lnw, lnb)


def _mamba_kernel(z_ref, xbc_ref, xh_ref, dt_ref, cw_ref, cb_ref, dtb_ref, an_ref, ex_ref, dsk_ref, nw_ref,
                  o_ref, st_ref):
    @pl.when(pl.program_id(1) == 0)
    def _():
        st_ref[...] = jnp.zeros_like(st_ref)

    C = MB_CHUNK
    W = MB_WIDTH
    N = MB_STATE
    xbc = xbc_ref[...]
    xa = jnp.concatenate([xh_ref[...], xbc], axis=0)
    halo_dead = jnp.where(pl.program_id(1) == 0, HALO, 0)
    xa = jnp.where(_iota2((HALO + C, 1), 0) < halo_dead, 0.0, xa)
    cw = cw_ref[...]
    conv = cw[3:4] * xbc + cb_ref[...]
    for kk in range(1, MB_CONV):
        conv = conv + cw[3 - kk:4 - kk] * pltpu.roll(xa, kk, 0)[HALO:]
    xbc = _silu(conv)
    xs = xbc[:, :W]
    dt = _softplus(dt_ref[...] + dtb_ref[...])
    a = dt * an_ref[...]
    ri = _iota2((C, C), 0)
    ci = _iota2((C, C), 1)
    causal = ri >= ci
    acs = _dot(causal.astype(F32), a, HI)
    acs_t = acs.T
    ex = ex_ref[...]
    acs_x = _dot(acs, ex, HI)
    dt_x = _dot(dt, ex, HI)
    xdt = xs * dt_x
    st = st_ref[...]
    lane_w = _iota2((C, W), 1)
    y = jnp.zeros((C, W), F32)
    for gi in range(MB_GROUPS):
        bm = xbc[:, W + gi * N:W + (gi + 1) * N]
        cm = xbc[:, W + MB_GROUPS * N + gi * N:W + MB_GROUPS * N + (gi + 1) * N]
        cb = _bdot_nt(cm, bm)
        hpg = MB_HEADS // MB_GROUPS
        for e in range(hpg):
            hd = gi * hpg + e
            lmat = jnp.exp(jnp.where(causal, acs[:, hd:hd + 1] - acs_t[hd:hd + 1, :], -jnp.inf))
            hm = (lane_w // MB_HEADDIM) == hd
            y = y + _bdot(cb * lmat, jnp.where(hm, xdt, 0.0))
        gw = W // MB_GROUPS
        off = _bdot_nt(cm, st[gi * gw:(gi + 1) * gw])
        gm = (lane_w // gw) == gi
        offp = jnp.concatenate([off] * MB_GROUPS, axis=1)
        y = y + jnp.where(gm, offp * jnp.exp(acs_x), 0.0)
    acs_last = acs_x[C - 1:C]
    xw_t = (xdt * jnp.exp(acs_last - acs_x)).T
    acs_xt = acs_x.T
    chunk_decay = jnp.exp(acs_xt[:, C - 1:C])
    gw = W // MB_GROUPS
    new = [_bdot(xw_t[gi * gw:(gi + 1) * gw], xbc[:, W + gi * N:W + (gi + 1) * N]) for gi in range(MB_GROUPS)]
    st_ref[...] = chunk_decay * st + jnp.concatenate(new, axis=0)
    y = (y + xs * dsk_ref[...]) * _silu(z_ref[...])
    parts = []
    for gi in range(MB_GROUPS):
        yg = y[:, gi * gw:(gi + 1) * gw]
        parts.append(yg * lax.rsqrt(jnp.mean(yg * yg, axis=-1, keepdims=True) + EPS))
    o_ref[...] = jnp.concatenate(parts, axis=1) * nw_ref[...]


def mamba(z, xbc, dt, B, L, cw, cb, dtb, an, ex, dsk, nw):
    T = z.shape[0]
    C = MB_CHUNK
    nc = L // C
    hb = C // HALO
    return pl.pallas_call(
        _mamba_kernel,
        out_shape=jax.ShapeDtypeStruct((T, MB_WIDTH), F32),
        grid=(B, nc),
        in_specs=[pl.BlockSpec((C, MB_WIDTH), lambda b, c: (b * nc + c, 0)),
                  pl.BlockSpec((C, MB_XBC), lambda b, c: (b * nc + c, 0)),
                  pl.BlockSpec((HALO, MB_XBC), lambda b, c: (jnp.maximum((b * nc + c) * hb - 1, 0), 0)),
                  pl.BlockSpec((C, LANE), lambda b, c: (b * nc + c, 0)),
                  _const_spec((MB_CONV, MB_XBC)), _const_spec((1, MB_XBC)), _const_spec((1, LANE)),
                  _const_spec((1, LANE)), _const_spec((LANE, MB_WIDTH)), _const_spec((1, MB_WIDTH)),
                  _const_spec((1, MB_WIDTH))],
        out_specs=pl.BlockSpec((C, MB_WIDTH), lambda b, c: (b * nc + c, 0)),
        scratch_shapes=[pltpu.VMEM((MB_WIDTH, MB_STATE), F32)],
        compiler_params=_cparams("parallel", "arbitrary"),
        name="mamba2",
    )(z, xbc, xbc, dt, cw, cb, dtb, an, ex, dsk, nw)


def _pack_even(w_in, wa2, ba, gnorm, w_out):
    nk = GLA_HEADS * GLA_DK
    nv = GLA_HEADS * GLA_DV
    pad_v = lambda w: jnp.pad(w.reshape(D_MODEL, GLA_HEADS, GLA_DV),
                              ((0, 0), (0, 0), (0, GLA_DVP - GLA_DV))).reshape(D_MODEL, GLA_HEADS * GLA_DVP)
    wq = w_in[:, :2 * nk]
    wv = pad_v(w_in[:, 2 * nk:2 * nk + nv])
    wg = pad_v(w_in[:, 2 * nk + nv:2 * nk + 2 * nv])
    walo = jnp.pad(w_in[:, 2 * nk + 2 * nv:2 * nk + 2 * nv + GLA_RANK], ((0, 0), (0, LANE - GLA_RANK)))
    wu = w_in[:, 2 * nk + 2 * nv + GLA_RANK:]
    w_cat = jnp.concatenate([wq, wv, wg, wu, walo], axis=1).astype(BF16)
    wa = jnp.pad(wa2.reshape(GLA_RANK, GLA_HEADS, GLA_DK).transpose(1, 0, 2),
                 ((0, 0), (0, LANE - GLA_RANK), (0, 0)))
    bah = ba.reshape(GLA_HEADS, 1, GLA_DK)
    nw = jnp.pad(gnorm, (0, GLA_DVP - GLA_DV)).reshape(1, GLA_DVP)
    woa = jnp.pad(w_out[:nv].reshape(GLA_HEADS, GLA_DV, D_MODEL),
                  ((0, 0), (0, GLA_DVP - GLA_DV), (0, 0))).reshape(GLA_HEADS * GLA_DVP, D_MODEL).astype(BF16)
    wob = w_out[nv:].astype(BF16)
    return w_cat, wa, bah, nw, woa, wob


def _pack_s5(lam_re, lam_im, log_dt, b_re, b_im, c_re, c_im):
    dt = jnp.exp(log_dt)[:, None]
    mag = jnp.exp(lam_re * dt)
    ang = lam_im * dt
    lb_re = mag * jnp.cos(ang)
    lb_im = mag * jnp.sin(ang)
    den = lam_re * lam_re + lam_im * lam_im
    nr = lb_re - 1.0
    f_re = (nr * lam_re + lb_im * lam_im) / den
    f_im = (lb_im * lam_re - nr * lam_im) / den
    bb_re = f_re[..., None] * b_re - f_im[..., None] * b_im
    bb_im = f_re[..., None] * b_im + f_im[..., None] * b_re
    eye = jnp.eye(S5_GROUPS, dtype=F32)
    bd_in = lambda m: jnp.einsum('gpc,gh->gchp', m, eye).reshape(S5_WIDTH, S5_NSTATE).astype(BF16)
    bd_out = lambda m: jnp.einsum('gcp,gh->gphc', m, eye).reshape(S5_NSTATE, S5_WIDTH).astype(BF16)
    return (lb_re.reshape(1, S5_NSTATE), lb_im.reshape(1, S5_NSTATE),
            bd_in(bb_re), bd_in(bb_im), bd_out(c_re), bd_out(c_im))


def _pack_odd(w_in, w0, w2, a0, a2):
    w_cat = jnp.pad(w_in, ((0, 0), (0, PROJ_COLS - w_in.shape[1]))).astype(BF16)
    W = RW_WIDTH
    wlr = jnp.zeros((LANE, 2 * W), F32).at[:64, :W].set(w2).at[64:, W:].set(a2)
    w0a0 = jnp.concatenate([w0, a0]).reshape(1, 2 * W)
    return w_cat, wlr, w0a0


def kernel(x, norm_mix, norm_ffn, norm_final, e_w_in, e_gla_wa2, e_gla_ba, e_gla_norm, e_s5_lambda_re, e_s5_lambda_im, e_s5_log_dt, e_s5_b_re, e_s5_b_im, e_s5_c_re, e_s5_c_im, e_s5_d, e_s5_w_glu, e_s5_b_glu, e_w_out, o_w_in, o_rw_mu, o_rw_w0, o_rw_w2, o_rw_a0, o_rw_a2, o_rw_g2, o_rw_k_k, o_rw_k_a, o_rw_r_k, o_rw_ln_w, o_rw_ln_b, o_mb_conv_w, o_mb_conv_b, o_mb_dt_bias, o_mb_a_log, o_mb_d, o_mb_norm, o_w_out, ffn_w_up, ffn_conv_w, ffn_conv_b, ffn_w_down):
    B, L, D = x.shape
    T = B * L
    depth = norm_mix.shape[0]
    x2 = x.reshape(T, D)
    row = lambda t: t.reshape(1, -1)
    for i in range(depth):
        j = i // 2
        if i % 2 == 0:
            w_cat, wa, bah, gnw, woa, wob = _pack_even(e_w_in[j], e_gla_wa2[j], e_gla_ba[j], e_gla_norm[j],
                                                       e_w_out[j])
            nkq = GLA_HEADS * GLA_DK
            nvp = GLA_HEADS * GLA_DVP
            q, k, v, g, u, alo = norm_proj(x2, row(norm_mix[i]), w_cat, (nkq, nkq, nvp, nvp, S5_WIDTH, LANE))
            ya = gla(q, k, v, g, alo, B, L, wa, bah, gnw)
            lr, li, br, bi, cr, ci = _pack_s5(e_s5_lambda_re[j], e_s5_lambda_im[j], e_s5_log_dt[j],
                                              e_s5_b_re[j], e_s5_b_im[j], e_s5_c_re[j], e_s5_c_im[j])
            yb = s5(u, B, L, lr, li, br, bi, cr, ci, row(e_s5_d[j]), e_s5_w_glu[j].astype(BF16),
                    row(e_s5_b_glu[j]))
            x2 = out_proj(x2, ya, yb, woa, wob)
        else:
            W = RW_WIDTH
            w_cat, wlr, w0a0 = _pack_odd(o_w_in[j], o_rw_w0[j], o_rw_w2[j], o_rw_a0[j], o_rw_a2[j])
            p_rw, p_z, p_xbc, p_dt = norm_proj(x2, row(norm_mix[i]), w_cat, (RW_COLS, MB_WIDTH, MB_XBC, LANE))
            head_of = jnp.arange(W) // RW_HEAD
            bd = (head_of[:, None] == head_of[None, :]).astype(F32)
            r, a, v, lw, g, bon, kt, bt, lwt = rwkv_prep(
                p_rw, L, row(o_rw_mu[j]), wlr, w0a0, o_rw_g2[j], row(o_rw_k_k[j]), row(o_rw_k_a[j]),
                row(o_rw_r_k[j]), bd)
            yc = wkv(r, a, v, lw, g, bon, kt, bt, lwt, row(o_rw_ln_w[j]), row(o_rw_ln_b[j]), B, L)
            an = jnp.pad(-jnp.exp(o_mb_a_log[j]), (0, LANE - MB_HEADS)).reshape(1, LANE)
            dtb = jnp.pad(o_mb_dt_bias[j], (0, LANE - MB_HEADS)).reshape(1, LANE)
            ex = (jnp.arange(LANE)[:, None] == (jnp.arange(MB_WIDTH) // MB_HEADDIM)[None, :]).astype(F32)
            dsk = jnp.repeat(o_mb_d[j], MB_HEADDIM).reshape(1, MB_WIDTH)
            yd = mamba(p_z, p_xbc, p_dt, B, L, o_mb_conv_w[j], row(o_mb_conv_b[j]), dtb, an, ex, dsk, row(o_mb_norm[j]))
            x2 = out_proj(x2, yc, yd, o_w_out[j][:W].astype(BF16), o_w_out[j][W:].astype(BF16))
        wup = ffn_w_up[i]
        x2 = ffn(x2, L, row(norm_ffn[i]), wup[:, :D_FF].astype(BF16), wup[:, D_FF:].astype(BF16),
                 ffn_conv_w[i], row(ffn_conv_b[i]), ffn_w_down[i].astype(BF16), row(norm_final),
                 final_norm=(i == depth - 1))
    return x2.reshape(B, L, D)
```

```python
import functools
import math

import jax
import jax.numpy as jnp
from jax import lax
from jax.experimental import pallas as pl
from jax.experimental.pallas import tpu as pltpu

F32 = jnp.float32
BF16 = jnp.bfloat16
HI = lax.Precision.HIGHEST

D_MODEL = 1024
D_FF = 2816
EPS = 1e-6
LANE = 128
HALO = 8

GLA_HEADS = 4
GLA_DK = 128
GLA_DV = 192
GLA_DVP = 256
GLA_RANK = 16
GLA_TAU = 16.0
GLA_CHUNK = 128
S5_WIDTH = 256
S5_GROUPS = 16
S5_GROUP = 16
S5_STATE = 64
S5_NSTATE = S5_GROUPS * S5_STATE
S5_CHUNK = 256
RW_WIDTH = 512
RW_HEAD = 64
RW_COLS = 1792
RW_GN_EPS = 64e-5
RW_CHUNK = 128
MB_WIDTH = 512
MB_HEADS = 8
MB_HEADDIM = 64
MB_GROUPS = 2
MB_STATE = 128
MB_CONV = 4
MB_CHUNK = 128
MB_XBC = 1024
PROJ_COLS = 3456

VMEM_LIMIT = 56 * 1024 * 1024


def _cparams(*sem):
    return pltpu.CompilerParams(dimension_semantics=sem, vmem_limit_bytes=VMEM_LIMIT)


def _dot(a, b, precision=None):
    return jnp.dot(a, b, preferred_element_type=F32, precision=precision)


def _dot_nt(a, b, precision=None):
    return lax.dot_general(a, b, (((1,), (1,)), ((), ())), preferred_element_type=F32, precision=precision)


def _bdot(a, b):
    return _dot(a.astype(BF16), b.astype(BF16))


def _bdot_nt(a, b):
    return _dot_nt(a.astype(BF16), b.astype(BF16))


def _split(x):
    hi = x.astype(BF16)
    return hi, (x - hi.astype(F32)).astype(BF16)


def _dot_x2(x, e):
    hi, lo = _split(x)
    return _dot(hi, e) + _dot(lo, e)


def _dot_2x(e, x):
    hi, lo = _split(x)
    return _dot(e, hi) + _dot(e, lo)


def _dot_x3(a, b):
    ah, al = _split(a)
    bh, bl = _split(b)
    return _dot(ah, bh) + (_dot(al, bh) + _dot(ah, bl))


def _rms(x, w):
    return x * lax.rsqrt(jnp.mean(x * x, axis=-1, keepdims=True) + EPS) * w


def _sigmoid(x):
    return 1.0 / (1.0 + jnp.exp(-x))


def _silu(x):
    return x * _sigmoid(x)


def _softplus(x):
    return jnp.maximum(x, 0.0) + jnp.log(1.0 + jnp.exp(-jnp.abs(x)))


def _iota2(shape, dim):
    return lax.broadcasted_iota(jnp.int32, shape, dim)


def _const_spec(shape):
    nd = len(shape)
    return pl.BlockSpec(shape, lambda *_: (0,) * nd)


def _norm_proj_kernel(x_ref, nw_ref, w_ref, *o_refs):
    h = _rms(x_ref[...], nw_ref[...])
    y = _dot(h.astype(BF16), w_ref[...])
    off = 0
    for o_ref in o_refs:
        n = o_ref.shape[1]
        o_ref[...] = y[:, off:off + n]
        off += n


def norm_proj(x2, nw, w, splits, tm=512):
    T, D = x2.shape
    N = w.shape[1]
    assert sum(splits) == N
    return pl.pallas_call(
        _norm_proj_kernel,
        out_shape=[jax.ShapeDtypeStruct((T, n), F32) for n in splits],
        grid=(T // tm,),
        in_specs=[pl.BlockSpec((tm, D), lambda i: (i, 0)), _const_spec((1, D)), _const_spec((D, N))],
        out_specs=[pl.BlockSpec((tm, n), lambda i: (i, 0)) for n in splits],
        compiler_params=_cparams("parallel"),
        name="norm_proj",
    )(x2, nw, w)


def _out_proj_kernel(x_ref, ya_ref, yb_ref, wa_ref, wb_ref, o_ref):
    o_ref[...] = x_ref[...] + _bdot(ya_ref[...], wa_ref[...]) + _bdot(yb_ref[...], wb_ref[...])


def out_proj(x2, ya, yb, wa, wb, tm=512):
    T, D = x2.shape
    ka, kb = ya.shape[1], yb.shape[1]
    return pl.pallas_call(
        _out_proj_kernel,
        out_shape=jax.ShapeDtypeStruct((T, D), F32),
        grid=(T // tm,),
        in_specs=[pl.BlockSpec((tm, D), lambda i: (i, 0)),
                  pl.BlockSpec((tm, ka), lambda i: (i, 0)),
                  pl.BlockSpec((tm, kb), lambda i: (i, 0)),
                  _const_spec((ka, D)), _const_spec((kb, D))],
        out_specs=pl.BlockSpec((tm, D), lambda i: (i, 0)),
        compiler_params=_cparams("parallel"),
        name="out_proj",
    )(x2, ya, yb, wa, wb)


def _ffn_kernel(x_ref, xh_ref, nw_ref, wg_ref, wu_ref, cw_ref, cb_ref, wd_ref, fnw_ref, o_ref, *,
                tiles_per_seq, final_norm):
    tm = x_ref.shape[0]
    x = x_ref[...]
    xa = jnp.concatenate([xh_ref[...], x], axis=0)
    h = _rms(xa, nw_ref[...]).astype(BF16)
    gate = _dot(h, wg_ref[...])
    halo_dead = jnp.where((pl.program_id(0) % tiles_per_seq) == 0, HALO, 0)
    gate = jnp.where(_iota2((HALO + tm, 1), 0) < halo_dead, 0.0, gate)
    cw = cw_ref[...]
    conv = (cw[2:3] * gate[HALO:]
            + cw[1:2] * pltpu.roll(gate, 1, 0)[HALO:]
            + cw[0:1] * pltpu.roll(gate, 2, 0)[HALO:]) + cb_ref[...]
    up = _dot(h[HALO:], wu_ref[...])
    act = (_silu(conv) * up).astype(BF16)
    y = x + _dot(act, wd_ref[...])
    if final_norm:
        y = _rms(y, fnw_ref[...])
    o_ref[...] = y


def ffn(x2, seq_len, nw, wg, wu, cw, cb, wd, fnw, final_norm, tm=256):
    T, D = x2.shape
    FF = wg.shape[1]
    hb = tm // HALO
    kern = functools.partial(_ffn_kernel, tiles_per_seq=seq_len // tm, final_norm=final_norm)
    return pl.pallas_call(
        kern,
        out_shape=jax.ShapeDtypeStruct((T, D), F32),
        grid=(T // tm,),
        in_specs=[pl.BlockSpec((tm, D), lambda i: (i, 0)),
                  pl.BlockSpec((HALO, D), lambda i: (jnp.maximum(i * hb - 1, 0), 0)),
                  _const_spec((1, D)), _const_spec((D, FF)), _const_spec((D, FF)),
                  _const_spec((3, FF)), _const_spec((1, FF)), _const_spec((FF, D)), _const_spec((1, D))],
        out_specs=pl.BlockSpec((tm, D), lambda i: (i, 0)),
        compiler_params=_cparams("parallel"),
        name="ffn",
    )(x2, x2, nw, wg, wu, cw, cb, wd, fnw)


def _gla_kernel(q_ref, k_ref, v_ref, g_ref, alo_ref, wa_ref, ba_ref, nw_ref, o_ref, st_ref):
    @pl.when(pl.program_id(2) == 0)
    def _():
        st_ref[...] = jnp.zeros_like(st_ref)

    C = GLA_CHUNK
    tq = q_ref.shape[0]
    xg = _bdot(alo_ref[...], wa_ref[0]) + ba_ref[0]
    la = (jnp.minimum(xg, 0.0) - jnp.log(1.0 + jnp.exp(-jnp.abs(xg)))) * (1.0 / GLA_TAU)
    ri = _iota2((C, C), 0)
    ci = _iota2((C, C), 1)
    tril = (ri >= ci)
    tril_b = jnp.where(tril, 1.0, 0.0).astype(BF16)
    scale = GLA_DK ** -0.5
    for j in range(tq // C):
        sl = slice(j * C, (j + 1) * C)
        b = _dot_2x(tril_b, la[sl])
        b_mid = b[C // 2 - 1:C // 2]
        b_last = b[C - 1:C]
        q = q_ref[sl, :] * scale
        k = k_ref[sl, :]
        v = v_ref[sl, :]
        st = st_ref[...]
        qe = q * jnp.exp(b - b_mid)
        ke = k * jnp.exp(jnp.minimum(b_mid - b, 80.0))
        att = jnp.where(tril, _bdot_nt(qe, ke), 0.0)
        o = _bdot(att, v) + _bdot_nt(q * jnp.exp(b), st)
        kd = k * jnp.exp(b_last - b)
        st_ref[...] = st * jnp.exp(b_last) + _bdot(v.T, kd)
        ms = jnp.sum(o * o, axis=-1, keepdims=True) * (1.0 / GLA_DV)
        o = o * lax.rsqrt(ms + EPS) * nw_ref[...]
        o_ref[sl, :] = o * _silu(g_ref[sl, :])


def gla(q, k, v, g, alo, B, L, wa, ba, nw, tq=512):
    T = q.shape[0]
    nq = L // tq
    per_head = lambda w: pl.BlockSpec((tq, w), lambda b, h, c: (b * nq + c, h))
    return pl.pallas_call(
        _gla_kernel,
        out_shape=jax.ShapeDtypeStruct((T, GLA_HEADS * GLA_DVP), F32),
        grid=(B, GLA_HEADS, nq),
        in_specs=[per_head(GLA_DK), per_head(GLA_DK), per_head(GLA_DVP), per_head(GLA_DVP),
                  pl.BlockSpec((tq, LANE), lambda b, h, c: (b * nq + c, 0)),
                  pl.BlockSpec((1, LANE, GLA_DK), lambda b, h, c: (h, 0, 0)),
                  pl.BlockSpec((1, 1, GLA_DK), lambda b, h, c: (h, 0, 0)),
                  _const_spec((1, GLA_DVP))],
        out_specs=per_head(GLA_DVP),
        scratch_shapes=[pltpu.VMEM((GLA_DVP, GLA_DK), F32)],
        compiler_params=_cparams("parallel", "parallel", "arbitrary"),
        name="gla",
    )(q, k, v, g, alo, wa, ba, nw)


def _shift_rows(x, sh):
    n = x.shape[0]
    if sh % HALO == 0:
        return jnp.concatenate([jnp.zeros((sh, x.shape[1]), x.dtype), x[:n - sh]], axis=0)
    rolled = pltpu.roll(x, sh, 0)
    return jnp.where(_iota2((n, 1), 0) < sh, 0.0, rolled)


def _s5_kernel(u_ref, lr_ref, li_ref, br_ref, bi_ref, cr_ref, ci_ref, d_ref, wg_ref, bg_ref, o_ref,
               sr_ref, si_ref):
    @pl.when(pl.program_id(1) == 0)
    def _():
        sr_ref[...] = jnp.zeros_like(sr_ref)
        si_ref[...] = jnp.zeros_like(si_ref)

    tc = u_ref.shape[0]
    u = u_ref[...]
    ub = u.astype(BF16)
    lr = lr_ref[...]
    li = li_ref[...]
    xr = _dot(ub, br_ref[...])
    xi = _dot(ub, bi_ref[...])
    pr = sr_ref[...]
    pi = si_ref[...]
    first = _iota2((tc, 1), 0) == 0
    xr = xr + jnp.where(first, lr * pr - li * pi, 0.0)
    xi = xi + jnp.where(first, lr * pi + li * pr, 0.0)
    mr, mi = lr, li
    sh = 1
    while sh < tc:
        yr = _shift_rows(xr, sh)
        yi = _shift_rows(xi, sh)
        xr, xi = xr + mr * yr - mi * yi, xi + mr * yi + mi * yr
        mr, mi = mr * mr - mi * mi, 2.0 * mr * mi
        sh *= 2
    sr_ref[...] = xr[tc - 1:tc]
    si_ref[...] = xi[tc - 1:tc]
    y = _bdot(xr, cr_ref[...]) - _bdot(xi, ci_ref[...]) + d_ref[...] * u
    z = 0.5 * y * (1.0 + jnp.tanh(math.sqrt(2.0 / math.pi) * (y + 0.044715 * (y * y * y))))
    o_ref[...] = z * _sigmoid(_bdot(z, wg_ref[...]) + bg_ref[...])


def s5(u, B, L, lr, li, br, bi, cr, ci, d, wg, bg, tc=S5_CHUNK):
    T = u.shape[0]
    nc = L // tc
    NS = S5_NSTATE
    return pl.pallas_call(
        _s5_kernel,
        out_shape=jax.ShapeDtypeStruct((T, S5_WIDTH), F32),
        grid=(B, nc),
        in_specs=[pl.BlockSpec((tc, S5_WIDTH), lambda b, c: (b * nc + c, 0)),
                  _const_spec((1, NS)), _const_spec((1, NS)),
                  _const_spec((S5_WIDTH, NS)), _const_spec((S5_WIDTH, NS)),
                  _const_spec((NS, S5_WIDTH)), _const_spec((NS, S5_WIDTH)),
                  _const_spec((1, S5_WIDTH)), _const_spec((S5_WIDTH, S5_WIDTH)), _const_spec((1, S5_WIDTH))],
        out_specs=pl.BlockSpec((tc, S5_WIDTH), lambda b, c: (b * nc + c, 0)),
        scratch_shapes=[pltpu.VMEM((1, NS), F32), pltpu.VMEM((1, NS), F32)],
        compiler_params=_cparams("parallel", "arbitrary"),
        name="s5",
    )(u, lr, li, br, bi, cr, ci, d, wg, bg)


def _rwkv_prep_kernel(p_ref, ph_ref, mu_ref, wlr_ref, w0a0_ref, g2_ref, kk_ref, ka_ref, rk_ref, bd_ref,
                      r_ref, a_ref, v_ref, lw_ref, g_ref, bon_ref, kt_ref, bt_ref, lwt_ref, *, tiles_per_seq):
    tm = p_ref.shape[0]
    W = RW_WIDTH
    p = p_ref[...]
    pa = jnp.concatenate([ph_ref[...], p], axis=0)
    prev = pltpu.roll(pa, 1, 0)[HALO:]
    first_dead = jnp.where((pl.program_id(0) % tiles_per_seq) == 0, 1, 0)
    prev = jnp.where(_iota2((tm, 1), 0) < first_dead, 0.0, prev)
    pm = p + (prev - p) * mu_ref[...]
    r = pm[:, :W]
    k = pm[:, W:2 * W]
    v = pm[:, 2 * W:3 * W]
    xwa = pm[:, 3 * W:3 * W + LANE]
    xg = pm[:, 3 * W + LANE:]
    xwa = jnp.where(_iota2((tm, LANE), 1) < 64, jnp.tanh(xwa), xwa)
    wa = _dot_x3(xwa, wlr_ref[...]) + w0a0_ref[...]
    wlog = -_softplus(-wa[:, :W]) - 0.5
    lw = -jnp.exp(wlog)
    a = _sigmoid(wa[:, W:])
    g = _bdot(_sigmoid(xg), g2_ref[...])
    bd = bd_ref[...]
    kk = k * kk_ref[...]
    kk = kk / jnp.maximum(jnp.sqrt(_dot_x2(kk * kk, bd)), 1e-12)
    k2 = k * (1.0 + (a - 1.0) * ka_ref[...])
    r_ref[...] = r
    a_ref[...] = -kk
    v_ref[...] = v
    lw_ref[...] = lw
    g_ref[...] = g
    bon_ref[...] = _dot_x2(r * k2 * rk_ref[...], bd) * v
    kt_ref[...] = k2.T
    bt_ref[...] = (kk * a).T
    lwt_ref[...] = lw.T


def rwkv_prep(p, L, mu, wlr, w0a0, g2, kk, ka, rk, bd, tm=256):
    T = p.shape[0]
    W = RW_WIDTH
    hb = tm // HALO
    tok = jax.ShapeDtypeStruct((T, W), F32)
    chn = jax.ShapeDtypeStruct((W, T), F32)
    tok_spec = pl.BlockSpec((tm, W), lambda i: (i, 0))
    chn_spec = pl.BlockSpec((W, tm), lambda i: (0, i))
    kern = functools.partial(_rwkv_prep_kernel, tiles_per_seq=L // tm)
    return pl.pallas_call(
        kern,
        out_shape=[tok] * 6 + [chn] * 3,
        grid=(T // tm,),
        in_specs=[pl.BlockSpec((tm, RW_COLS), lambda i: (i, 0)),
                  pl.BlockSpec((HALO, RW_COLS), lambda i: (jnp.maximum(i * hb - 1, 0), 0)),
                  _const_spec((1, RW_COLS)), _const_spec((LANE, 2 * W)), _const_spec((1, 2 * W)),
                  _const_spec((LANE, W)), _const_spec((1, W)), _const_spec((1, W)), _const_spec((1, W)),
                  _const_spec((W, W))],
        out_specs=[tok_spec] * 6 + [chn_spec] * 3,
        compiler_params=_cparams("parallel"),
        name="rwkv_prep",
    )(p, p, mu, wlr, w0a0, g2, kk, ka, rk, bd)


def _wkv_kernel(r_ref, a_ref, v_ref, lw_ref, g_ref, bon_ref, kt_ref, bt_ref, lwt_ref, lnw_ref, lnb_ref,
                o_ref, h_ref):
    @pl.when(pl.program_id(2) == 0)
    def _():
        h_ref[...] = jnp.zeros_like(h_ref)

    C = RW_CHUNK
    tq = r_ref.shape[0]
    ri = _iota2((C, C), 0)
    ci = _iota2((C, C), 1)
    incl = ri >= ci
    strict = ri > ci
    tril_b = jnp.where(incl, 1.0, 0.0).astype(BF16)
    triu_b = jnp.where(ri <= ci, 1.0, 0.0).astype(BF16)
    m0 = ci < RW_HEAD
    bdm = (ri // RW_HEAD) == (ci // RW_HEAD)
    gn = jnp.where(bdm, 1.0 / RW_HEAD, 0.0).astype(BF16)
    for j in range(tq // C):
        sl = slice(j * C, (j + 1) * C)
        r = r_ref[sl, :]
        a = a_ref[sl, :]
        v = v_ref[sl, :]
        lw = lw_ref[sl, :]
        kt = kt_ref[:, sl]
        bt = bt_ref[:, sl]
        lwt = lwt_ref[:, sl]
        h = h_ref[...]
        c = _dot_2x(tril_b, lw)
        ct = _dot_x2(lwt, triu_b)
        c_mid = c[C // 2 - 1:C // 2]
        ct_mid = ct[:, C // 2 - 1:C // 2]
        ct_last = ct[:, C - 1:C]
        at = a * jnp.exp(c - lw - c_mid)
        rt = r * jnp.exp(c - c_mid)
        e_mid = jnp.exp(ct_mid - ct)
        e_last = jnp.exp(ct_last - ct)
        lhs = jnp.concatenate([jnp.where(m0, at, 0.0), jnp.where(m0, 0.0, at),
                               jnp.where(m0, rt, 0.0), jnp.where(m0, 0.0, rt)], axis=0)
        rhs = jnp.concatenate([bt * e_mid, kt * e_mid], axis=1)
        aa = _bdot(lhs, rhs)
        n0 = jnp.where(strict, aa[0:C, :C], 0.0)
        n1 = jnp.where(strict, aa[C:2 * C, :C], 0.0)
        ak0 = jnp.where(strict, aa[0:C, C:], 0.0)
        ak1 = jnp.where(strict, aa[C:2 * C, C:], 0.0)
        rb0 = jnp.where(incl, aa[2 * C:3 * C, :C], 0.0)
        rb1 = jnp.where(incl, aa[3 * C:, :C], 0.0)
        rk0 = jnp.where(incl, aa[2 * C:3 * C, C:], 0.0)
        rk1 = jnp.where(incl, aa[3 * C:, C:], 0.0)
        hh = _bdot(jnp.concatenate([a * jnp.exp(c - lw), r * jnp.exp(c)], axis=0), h)
        vb = v.astype(BF16)
        x = hh[:C] + jnp.where(m0, _bdot(ak0, vb), _bdot(ak1, vb))
        sh = 1
        while sh < C:
            n0b = n0.astype(BF16)
            n1b = n1.astype(BF16)
            xb = x.astype(BF16)
            x = x + jnp.where(m0, _dot(n0b, xb), _dot(n1b, xb))
            sh *= 2
            if sh < C:
                n0 = _dot(n0b, n0b)
                n1 = _dot(n1b, n1b)
        xb = x.astype(BF16)
        y = hh[C:] + jnp.where(m0, _bdot(rb0, xb) + _bdot(rk0, vb), _bdot(rb1, xb) + _bdot(rk1, vb))
        upd = _bdot(jnp.concatenate([bt * e_last, kt * e_last], axis=1), jnp.concatenate([xb, vb], axis=0))
        h_ref[...] = h * jnp.exp(ct_last) + jnp.where(bdm, upd, 0.0)
        mean = _dot_x2(y, gn)
        yc = y - mean
        var = _dot_x2(yc * yc, gn)
        yn = yc * lax.rsqrt(var + RW_GN_EPS) * lnw_ref[...] + lnb_ref[...]
        o_ref[sl, :] = (yn + bon_ref[sl, :]) * g_ref[sl, :]


def wkv(r, a, v, lw, g, bon, kt, bt, lwt, lnw, lnb, B, L, tq=256):
    T = r.shape[0]
    nq = L // tq
    npair = RW_WIDTH // LANE
    tok = pl.BlockSpec((tq, LANE), lambda b, h, c: (b * nq + c, h))
    chn = pl.BlockSpec((LANE, tq), lambda b, h, c: (h, b * nq + c))
    vec = pl.BlockSpec((1, LANE), lambda b, h, c: (0, h))
    return pl.pallas_call(
        _wkv_kernel,
        out_shape=jax.ShapeDtypeStruct((T, RW_WIDTH), F32),
        grid=(B, npair, nq),
        in_specs=[tok] * 6 + [chn] * 3 + [vec, vec],
        out_specs=tok,
        scratch_shapes=[pltpu.VMEM((LANE, LANE), F32)],
        compiler_params=_cparams("parallel", "parallel", "arbitrary"),
        name="wkv7",
    )(r, a, v, lw, g, bon, kt, bt, lwt, lnw, lnb)


def _mamba_kernel(z_ref, xbc_ref, xh_ref, dt_ref, cw_ref, cb_ref, dtb_ref, an_ref, ex_ref, dsk_ref, nw_ref,
                  o_ref, st_ref):
    @pl.when(pl.program_id(1) == 0)
    def _():
        st_ref[...] = jnp.zeros_like(st_ref)

    C = MB_CHUNK
    W = MB_WIDTH
    N = MB_STATE
    xbc = xbc_ref[...]
    xa = jnp.concatenate([xh_ref[...], xbc], axis=0)
    halo_dead = jnp.where(pl.program_id(1) == 0, HALO, 0)
    xa = jnp.where(_iota2((HALO + C, 1), 0) < halo_dead, 0.0, xa)
    cw = cw_ref[...]
    conv = cw[3:4] * xbc + cb_ref[...]
    for kk in range(1, MB_CONV):
        conv = conv + cw[3 - kk:4 - kk] * pltpu.roll(xa, kk, 0)[HALO:]
    xbc = _silu(conv)
    xs = xbc[:, :W]
    dt = _softplus(dt_ref[...] + dtb_ref[...])
    a = dt * an_ref[...]
    ri = _iota2((C, C), 0)
    ci = _iota2((C, C), 1)
    causal = ri >= ci
    acs = _dot_2x(jnp.where(causal, 1.0, 0.0).astype(BF16), a)
    acs_t = acs.T
    ex = ex_ref[...]
    acs_x = _dot_x2(acs, ex)
    dt_x = _dot_x2(dt, ex)
    xdt = xs * dt_x
    st = st_ref[...]
    lane_w = _iota2((C, W), 1)
    y = jnp.zeros((C, W), F32)
    for gi in range(MB_GROUPS):
        bm = xbc[:, W + gi * N:W + (gi + 1) * N]
        cm = xbc[:, W + MB_GROUPS * N + gi * N:W + MB_GROUPS * N + (gi + 1) * N]
        cb = _bdot_nt(cm, bm)
        hpg = MB_HEADS // MB_GROUPS
        for e in range(hpg):
            hd = gi * hpg + e
            lmat = jnp.exp(jnp.where(causal, acs[:, hd:hd + 1] - acs_t[hd:hd + 1, :], -jnp.inf))
            hm = (lane_w // MB_HEADDIM) == hd
            y = y + _bdot(cb * lmat, jnp.where(hm, xdt, 0.0))
        gw = W // MB_GROUPS
        off = _bdot_nt(cm, st[gi * gw:(gi + 1) * gw])
        gm = (lane_w // gw) == gi
        offp = jnp.concatenate([off] * MB_GROUPS, axis=1)
        y = y + jnp.where(gm, offp * jnp.exp(acs_x), 0.0)
    acs_last = acs_x[C - 1:C]
    xw_t = (xdt * jnp.exp(acs_last - acs_x)).T
    acs_xt = acs_x.T
    chunk_decay = jnp.exp(acs_xt[:, C - 1:C])
    gw = W // MB_GROUPS
    new = [_bdot(xw_t[gi * gw:(gi + 1) * gw], xbc[:, W + gi * N:W + (gi + 1) * N]) for gi in range(MB_GROUPS)]
    st_ref[...] = chunk_decay * st + jnp.concatenate(new, axis=0)
    y = (y + xs * dsk_ref[...]) * _silu(z_ref[...])
    parts = []
    for gi in range(MB_GROUPS):
        yg = y[:, gi * gw:(gi + 1) * gw]
        parts.append(yg * lax.rsqrt(jnp.mean(yg * yg, axis=-1, keepdims=True) + EPS))
    o_ref[...] = jnp.concatenate(parts, axis=1) * nw_ref[...]


def mamba(z, xbc, dt, B, L, cw, cb, dtb, an, ex, dsk, nw):
    T = z.shape[0]
    C = MB_CHUNK
    nc = L // C
    hb = C // HALO
    return pl.pallas_call(
        _mamba_kernel,
        out_shape=jax.ShapeDtypeStruct((T, MB_WIDTH), F32),
        grid=(B, nc),
        in_specs=[pl.BlockSpec((C, MB_WIDTH), lambda b, c: (b * nc + c, 0)),
                  pl.BlockSpec((C, MB_XBC), lambda b, c: (b * nc + c, 0)),
                  pl.BlockSpec((HALO, MB_XBC), lambda b, c: (jnp.maximum((b * nc + c) * hb - 1, 0), 0)),
                  pl.BlockSpec((C, LANE), lambda b, c: (b * nc + c, 0)),
                  _const_spec((MB_CONV, MB_XBC)), _const_spec((1, MB_XBC)), _const_spec((1, LANE)),
                  _const_spec((1, LANE)), _const_spec((LANE, MB_WIDTH)), _const_spec((1, MB_WIDTH)),
                  _const_spec((1, MB_WIDTH))],
        out_specs=pl.BlockSpec((C, MB_WIDTH), lambda b, c: (b * nc + c, 0)),
        scratch_shapes=[pltpu.VMEM((MB_WIDTH, MB_STATE), F32)],
        compiler_params=_cparams("parallel", "arbitrary"),
        name="mamba2",
    )(z, xbc, xbc, dt, cw, cb, dtb, an, ex, dsk, nw)


def _pack_even(w_in, wa2, ba, gnorm, w_out):
    nk = GLA_HEADS * GLA_DK
    nv = GLA_HEADS * GLA_DV
    pad_v = lambda w: jnp.pad(w.reshape(D_MODEL, GLA_HEADS, GLA_DV),
                              ((0, 0), (0, 0), (0, GLA_DVP - GLA_DV))).reshape(D_MODEL, GLA_HEADS * GLA_DVP)
    wq = w_in[:, :2 * nk]
    wv = pad_v(w_in[:, 2 * nk:2 * nk + nv])
    wg = pad_v(w_in[:, 2 * nk + nv:2 * nk + 2 * nv])
    walo = jnp.pad(w_in[:, 2 * nk + 2 * nv:2 * nk + 2 * nv + GLA_RANK], ((0, 0), (0, LANE - GLA_RANK)))
    wu = w_in[:, 2 * nk + 2 * nv + GLA_RANK:]
    w_cat = jnp.concatenate([wq, wv, wg, wu, walo], axis=1).astype(BF16)
    wa = jnp.pad(wa2.reshape(GLA_RANK, GLA_HEADS, GLA_DK).transpose(1, 0, 2),
                 ((0, 0), (0, LANE - GLA_RANK), (0, 0)))
    bah = ba.reshape(GLA_HEADS, 1, GLA_DK)
    nw = jnp.pad(gnorm, (0, GLA_DVP - GLA_DV)).reshape(1, GLA_DVP)
    woa = jnp.pad(w_out[:nv].reshape(GLA_HEADS, GLA_DV, D_MODEL),
                  ((0, 0), (0, GLA_DVP - GLA_DV), (0, 0))).reshape(GLA_HEADS * GLA_DVP, D_MODEL).astype(BF16)
    wob = w_out[nv:].astype(BF16)
    return w_cat, wa, bah, nw, woa, wob


def _pack_s5(lam_re, lam_im, log_dt, b_re, b_im, c_re, c_im):
    dt = jnp.exp(log_dt)[:, None]
    mag = jnp.exp(lam_re * dt)
    ang = lam_im * dt
    lb_re = mag * jnp.cos(ang)
    lb_im = mag * jnp.sin(ang)
    den = lam_re * lam_re + lam_im * lam_im
    nr = lb_re - 1.0
    f_re = (nr * lam_re + lb_im * lam_im) / den
    f_im = (lb_im * lam_re - nr * lam_im) / den
    bb_re = f_re[..., None] * b_re - f_im[..., None] * b_im
    bb_im = f_re[..., None] * b_im + f_im[..., None] * b_re
    eye = jnp.eye(S5_GROUPS, dtype=F32)
    bd_in = lambda m: jnp.einsum('gpc,gh->gchp', m, eye).reshape(S5_WIDTH, S5_NSTATE).astype(BF16)
    bd_out = lambda m: jnp.einsum('gcp,gh->gphc', m, eye).reshape(S5_NSTATE, S5_WIDTH).astype(BF16)
    return (lb_re.reshape(1, S5_NSTATE), lb_im.reshape(1, S5_NSTATE),
            bd_in(bb_re), bd_in(bb_im), bd_out(c_re), bd_out(c_im))


def _pack_odd(w_in, w0, w2, a0, a2):
    w_cat = jnp.pad(w_in, ((0, 0), (0, PROJ_COLS - w_in.shape[1]))).astype(BF16)
    W = RW_WIDTH
    wlr = jnp.zeros((LANE, 2 * W), F32).at[:64, :W].set(w2).at[64:, W:].set(a2)
    w0a0 = jnp.concatenate([w0, a0]).reshape(1, 2 * W)
    return w_cat, wlr, w0a0


def kernel(x, norm_mix, norm_ffn, norm_final, e_w_in, e_gla_wa2, e_gla_ba, e_gla_norm, e_s5_lambda_re, e_s5_lambda_im, e_s5_log_dt, e_s5_b_re, e_s5_b_im, e_s5_c_re, e_s5_c_im, e_s5_d, e_s5_w_glu, e_s5_b_glu, e_w_out, o_w_in, o_rw_mu, o_rw_w0, o_rw_w2, o_rw_a0, o_rw_a2, o_rw_g2, o_rw_k_k, o_rw_k_a, o_rw_r_k, o_rw_ln_w, o_rw_ln_b, o_mb_conv_w, o_mb_conv_b, o_mb_dt_bias, o_mb_a_log, o_mb_d, o_mb_norm, o_w_out, ffn_w_up, ffn_conv_w, ffn_conv_b, ffn_w_down):
    B, L, D = x.shape
    T = B * L
    depth = norm_mix.shape[0]
    x2 = x.reshape(T, D)
    row = lambda t: t.reshape(1, -1)
    for i in range(depth):
        j = i // 2
        if i % 2 == 0:
            w_cat, wa, bah, gnw, woa, wob = _pack_even(e_w_in[j], e_gla_wa2[j], e_gla_ba[j], e_gla_norm[j],
                                                       e_w_out[j])
            nkq = GLA_HEADS * GLA_DK
            nvp = GLA_HEADS * GLA_DVP
            q, k, v, g, u, alo = norm_proj(x2, row(norm_mix[i]), w_cat, (nkq, nkq, nvp, nvp, S5_WIDTH, LANE))
            ya = gla(q, k, v, g, alo, B, L, wa, bah, gnw)
            lr, li, br, bi, cr, ci = _pack_s5(e_s5_lambda_re[j], e_s5_lambda_im[j], e_s5_log_dt[j],
                                              e_s5_b_re[j], e_s5_b_im[j], e_s5_c_re[j], e_s5_c_im[j])
            yb = s5(u, B, L, lr, li, br, bi, cr, ci, row(e_s5_d[j]), e_s5_w_glu[j].astype(BF16),
                    row(e_s5_b_glu[j]))
            x2 = out_proj(x2, ya, yb, woa, wob)
        else:
            W = RW_WIDTH
            w_cat, wlr, w0a0 = _pack_odd(o_w_in[j], o_rw_w0[j], o_rw_w2[j], o_rw_a0[j], o_rw_a2[j])
            p_rw, p_z, p_xbc, p_dt = norm_proj(x2, row(norm_mix[i]), w_cat, (RW_COLS, MB_WIDTH, MB_XBC, LANE))
            head_of = jnp.arange(W) // RW_HEAD
            bd = (head_of[:, None] == head_of[None, :]).astype(BF16)
            r, a, v, lw, g, bon, kt, bt, lwt = rwkv_prep(
                p_rw, L, row(o_rw_mu[j]), wlr, w0a0, o_rw_g2[j], row(o_rw_k_k[j]), row(o_rw_k_a[j]),
                row(o_rw_r_k[j]), bd)
            yc = wkv(r, a, v, lw, g, bon, kt, bt, lwt, row(o_rw_ln_w[j]), row(o_rw_ln_b[j]), B, L)
            an = jnp.pad(-jnp.exp(o_mb_a_log[j]), (0, LANE - MB_HEADS)).reshape(1, LANE)
            dtb = jnp.pad(o_mb_dt_bias[j], (0, LANE - MB_HEADS)).reshape(1, LANE)
            ex = (jnp.arange(LANE)[:, None] == (jnp.arange(MB_WIDTH) // MB_HEADDIM)[None, :]).astype(BF16)
            dsk = jnp.repeat(o_mb_d[j], MB_HEADDIM).reshape(1, MB_WIDTH)
            yd = mamba(p_z, p_xbc, p_dt, B, L, o_mb_conv_w[j], row(o_mb_conv_b[j]), dtb, an, ex, dsk, row(o_mb_norm[j]))
            x2 = out_proj(x2, yc, yd, o_w_out[j][:W].astype(BF16), o_w_out[j][W:].astype(BF16))
        wup = ffn_w_up[i]
        x2 = ffn(x2, L, row(norm_ffn[i]), wup[:, :D_FF].astype(BF16), wup[:, D_FF:].astype(BF16),
                 ffn_conv_w[i], row(ffn_conv_b[i]), ffn_w_down[i].astype(BF16), row(norm_final),
                 final_norm=(i == depth - 1))
    return x2.reshape(B, L, D)
```

```python
import functools
import math

import jax
import jax.numpy as jnp
from jax import lax
from jax.experimental import pallas as pl
from jax.experimental.pallas import tpu as pltpu

F32 = jnp.float32
BF16 = jnp.bfloat16
HI = lax.Precision.HIGHEST

D_MODEL = 1024
D_FF = 2816
EPS = 1e-6
LANE = 128
HALO = 8

GLA_HEADS = 4
GLA_DK = 128
GLA_DV = 192
GLA_DVP = 256
GLA_RANK = 16
GLA_TAU = 16.0
GLA_CHUNK = 128
S5_WIDTH = 256
S5_GROUPS = 16
S5_GROUP = 16
S5_STATE = 64
S5_NSTATE = S5_GROUPS * S5_STATE
S5_CHUNK = 256
RW_WIDTH = 512
RW_HEAD = 64
RW_COLS = 1792
RW_GN_EPS = 64e-5
RW_CHUNK = 128
MB_WIDTH = 512
MB_HEADS = 8
MB_HEADDIM = 64
MB_GROUPS = 2
MB_STATE = 128
MB_CONV = 4
MB_CHUNK = 128
MB_XBC = 1024
PROJ_COLS = 3456

VMEM_LIMIT = 56 * 1024 * 1024


def _cparams(*sem):
    return pltpu.CompilerParams(dimension_semantics=sem, vmem_limit_bytes=VMEM_LIMIT)


def _dot(a, b, precision=None):
    return jnp.dot(a, b, preferred_element_type=F32, precision=precision)


def _dot_nt(a, b, precision=None):
    return lax.dot_general(a, b, (((1,), (1,)), ((), ())), preferred_element_type=F32, precision=precision)


def _bdot(a, b):
    return _dot(a.astype(BF16), b.astype(BF16))


def _bdot_nt(a, b):
    return _dot_nt(a.astype(BF16), b.astype(BF16))


def _split(x):
    hi = x.astype(BF16)
    return hi, (x - hi.astype(F32)).astype(BF16)


def _dot_x2(x, e):
    hi, lo = _split(x)
    return _dot(hi, e) + _dot(lo, e)


def _dot_2x(e, x):
    hi, lo = _split(x)
    return _dot(e, hi) + _dot(e, lo)


def _dot_x3(a, b):
    ah, al = _split(a)
    bh, bl = _split(b)
    return _dot(ah, bh) + (_dot(al, bh) + _dot(ah, bl))


def _rms(x, w):
    return x * lax.rsqrt(jnp.mean(x * x, axis=-1, keepdims=True) + EPS) * w


def _sigmoid(x):
    return 1.0 / (1.0 + jnp.exp(-x))


def _silu(x):
    return x * _sigmoid(x)


def _softplus(x):
    return jnp.maximum(x, 0.0) + jnp.log(1.0 + jnp.exp(-jnp.abs(x)))


def _iota2(shape, dim):
    return lax.broadcasted_iota(jnp.int32, shape, dim)


def _const_spec(shape):
    nd = len(shape)
    return pl.BlockSpec(shape, lambda *_: (0,) * nd)


def _norm_proj_kernel(x_ref, nw_ref, w_ref, *o_refs):
    h = _rms(x_ref[...], nw_ref[...])
    y = _dot(h.astype(BF16), w_ref[...])
    off = 0
    for o_ref in o_refs:
        n = o_ref.shape[1]
        o_ref[...] = y[:, off:off + n]
        off += n


def norm_proj(x2, nw, w, splits, tm=512):
    T, D = x2.shape
    N = w.shape[1]
    assert sum(splits) == N
    return pl.pallas_call(
        _norm_proj_kernel,
        out_shape=[jax.ShapeDtypeStruct((T, n), F32) for n in splits],
        grid=(T // tm,),
        in_specs=[pl.BlockSpec((tm, D), lambda i: (i, 0)), _const_spec((1, D)), _const_spec((D, N))],
        out_specs=[pl.BlockSpec((tm, n), lambda i: (i, 0)) for n in splits],
        compiler_params=_cparams("parallel"),
        name="norm_proj",
    )(x2, nw, w)


def _ffn_kernel(x_ref, xh_ref, ya_ref, yah_ref, yb_ref, ybh_ref, wa_ref, wb_ref, nw_ref, wg_ref, wu_ref,
                cw_ref, cb_ref, wd_ref, fnw_ref, o_ref, *, tiles_per_seq, final_norm):
    tm = x_ref.shape[0]
    ya = jnp.concatenate([yah_ref[...], ya_ref[...]], axis=0).astype(BF16)
    yb = jnp.concatenate([ybh_ref[...], yb_ref[...]], axis=0).astype(BF16)
    xa = (jnp.concatenate([xh_ref[...], x_ref[...]], axis=0)
          + _dot(ya, wa_ref[...]) + _dot(yb, wb_ref[...]))
    x = xa[HALO:]
    h = _rms(xa, nw_ref[...]).astype(BF16)
    gate = _dot(h, wg_ref[...])
    halo_dead = jnp.where((pl.program_id(0) % tiles_per_seq) == 0, HALO, 0)
    gate = jnp.where(_iota2((HALO + tm, 1), 0) < halo_dead, 0.0, gate)
    cw = cw_ref[...]
    conv = (cw[2:3] * gate[HALO:]
            + cw[1:2] * pltpu.roll(gate, 1, 0)[HALO:]
            + cw[0:1] * pltpu.roll(gate, 2, 0)[HALO:]) + cb_ref[...]
    up = _dot(h[HALO:], wu_ref[...])
    act = (_silu(conv) * up).astype(BF16)
    y = x + _dot(act, wd_ref[...])
    if final_norm:
        y = _rms(y, fnw_ref[...])
    o_ref[...] = y


def ffn(x2, ya, yb, seq_len, wa, wb, nw, wg, wu, cw, cb, wd, fnw, final_norm, tm=256):
    T, D = x2.shape
    FF = wg.shape[1]
    ka, kb = ya.shape[1], yb.shape[1]
    hb = tm // HALO
    tile = lambda n: pl.BlockSpec((tm, n), lambda i: (i, 0))
    halo = lambda n: pl.BlockSpec((HALO, n), lambda i: (jnp.maximum(i * hb - 1, 0), 0))
    kern = functools.partial(_ffn_kernel, tiles_per_seq=seq_len // tm, final_norm=final_norm)
    return pl.pallas_call(
        kern,
        out_shape=jax.ShapeDtypeStruct((T, D), F32),
        grid=(T // tm,),
        in_specs=[tile(D), halo(D), tile(ka), halo(ka), tile(kb), halo(kb),
                  _const_spec((ka, D)), _const_spec((kb, D)),
                  _const_spec((1, D)), _const_spec((D, FF)), _const_spec((D, FF)),
                  _const_spec((3, FF)), _const_spec((1, FF)), _const_spec((FF, D)), _const_spec((1, D))],
        out_specs=tile(D),
        compiler_params=_cparams("parallel"),
        name="ffn",
    )(x2, x2, ya, ya, yb, yb, wa, wb, nw, wg, wu, cw, cb, wd, fnw)


def _gla_kernel(q_ref, k_ref, v_ref, g_ref, alo_ref, wa_ref, ba_ref, nw_ref, o_ref, st_ref):
    @pl.when(pl.program_id(2) == 0)
    def _():
        st_ref[...] = jnp.zeros_like(st_ref)

    C = GLA_CHUNK
    tq = q_ref.shape[0]
    xg = _bdot(alo_ref[...], wa_ref[0]) + ba_ref[0]
    la = (jnp.minimum(xg, 0.0) - jnp.log(1.0 + jnp.exp(-jnp.abs(xg)))) * (1.0 / GLA_TAU)
    ri = _iota2((C, C), 0)
    ci = _iota2((C, C), 1)
    tril = (ri >= ci)
    tril_b = jnp.where(tril, 1.0, 0.0).astype(BF16)
    scale = GLA_DK ** -0.5
    o_in, kdv, qd, dec = [], [], [], []
    sls = [slice(j * C, (j + 1) * C) for j in range(tq // C)]
    for sl in sls:
        b = _dot_2x(tril_b, la[sl])
        b_mid = b[C // 2 - 1:C // 2]
        b_last = b[C - 1:C]
        q = q_ref[sl, :] * scale
        k = k_ref[sl, :]
        v = v_ref[sl, :]
        qe = q * jnp.exp(b - b_mid)
        ke = k * jnp.exp(jnp.minimum(b_mid - b, 80.0))
        att = jnp.where(tril, _bdot_nt(qe, ke), 0.0)
        o_in.append(_bdot(att, v))
        kdv.append(_bdot(v.T, k * jnp.exp(b_last - b)))
        qd.append((q * jnp.exp(b)).astype(BF16))
        dec.append(jnp.exp(b_last))
    st = st_ref[...]
    sts = []
    for j in range(len(sls)):
        sts.append(st.astype(BF16))
        st = st * dec[j] + kdv[j]
    st_ref[...] = st
    for j, sl in enumerate(sls):
        o = o_in[j] + _dot_nt(qd[j], sts[j])
        ms = jnp.sum(o * o, axis=-1, keepdims=True) * (1.0 / GLA_DV)
        o = o * lax.rsqrt(ms + EPS) * nw_ref[...]
        o_ref[sl, :] = o * _silu(g_ref[sl, :])


def gla(q, k, v, g, alo, B, L, wa, ba, nw, tq=512):
    T = q.shape[0]
    nq = L // tq
    per_head = lambda w: pl.BlockSpec((tq, w), lambda b, h, c: (b * nq + c, h))
    return pl.pallas_call(
        _gla_kernel,
        out_shape=jax.ShapeDtypeStruct((T, GLA_HEADS * GLA_DVP), F32),
        grid=(B, GLA_HEADS, nq),
        in_specs=[per_head(GLA_DK), per_head(GLA_DK), per_head(GLA_DVP), per_head(GLA_DVP),
                  pl.BlockSpec((tq, LANE), lambda b, h, c: (b * nq + c, 0)),
                  pl.BlockSpec((1, LANE, GLA_DK), lambda b, h, c: (h, 0, 0)),
                  pl.BlockSpec((1, 1, GLA_DK), lambda b, h, c: (h, 0, 0)),
                  _const_spec((1, GLA_DVP))],
        out_specs=per_head(GLA_DVP),
        scratch_shapes=[pltpu.VMEM((GLA_DVP, GLA_DK), F32)],
        compiler_params=_cparams("parallel", "parallel", "arbitrary"),
        name="gla",
    )(q, k, v, g, alo, wa, ba, nw)


def _cmul(ar, ai, br, bi):
    return ar * br - ai * bi, ar * bi + ai * br


def _s5_kernel(u_ref, perm_ref, unperm_ref, lr_ref, li_ref, br_ref, bi_ref, cr_ref, ci_ref, d_ref, wg_ref,
               bg_ref, o_ref, sr_ref, si_ref, pr_ref, pi_ref, xr_ref, xi_ref):
    tc = u_ref.shape[0]
    seg = tc // HALO
    ns = lr_ref.shape[1]
    lr = lr_ref[...]
    li = li_ref[...]
    lr8 = jnp.broadcast_to(lr, (HALO, ns))
    li8 = jnp.broadcast_to(li, (HALO, ns))
    rows = lambda j: slice(j * HALO, (j + 1) * HALO)

    @pl.when(pl.program_id(1) == 0)
    def _():
        sr_ref[...] = jnp.zeros_like(sr_ref)
        si_ref[...] = jnp.zeros_like(si_ref)
        zr, zi = lr8, li8
        for j in range(seg):
            pr_ref[rows(j), :] = zr
            pi_ref[rows(j), :] = zi
            zr, zi = _cmul(lr8, li8, zr, zi)

    u = u_ref[...]
    ub = _dot(perm_ref[...], u.astype(BF16)).astype(BF16)
    xr_ref[...] = _dot(ub, br_ref[...])
    xi_ref[...] = _dot(ub, bi_ref[...])
    er = xr_ref[rows(0), :]
    ei = xi_ref[rows(0), :]
    for j in range(1, seg):
        tr, ti = _cmul(lr8, li8, er, ei)
        er = xr_ref[rows(j), :] + tr
        ei = xi_ref[rows(j), :] + ti
        xr_ref[rows(j), :] = er
        xi_ref[rows(j), :] = ei
    mr, mi = lr, li
    for _ in range(seg.bit_length() - 1):
        mr, mi = _cmul(mr, mi, mr, mi)
    row8 = _iota2((HALO, 1), 0)
    cr0, ci0 = _cmul(mr, mi, sr_ref[...], si_ref[...])
    er = er + jnp.where(row8 == 0, cr0, 0.0)
    ei = ei + jnp.where(row8 == 0, ci0, 0.0)
    sh = 1
    while sh < HALO:
        yr, yi = _cmul(mr, mi, jnp.where(row8 < sh, 0.0, pltpu.roll(er, sh, 0)),
                       jnp.where(row8 < sh, 0.0, pltpu.roll(ei, sh, 0)))
        er, ei = er + yr, ei + yi
        mr, mi = _cmul(mr, mi, mr, mi)
        sh *= 2
    inr = jnp.where(row8 == 0, sr_ref[...], pltpu.roll(er, 1, 0))
    ini = jnp.where(row8 == 0, si_ref[...], pltpu.roll(ei, 1, 0))
    sr_ref[...] = er[HALO - 1:]
    si_ref[...] = ei[HALO - 1:]
    ar, ai = _cmul(pr_ref[...].reshape(seg, HALO, ns), pi_ref[...].reshape(seg, HALO, ns), inr[None], ini[None])
    xr = xr_ref[...] + ar.reshape(tc, ns)
    xi = xi_ref[...] + ai.reshape(tc, ns)
    yp = _bdot(xr, cr_ref[...]) - _bdot(xi, ci_ref[...])
    y = _dot_2x(unperm_ref[...], yp) + d_ref[...] * u
    z = 0.5 * y * (1.0 + jnp.tanh(math.sqrt(2.0 / math.pi) * (y + 0.044715 * (y * y * y))))
    o_ref[...] = z * _sigmoid(_bdot(z, wg_ref[...]) + bg_ref[...])


def s5(u, B, L, lr, li, br, bi, cr, ci, d, wg, bg, tc=S5_CHUNK):
    T = u.shape[0]
    nc = L // tc
    NS = S5_NSTATE
    seg = tc // HALO
    rho = jnp.arange(tc)
    perm = (jnp.arange(tc)[None, :] == (seg * (rho % HALO) + rho // HALO)[:, None]).astype(BF16)
    big = pltpu.VMEM((tc, NS), F32)
    return pl.pallas_call(
        _s5_kernel,
        out_shape=jax.ShapeDtypeStruct((T, S5_WIDTH), F32),
        grid=(B, nc),
        in_specs=[pl.BlockSpec((tc, S5_WIDTH), lambda b, c: (b * nc + c, 0)),
                  _const_spec((tc, tc)), _const_spec((tc, tc)),
                  _const_spec((1, NS)), _const_spec((1, NS)),
                  _const_spec((S5_WIDTH, NS)), _const_spec((S5_WIDTH, NS)),
                  _const_spec((NS, S5_WIDTH)), _const_spec((NS, S5_WIDTH)),
                  _const_spec((1, S5_WIDTH)), _const_spec((S5_WIDTH, S5_WIDTH)), _const_spec((1, S5_WIDTH))],
        out_specs=pl.BlockSpec((tc, S5_WIDTH), lambda b, c: (b * nc + c, 0)),
        scratch_shapes=[pltpu.VMEM((1, NS), F32), pltpu.VMEM((1, NS), F32), big, big, big, big],
        compiler_params=_cparams("parallel", "arbitrary"),
        name="s5",
    )(u, perm, perm.T, lr, li, br, bi, cr, ci, d, wg, bg)


def _rwkv_prep_kernel(p_ref, ph_ref, mu_ref, wlr_ref, w0a0_ref, g2_ref, kk_ref, ka_ref, rk_ref, bd_ref,
                      r_ref, a_ref, v_ref, lw_ref, g_ref, bon_ref, kt_ref, bt_ref, lwt_ref, *, tiles_per_seq):
    tm = p_ref.shape[0]
    W = RW_WIDTH
    p = p_ref[...]
    pa = jnp.concatenate([ph_ref[...], p], axis=0)
    prev = pltpu.roll(pa, 1, 0)[HALO:]
    first_dead = jnp.where((pl.program_id(0) % tiles_per_seq) == 0, 1, 0)
    prev = jnp.where(_iota2((tm, 1), 0) < first_dead, 0.0, prev)
    pm = p + (prev - p) * mu_ref[...]
    r = pm[:, :W]
    k = pm[:, W:2 * W]
    v = pm[:, 2 * W:3 * W]
    xwa = pm[:, 3 * W:3 * W + LANE]
    xg = pm[:, 3 * W + LANE:]
    xwa = jnp.where(_iota2((tm, LANE), 1) < 64, jnp.tanh(xwa), xwa)
    wa = _dot_x3(xwa, wlr_ref[...]) + w0a0_ref[...]
    wlog = -_softplus(-wa[:, :W]) - 0.5
    lw = -jnp.exp(wlog)
    a = _sigmoid(wa[:, W:])
    g = _bdot(_sigmoid(xg), g2_ref[...])
    bd = bd_ref[...]
    kk = k * kk_ref[...]
    kk = kk / jnp.maximum(jnp.sqrt(_dot_x2(kk * kk, bd)), 1e-12)
    k2 = k * (1.0 + (a - 1.0) * ka_ref[...])
    r_ref[...] = r
    a_ref[...] = -kk
    v_ref[...] = v
    lw_ref[...] = lw
    g_ref[...] = g
    bon_ref[...] = _dot_x2(r * k2 * rk_ref[...], bd) * v
    kt_ref[...] = k2.T
    bt_ref[...] = (kk * a).T
    lwt_ref[...] = lw.T


def rwkv_prep(p, L, mu, wlr, w0a0, g2, kk, ka, rk, bd, tm=256):
    T = p.shape[0]
    W = RW_WIDTH
    hb = tm // HALO
    tok = jax.ShapeDtypeStruct((T, W), F32)
    chn = jax.ShapeDtypeStruct((W, T), F32)
    tok_spec = pl.BlockSpec((tm, W), lambda i: (i, 0))
    chn_spec = pl.BlockSpec((W, tm), lambda i: (0, i))
    kern = functools.partial(_rwkv_prep_kernel, tiles_per_seq=L // tm)
    return pl.pallas_call(
        kern,
        out_shape=[tok] * 6 + [chn] * 3,
        grid=(T // tm,),
        in_specs=[pl.BlockSpec((tm, RW_COLS), lambda i: (i, 0)),
                  pl.BlockSpec((HALO, RW_COLS), lambda i: (jnp.maximum(i * hb - 1, 0), 0)),
                  _const_spec((1, RW_COLS)), _const_spec((LANE, 2 * W)), _const_spec((1, 2 * W)),
                  _const_spec((LANE, W)), _const_spec((1, W)), _const_spec((1, W)), _const_spec((1, W)),
                  _const_spec((W, W))],
        out_specs=[tok_spec] * 6 + [chn_spec] * 3,
        compiler_params=_cparams("parallel"),
        name="rwkv_prep",
    )(p, p, mu, wlr, w0a0, g2, kk, ka, rk, bd)


def _wkv_local_kernel(r_ref, a_ref, v_ref, lw_ref, kt_ref, bt_ref, lwt_ref, wr_ref, y0_ref, m_ref, hl_ref):
    C = RW_CHUNK
    nch = r_ref.shape[0] // C
    ri = _iota2((C, C), 0)
    ci = _iota2((C, C), 1)
    incl = ri >= ci
    strict = ri > ci
    tril_b = jnp.where(incl, 1.0, 0.0).astype(BF16)
    triu_b = jnp.where(ri <= ci, 1.0, 0.0).astype(BF16)
    m0 = ci < RW_HEAD
    bdm = (ri // RW_HEAD) == (ci // RW_HEAD)
    swap = lambda t: pltpu.roll(t, RW_HEAD, 1)
    sls = [slice(j * C, (j + 1) * C) for j in range(nch)]
    chunks = range(nch)
    pairs = [(j, h) for j in chunks for h in range(2)]
    r = [r_ref[sl, :] for sl in sls]
    a = [a_ref[sl, :] for sl in sls]
    lw = [lw_ref[sl, :] for sl in sls]
    kt = [kt_ref[:, sl] for sl in sls]
    bt = [bt_ref[:, sl] for sl in sls]
    vb = [v_ref[sl, :].astype(BF16) for sl in sls]
    c = [_dot_2x(tril_b, lw[j]) for j in chunks]
    ct = [_dot_x2(lwt_ref[:, sls[j]], triu_b) for j in chunks]
    aa = []
    for j in chunks:
        c_mid = c[j][C // 2 - 1:C // 2]
        ct_mid = ct[j][:, C // 2 - 1:C // 2]
        at = a[j] * jnp.exp(c[j] - lw[j] - c_mid)
        rt = r[j] * jnp.exp(c[j] - c_mid)
        e_mid = jnp.exp(ct_mid - ct[j])
        lhs = jnp.concatenate([jnp.where(m0, at, 0.0), jnp.where(m0, 0.0, at),
                               jnp.where(m0, rt, 0.0), jnp.where(m0, 0.0, rt)], axis=0)
        rhs = jnp.concatenate([bt[j] * e_mid, kt[j] * e_mid], axis=1)
        aa.append(_bdot(lhs, rhs))
    n = {(j, h): jnp.where(strict, aa[j][h * C:(h + 1) * C, :C], 0.0) for j, h in pairs}
    ak = {(j, h): jnp.where(strict, aa[j][h * C:(h + 1) * C, C:], 0.0).astype(BF16) for j, h in pairs}
    rb = {(j, h): jnp.where(incl, aa[j][(2 + h) * C:(3 + h) * C, :C], 0.0).astype(BF16) for j, h in pairs}
    rk = {(j, h): jnp.where(incl, aa[j][(2 + h) * C:(3 + h) * C, C:], 0.0).astype(BF16) for j, h in pairs}
    x = {}
    for j in chunks:
        a_abs = a[j] * jnp.exp(c[j] - lw[j])
        akv = jnp.where(m0, _dot(ak[j, 0], vb[j]), _dot(ak[j, 1], vb[j]))
        x[j, 0] = jnp.where(m0, a_abs, swap(akv))
        x[j, 1] = jnp.where(m0, swap(a_abs), akv)
    sh = 1
    while sh < C:
        nb = {p: n[p].astype(BF16) for p in pairs}
        x = {p: x[p] + _dot(nb[p], x[p].astype(BF16)) for p in pairs}
        sh *= 2
        if sh < C:
            n = {p: _dot(nb[p], nb[p]) for p in pairs}
    g = {p: _dot(rb[p], x[p].astype(BF16)) for p in pairs}
    for j in chunks:
        wa = jnp.where(m0, x[j, 0], swap(x[j, 1]))
        u0 = jnp.where(m0, swap(x[j, 0]), x[j, 1])
        wr_ref[sls[j], :] = r[j] * jnp.exp(c[j]) + jnp.where(m0, g[j, 0], swap(g[j, 1]))
        y0_ref[sls[j], :] = (jnp.where(m0, swap(g[j, 0]), g[j, 1])
                             + jnp.where(m0, _dot(rk[j, 0], vb[j]), _dot(rk[j, 1], vb[j])))
        ct_last = ct[j][:, C - 1:C]
        e_last = jnp.exp(ct_last - ct[j])
        bh = (bt[j] * e_last).astype(BF16)
        kh = (kt[j] * e_last).astype(BF16)
        m = _dot(bh, wa.astype(BF16))
        hl = _dot(jnp.concatenate([bh, kh], axis=1), jnp.concatenate([u0.astype(BF16), vb[j]], axis=0))
        m_ref[j, 0] = jnp.where(bdm, m, 0.0) + jnp.where(ri == ci, jnp.exp(ct_last), 0.0)
        hl_ref[j, 0] = jnp.where(bdm, hl, 0.0)


def wkv_local(r, a, v, lw, kt, bt, lwt, tq=512):
    T = r.shape[0]
    C = RW_CHUNK
    npair = RW_WIDTH // LANE
    tok = pl.BlockSpec((tq, LANE), lambda i, h: (i, h))
    chn = pl.BlockSpec((LANE, tq), lambda i, h: (h, i))
    mat = pl.BlockSpec((tq // C, 1, LANE, LANE), lambda i, h: (i, h, 0, 0))
    mat_shape = jax.ShapeDtypeStruct((T // C, npair, LANE, LANE), F32)
    return pl.pallas_call(
        _wkv_local_kernel,
        out_shape=[jax.ShapeDtypeStruct((T, RW_WIDTH), F32)] * 2 + [mat_shape] * 2,
        grid=(T // tq, npair),
        in_specs=[tok] * 4 + [chn] * 3,
        out_specs=[tok, tok, mat, mat],
        compiler_params=_cparams("parallel", "parallel"),
        name="wkv_local",
    )(r, a, v, lw, kt, bt, lwt)


def _wkv_scan_kernel(wr_ref, y0_ref, m_ref, hl_ref, g_ref, bon_ref, lnw_ref, lnb_ref, o_ref, h_ref):
    @pl.when(pl.program_id(1) == 0)
    def _():
        h_ref[...] = jnp.zeros_like(h_ref)

    C = RW_CHUNK
    nch = wr_ref.shape[0] // C
    npair = h_ref.shape[0]
    bdm = (_iota2((C, C), 0) // RW_HEAD) == (_iota2((C, C), 1) // RW_HEAD)
    gn = jnp.where(bdm, 1.0 / RW_HEAD, 0.0).astype(BF16)
    h = [h_ref[p] for p in range(npair)]
    for j in range(nch):
        sl = slice(j * C, (j + 1) * C)
        ys = []
        for p in range(npair):
            ln = slice(p * LANE, (p + 1) * LANE)
            ys.append(_bdot(wr_ref[sl, ln], h[p]) + y0_ref[sl, ln])
        h = [_dot_x3(m_ref[j, p], h[p]) + hl_ref[j, p] for p in range(npair)]
        for p in range(npair):
            ln = slice(p * LANE, (p + 1) * LANE)
            mean = _dot_x2(ys[p], gn)
            yc = ys[p] - mean
            var = _dot_x2(yc * yc, gn)
            yn = yc * lax.rsqrt(var + RW_GN_EPS) * lnw_ref[:, ln] + lnb_ref[:, ln]
            o_ref[sl, ln] = (yn + bon_ref[sl, ln]) * g_ref[sl, ln]
    for p in range(npair):
        h_ref[p] = h[p]


def wkv_scan(wr, y0, m, hl, g, bon, lnw, lnb, B, L, tq=512):
    T, W = wr.shape
    C = RW_CHUNK
    nq = L // tq
    npair = W // LANE
    tok = pl.BlockSpec((tq, W), lambda b, c: (b * nq + c, 0))
    mat = pl.BlockSpec((tq // C, npair, LANE, LANE), lambda b, c: (b * nq + c, 0, 0, 0))
    return pl.pallas_call(
        _wkv_scan_kernel,
        out_shape=jax.ShapeDtypeStruct((T, W), F32),
        grid=(B, nq),
        in_specs=[tok, tok, mat, mat, tok, tok, _const_spec((1, W)), _const_spec((1, W))],
        out_specs=tok,
        scratch_shapes=[pltpu.VMEM((npair, LANE, LANE), F32)],
        compiler_params=_cparams("parallel", "arbitrary"),
        name="wkv_scan",
    )(wr, y0, m, hl, g, bon, lnw, lnb)


def wkv(r, a, v, lw, g, bon, kt, bt, lwt, lnw, lnb, B, L):
    wr, y0, m, hl = wkv_local(r, a, v, lw, kt, bt, lwt)
    return wkv_scan(wr, y0, m, hl, g, bon, lnw, lnb, B, L)


def _mamba_kernel(z_ref, xbc_ref, xh_ref, dt_ref, cw_ref, cb_ref, dtb_ref, an_ref, ex_ref, dsk_ref, nw_ref,
                  o_ref, st_ref):
    @pl.when(pl.program_id(1) == 0)
    def _():
        st_ref[...] = jnp.zeros_like(st_ref)

    C = MB_CHUNK
    W = MB_WIDTH
    N = MB_STATE
    xbc = xbc_ref[...]
    xa = jnp.concatenate([xh_ref[...], xbc], axis=0)
    halo_dead = jnp.where(pl.program_id(1) == 0, HALO, 0)
    xa = jnp.where(_iota2((HALO + C, 1), 0) < halo_dead, 0.0, xa)
    cw = cw_ref[...]
    conv = cw[3:4] * xbc + cb_ref[...]
    for kk in range(1, MB_CONV):
        conv = conv + cw[3 - kk:4 - kk] * pltpu.roll(xa, kk, 0)[HALO:]
    xbc = _silu(conv)
    xs = xbc[:, :W]
    dt = _softplus(dt_ref[...] + dtb_ref[...])
    a = dt * an_ref[...]
    ri = _iota2((C, C), 0)
    ci = _iota2((C, C), 1)
    causal = ri >= ci
    acs = _dot_2x(jnp.where(causal, 1.0, 0.0).astype(BF16), a)
    acs_t = acs.T
    ex = ex_ref[...]
    acs_x = _dot_x2(acs, ex)
    dt_x = _dot_x2(dt, ex)
    xdt = xs * dt_x
    xdt_b = xdt.astype(BF16)
    st = st_ref[...]
    hpg = MB_HEADS // MB_GROUPS
    gw = W // MB_GROUPS
    head_g = _iota2((C, gw), 1) // MB_HEADDIM
    ys = []
    for gi in range(MB_GROUPS):
        bm = xbc[:, W + gi * N:W + (gi + 1) * N]
        cm = xbc[:, W + MB_GROUPS * N + gi * N:W + MB_GROUPS * N + (gi + 1) * N]
        cb = _bdot_nt(cm, bm)
        lm = []
        for e in range(hpg):
            hd = gi * hpg + e
            lmat = jnp.exp(jnp.where(causal, acs[:, hd:hd + 1] - acs_t[hd:hd + 1, :], -jnp.inf))
            lm.append((cb * lmat).astype(BF16))
        res = _dot(jnp.concatenate(lm, axis=0), xdt_b[:, gi * gw:(gi + 1) * gw])
        yg = res[:C]
        for e in range(1, hpg):
            yg = jnp.where(head_g == e, res[e * C:(e + 1) * C], yg)
        off = _bdot_nt(cm, st[gi * gw:(gi + 1) * gw])
        ys.append(yg + off * jnp.exp(acs_x[:, gi * gw:(gi + 1) * gw]))
    y = jnp.concatenate(ys, axis=1)
    acs_last = acs_x[C - 1:C]
    xw_t = (xdt * jnp.exp(acs_last - acs_x)).T
    acs_xt = acs_x.T
    chunk_decay = jnp.exp(acs_xt[:, C - 1:C])
    new = [_bdot(xw_t[gi * gw:(gi + 1) * gw], xbc[:, W + gi * N:W + (gi + 1) * N]) for gi in range(MB_GROUPS)]
    st_ref[...] = chunk_decay * st + jnp.concatenate(new, axis=0)
    y = (y + xs * dsk_ref[...]) * _silu(z_ref[...])
    parts = []
    for gi in range(MB_GROUPS):
        yg = y[:, gi * gw:(gi + 1) * gw]
        parts.append(yg * lax.rsqrt(jnp.mean(yg * yg, axis=-1, keepdims=True) + EPS))
    o_ref[...] = jnp.concatenate(parts, axis=1) * nw_ref[...]


def mamba(z, xbc, dt, B, L, cw, cb, dtb, an, ex, dsk, nw):
    T = z.shape[0]
    C = MB_CHUNK
    nc = L // C
    hb = C // HALO
    return pl.pallas_call(
        _mamba_kernel,
        out_shape=jax.ShapeDtypeStruct((T, MB_WIDTH), F32),
        grid=(B, nc),
        in_specs=[pl.BlockSpec((C, MB_WIDTH), lambda b, c: (b * nc + c, 0)),
                  pl.BlockSpec((C, MB_XBC), lambda b, c: (b * nc + c, 0)),
                  pl.BlockSpec((HALO, MB_XBC), lambda b, c: (jnp.maximum((b * nc + c) * hb - 1, 0), 0)),
                  pl.BlockSpec((C, LANE), lambda b, c: (b * nc + c, 0)),
                  _const_spec((MB_CONV, MB_XBC)), _const_spec((1, MB_XBC)), _const_spec((1, LANE)),
                  _const_spec((1, LANE)), _const_spec((LANE, MB_WIDTH)), _const_spec((1, MB_WIDTH)),
                  _const_spec((1, MB_WIDTH))],
        out_specs=pl.BlockSpec((C, MB_WIDTH), lambda b, c: (b * nc + c, 0)),
        scratch_shapes=[pltpu.VMEM((MB_WIDTH, MB_STATE), F32)],
        compiler_params=_cparams("parallel", "arbitrary"),
        name="mamba2",
    )(z, xbc, xbc, dt, cw, cb, dtb, an, ex, dsk, nw)


def _pack_even(w_in, wa2, ba, gnorm, w_out):
    nk = GLA_HEADS * GLA_DK
    nv = GLA_HEADS * GLA_DV
    pad_v = lambda w: jnp.pad(w.reshape(D_MODEL, GLA_HEADS, GLA_DV),
                              ((0, 0), (0, 0), (0, GLA_DVP - GLA_DV))).reshape(D_MODEL, GLA_HEADS * GLA_DVP)
    wq = w_in[:, :2 * nk]
    wv = pad_v(w_in[:, 2 * nk:2 * nk + nv])
    wg = pad_v(w_in[:, 2 * nk + nv:2 * nk + 2 * nv])
    walo = jnp.pad(w_in[:, 2 * nk + 2 * nv:2 * nk + 2 * nv + GLA_RANK], ((0, 0), (0, LANE - GLA_RANK)))
    wu = w_in[:, 2 * nk + 2 * nv + GLA_RANK:]
    w_cat = jnp.concatenate([wq, wv, wg, wu, walo], axis=1).astype(BF16)
    wa = jnp.pad(wa2.reshape(GLA_RANK, GLA_HEADS, GLA_DK).transpose(1, 0, 2),
                 ((0, 0), (0, LANE - GLA_RANK), (0, 0)))
    bah = ba.reshape(GLA_HEADS, 1, GLA_DK)
    nw = jnp.pad(gnorm, (0, GLA_DVP - GLA_DV)).reshape(1, GLA_DVP)
    woa = jnp.pad(w_out[:nv].reshape(GLA_HEADS, GLA_DV, D_MODEL),
                  ((0, 0), (0, GLA_DVP - GLA_DV), (0, 0))).reshape(GLA_HEADS * GLA_DVP, D_MODEL).astype(BF16)
    wob = w_out[nv:].astype(BF16)
    return w_cat, wa, bah, nw, woa, wob


def _pack_s5(lam_re, lam_im, log_dt, b_re, b_im, c_re, c_im):
    dt = jnp.exp(log_dt)[:, None]
    mag = jnp.exp(lam_re * dt)
    ang = lam_im * dt
    lb_re = mag * jnp.cos(ang)
    lb_im = mag * jnp.sin(ang)
    den = lam_re * lam_re + lam_im * lam_im
    nr = lb_re - 1.0
    f_re = (nr * lam_re + lb_im * lam_im) / den
    f_im = (lb_im * lam_re - nr * lam_im) / den
    bb_re = f_re[..., None] * b_re - f_im[..., None] * b_im
    bb_im = f_re[..., None] * b_im + f_im[..., None] * b_re
    eye = jnp.eye(S5_GROUPS, dtype=F32)
    bd_in = lambda m: jnp.einsum('gpc,gh->gchp', m, eye).reshape(S5_WIDTH, S5_NSTATE).astype(BF16)
    bd_out = lambda m: jnp.einsum('gcp,gh->gphc', m, eye).reshape(S5_NSTATE, S5_WIDTH).astype(BF16)
    return (lb_re.reshape(1, S5_NSTATE), lb_im.reshape(1, S5_NSTATE),
            bd_in(bb_re), bd_in(bb_im), bd_out(c_re), bd_out(c_im))


def _pack_odd(w_in, w0, w2, a0, a2):
    w_cat = jnp.pad(w_in, ((0, 0), (0, PROJ_COLS - w_in.shape[1]))).astype(BF16)
    W = RW_WIDTH
    wlr = jnp.zeros((LANE, 2 * W), F32).at[:64, :W].set(w2).at[64:, W:].set(a2)
    w0a0 = jnp.concatenate([w0, a0]).reshape(1, 2 * W)
    return w_cat, wlr, w0a0


def kernel(x, norm_mix, norm_ffn, norm_final, e_w_in, e_gla_wa2, e_gla_ba, e_gla_norm, e_s5_lambda_re, e_s5_lambda_im, e_s5_log_dt, e_s5_b_re, e_s5_b_im, e_s5_c_re, e_s5_c_im, e_s5_d, e_s5_w_glu, e_s5_b_glu, e_w_out, o_w_in, o_rw_mu, o_rw_w0, o_rw_w2, o_rw_a0, o_rw_a2, o_rw_g2, o_rw_k_k, o_rw_k_a, o_rw_r_k, o_rw_ln_w, o_rw_ln_b, o_mb_conv_w, o_mb_conv_b, o_mb_dt_bias, o_mb_a_log, o_mb_d, o_mb_norm, o_w_out, ffn_w_up, ffn_conv_w, ffn_conv_b, ffn_w_down):
    B, L, D = x.shape
    T = B * L
    depth = norm_mix.shape[0]
    x2 = x.reshape(T, D)
    row = lambda t: t.reshape(1, -1)
    for i in range(depth):
        j = i // 2
        if i % 2 == 0:
            w_cat, wa, bah, gnw, woa, wob = _pack_even(e_w_in[j], e_gla_wa2[j], e_gla_ba[j], e_gla_norm[j],
                                                       e_w_out[j])
            nkq = GLA_HEADS * GLA_DK
            nvp = GLA_HEADS * GLA_DVP
            q, k, v, g, u, alo = norm_proj(x2, row(norm_mix[i]), w_cat, (nkq, nkq, nvp, nvp, S5_WIDTH, LANE))
            ya = gla(q, k, v, g, alo, B, L, wa, bah, gnw)
            lr, li, br, bi, cr, ci = _pack_s5(e_s5_lambda_re[j], e_s5_lambda_im[j], e_s5_log_dt[j],
                                              e_s5_b_re[j], e_s5_b_im[j], e_s5_c_re[j], e_s5_c_im[j])
            yb = s5(u, B, L, lr, li, br, bi, cr, ci, row(e_s5_d[j]), e_s5_w_glu[j].astype(BF16),
                    row(e_s5_b_glu[j]))
            mix = (ya, yb, woa, wob)
        else:
            W = RW_WIDTH
            w_cat, wlr, w0a0 = _pack_odd(o_w_in[j], o_rw_w0[j], o_rw_w2[j], o_rw_a0[j], o_rw_a2[j])
            p_rw, p_z, p_xbc, p_dt = norm_proj(x2, row(norm_mix[i]), w_cat, (RW_COLS, MB_WIDTH, MB_XBC, LANE))
            head_of = jnp.arange(W) // RW_HEAD
            bd = (head_of[:, None] == head_of[None, :]).astype(BF16)
            r, a, v, lw, g, bon, kt, bt, lwt = rwkv_prep(
                p_rw, L, row(o_rw_mu[j]), wlr, w0a0, o_rw_g2[j], row(o_rw_k_k[j]), row(o_rw_k_a[j]),
                row(o_rw_r_k[j]), bd)
            yc = wkv(r, a, v, lw, g, bon, kt, bt, lwt, row(o_rw_ln_w[j]), row(o_rw_ln_b[j]), B, L)
            an = jnp.pad(-jnp.exp(o_mb_a_log[j]), (0, LANE - MB_HEADS)).reshape(1, LANE)
            dtb = jnp.pad(o_mb_dt_bias[j], (0, LANE - MB_HEADS)).reshape(1, LANE)
            ex = (jnp.arange(LANE)[:, None] == (jnp.arange(MB_WIDTH) // MB_HEADDIM)[None, :]).astype(BF16)
            dsk = jnp.repeat(o_mb_d[j], MB_HEADDIM).reshape(1, MB_WIDTH)
            yd = mamba(p_z, p_xbc, p_dt, B, L, o_mb_conv_w[j], row(o_mb_conv_b[j]), dtb, an, ex, dsk, row(o_mb_norm[j]))
            mix = (yc, yd, o_w_out[j][:W].astype(BF16), o_w_out[j][W:].astype(BF16))
        wup = ffn_w_up[i]
        x2 = ffn(x2, mix[0], mix[1], L, mix[2], mix[3], row(norm_ffn[i]), wup[:, :D_FF].astype(BF16),
                 wup[:, D_FF:].astype(BF16),
                 ffn_conv_w[i], row(ffn_conv_b[i]), ffn_w_down[i].astype(BF16), row(norm_final),
                 final_norm=(i == depth - 1))
    return x2.reshape(B, L, D)
```

```python
import functools
import math

import jax
import jax.numpy as jnp
from jax import lax
from jax.experimental import pallas as pl
from jax.experimental.pallas import tpu as pltpu

F32 = jnp.float32
BF16 = jnp.bfloat16
HI = lax.Precision.HIGHEST

D_MODEL = 1024
D_FF = 2816
EPS = 1e-6
LANE = 128
HALO = 8

GLA_HEADS = 4
GLA_DK = 128
GLA_DV = 192
GLA_DVP = 256
GLA_RANK = 16
GLA_TAU = 16.0
GLA_CHUNK = 128
S5_WIDTH = 256
S5_GROUPS = 16
S5_GROUP = 16
S5_STATE = 64
S5_NSTATE = S5_GROUPS * S5_STATE
S5_CHUNK = 256
RW_WIDTH = 512
RW_HEAD = 64
RW_COLS = 1792
RW_GN_EPS = 64e-5
RW_CHUNK = 128
MB_WIDTH = 512
MB_HEADS = 8
MB_HEADDIM = 64
MB_GROUPS = 2
MB_STATE = 128
MB_CONV = 4
MB_CHUNK = 128
MB_XBC = 1024
PROJ_COLS = 3456

VMEM_LIMIT = 56 * 1024 * 1024


def _cparams(*sem):
    return pltpu.CompilerParams(dimension_semantics=sem, vmem_limit_bytes=VMEM_LIMIT)


def _dot(a, b, precision=None):
    return jnp.dot(a, b, preferred_element_type=F32, precision=precision)


def _dot_nt(a, b, precision=None):
    return lax.dot_general(a, b, (((1,), (1,)), ((), ())), preferred_element_type=F32, precision=precision)


def _bdot(a, b):
    return _dot(a.astype(BF16), b.astype(BF16))


def _bdot_nt(a, b):
    return _dot_nt(a.astype(BF16), b.astype(BF16))


def _split(x):
    hi = x.astype(BF16)
    return hi, (x - hi.astype(F32)).astype(BF16)


def _dot_x2(x, e):
    hi, lo = _split(x)
    return _dot(hi, e) + _dot(lo, e)


def _dot_2x(e, x):
    hi, lo = _split(x)
    return _dot(e, hi) + _dot(e, lo)


def _dot_x3(a, b):
    ah, al = _split(a)
    bh, bl = _split(b)
    return _dot(ah, bh) + (_dot(al, bh) + _dot(ah, bl))


def _rms(x, w):
    return x * lax.rsqrt(jnp.mean(x * x, axis=-1, keepdims=True) + EPS) * w


def _sigmoid(x):
    return 1.0 / (1.0 + jnp.exp(-x))


def _silu(x):
    return x * _sigmoid(x)


def _softplus(x):
    return jnp.maximum(x, 0.0) + jnp.log(1.0 + jnp.exp(-jnp.abs(x)))


def _iota2(shape, dim):
    return lax.broadcasted_iota(jnp.int32, shape, dim)


def _const_spec(shape):
    nd = len(shape)
    return pl.BlockSpec(shape, lambda *_: (0,) * nd)


def _norm_proj_kernel(x_ref, nw_ref, w_ref, *o_refs):
    h = _rms(x_ref[...], nw_ref[...])
    y = _dot(h.astype(BF16), w_ref[...])
    off = 0
    for o_ref in o_refs:
        n = o_ref.shape[1]
        o_ref[...] = y[:, off:off + n]
        off += n


def norm_proj(x2, nw, w, splits, tm=512):
    T, D = x2.shape
    N = w.shape[1]
    assert sum(splits) == N
    return pl.pallas_call(
        _norm_proj_kernel,
        out_shape=[jax.ShapeDtypeStruct((T, n), F32) for n in splits],
        grid=(T // tm,),
        in_specs=[pl.BlockSpec((tm, D), lambda i: (i, 0)), _const_spec((1, D)), _const_spec((D, N))],
        out_specs=[pl.BlockSpec((tm, n), lambda i: (i, 0)) for n in splits],
        compiler_params=_cparams("parallel"),
        name="norm_proj",
    )(x2, nw, w)


def _ffn_kernel(x_ref, xh_ref, ya_ref, yah_ref, yb_ref, ybh_ref, wa_ref, wb_ref, nw_ref, wg_ref, wu_ref,
                cw_ref, cb_ref, wd_ref, fnw_ref, o_ref, *, tiles_per_seq, final_norm):
    tm = x_ref.shape[0]
    ya = jnp.concatenate([yah_ref[...], ya_ref[...]], axis=0).astype(BF16)
    yb = jnp.concatenate([ybh_ref[...], yb_ref[...]], axis=0).astype(BF16)
    xa = (jnp.concatenate([xh_ref[...], x_ref[...]], axis=0)
          + _dot(ya, wa_ref[...]) + _dot(yb, wb_ref[...]))
    x = xa[HALO:]
    h = _rms(xa, nw_ref[...]).astype(BF16)
    gate = _dot(h, wg_ref[...])
    halo_dead = jnp.where((pl.program_id(0) % tiles_per_seq) == 0, HALO, 0)
    gate = jnp.where(_iota2((HALO + tm, 1), 0) < halo_dead, 0.0, gate)
    cw = cw_ref[...]
    conv = (cw[2:3] * gate[HALO:]
            + cw[1:2] * pltpu.roll(gate, 1, 0)[HALO:]
            + cw[0:1] * pltpu.roll(gate, 2, 0)[HALO:]) + cb_ref[...]
    up = _dot(h[HALO:], wu_ref[...])
    act = (_silu(conv) * up).astype(BF16)
    y = x + _dot(act, wd_ref[...])
    if final_norm:
        y = _rms(y, fnw_ref[...])
    o_ref[...] = y


def ffn(x2, ya, yb, seq_len, wa, wb, nw, wg, wu, cw, cb, wd, fnw, final_norm, tm=256):
    T, D = x2.shape
    FF = wg.shape[1]
    ka, kb = ya.shape[1], yb.shape[1]
    hb = tm // HALO
    tile = lambda n: pl.BlockSpec((tm, n), lambda i: (i, 0))
    halo = lambda n: pl.BlockSpec((HALO, n), lambda i: (jnp.maximum(i * hb - 1, 0), 0))
    kern = functools.partial(_ffn_kernel, tiles_per_seq=seq_len // tm, final_norm=final_norm)
    return pl.pallas_call(
        kern,
        out_shape=jax.ShapeDtypeStruct((T, D), F32),
        grid=(T // tm,),
        in_specs=[tile(D), halo(D), tile(ka), halo(ka), tile(kb), halo(kb),
                  _const_spec((ka, D)), _const_spec((kb, D)),
                  _const_spec((1, D)), _const_spec((D, FF)), _const_spec((D, FF)),
                  _const_spec((3, FF)), _const_spec((1, FF)), _const_spec((FF, D)), _const_spec((1, D))],
        out_specs=tile(D),
        compiler_params=_cparams("parallel"),
        name="ffn",
    )(x2, x2, ya, ya, yb, yb, wa, wb, nw, wg, wu, cw, cb, wd, fnw)


def _gla_kernel(q_ref, k_ref, v_ref, g_ref, alo_ref, wa_ref, ba_ref, nw_ref, o_ref, st_ref):
    @pl.when(pl.program_id(2) == 0)
    def _():
        st_ref[...] = jnp.zeros_like(st_ref)

    C = GLA_CHUNK
    tq = q_ref.shape[0]
    nh = st_ref.shape[0]
    ri = _iota2((C, C), 0)
    ci = _iota2((C, C), 1)
    tril = (ri >= ci)
    tril_b = jnp.where(tril, 1.0, 0.0).astype(BF16)
    scale = GLA_DK ** -0.5
    alo = alo_ref[...].astype(BF16)
    sls = [slice(j * C, (j + 1) * C) for j in range(tq // C)]
    o_in, kdv, qd, dec = {}, {}, {}, {}
    for h in range(nh):
        kl = slice(h * GLA_DK, (h + 1) * GLA_DK)
        vl = slice(h * GLA_DVP, (h + 1) * GLA_DVP)
        xg = _dot(alo, wa_ref[h].astype(BF16)) + ba_ref[h]
        la = (jnp.minimum(xg, 0.0) - jnp.log(1.0 + jnp.exp(-jnp.abs(xg)))) * (1.0 / GLA_TAU)
        for j, sl in enumerate(sls):
            b = _dot_2x(tril_b, la[sl])
            b_mid = b[C // 2 - 1:C // 2]
            b_last = b[C - 1:C]
            q = q_ref[sl, kl] * scale
            k = k_ref[sl, kl]
            v = v_ref[sl, vl]
            qe = q * jnp.exp(b - b_mid)
            ke = k * jnp.exp(jnp.minimum(b_mid - b, 80.0))
            att = jnp.where(tril, _bdot_nt(qe, ke), 0.0)
            o_in[h, j] = _bdot(att, v)
            kdv[h, j] = _bdot(v.T, k * jnp.exp(b_last - b))
            qd[h, j] = (q * jnp.exp(b)).astype(BF16)
            dec[h, j] = jnp.exp(b_last)
    sts = {}
    for h in range(nh):
        st = st_ref[h]
        for j in range(len(sls)):
            sts[h, j] = st.astype(BF16)
            st = st * dec[h, j] + kdv[h, j]
        st_ref[h] = st
    for h in range(nh):
        vl = slice(h * GLA_DVP, (h + 1) * GLA_DVP)
        for j, sl in enumerate(sls):
            o = o_in[h, j] + _dot_nt(qd[h, j], sts[h, j])
            ms = jnp.sum(o * o, axis=-1, keepdims=True) * (1.0 / GLA_DV)
            o = o * lax.rsqrt(ms + EPS) * nw_ref[...]
            o_ref[sl, vl] = o * _silu(g_ref[sl, vl])


def gla(q, k, v, g, alo, B, L, wa, ba, nw, tq=512, nh=4):
    T = q.shape[0]
    nq = L // tq
    heads = lambda w: pl.BlockSpec((tq, nh * w), lambda b, h, c: (b * nq + c, h))
    return pl.pallas_call(
        _gla_kernel,
        out_shape=jax.ShapeDtypeStruct((T, GLA_HEADS * GLA_DVP), F32),
        grid=(B, GLA_HEADS // nh, nq),
        in_specs=[heads(GLA_DK), heads(GLA_DK), heads(GLA_DVP), heads(GLA_DVP),
                  pl.BlockSpec((tq, LANE), lambda b, h, c: (b * nq + c, 0)),
                  pl.BlockSpec((nh, LANE, GLA_DK), lambda b, h, c: (h, 0, 0)),
                  pl.BlockSpec((nh, 1, GLA_DK), lambda b, h, c: (h, 0, 0)),
                  _const_spec((1, GLA_DVP))],
        out_specs=heads(GLA_DVP),
        scratch_shapes=[pltpu.VMEM((nh, GLA_DVP, GLA_DK), F32)],
        compiler_params=_cparams("parallel", "parallel", "arbitrary"),
        name="gla",
    )(q, k, v, g, alo, wa, ba, nw)


def _cmul(ar, ai, br, bi):
    return ar * br - ai * bi, ar * bi + ai * br


def _s5_kernel(u_ref, perm_ref, unperm_ref, lr_ref, li_ref, br_ref, bi_ref, cr_ref, ci_ref, d_ref, wg_ref,
               bg_ref, o_ref, sr_ref, si_ref, pr_ref, pi_ref, xr_ref, xi_ref):
    tc = u_ref.shape[0]
    seg = tc // HALO
    ns = lr_ref.shape[1]
    lr = lr_ref[...]
    li = li_ref[...]
    lr8 = jnp.broadcast_to(lr, (HALO, ns))
    li8 = jnp.broadcast_to(li, (HALO, ns))
    rows = lambda j: slice(j * HALO, (j + 1) * HALO)

    @pl.when(pl.program_id(1) == 0)
    def _():
        sr_ref[...] = jnp.zeros_like(sr_ref)
        si_ref[...] = jnp.zeros_like(si_ref)
        zr, zi = lr8, li8
        for j in range(seg):
            pr_ref[rows(j), :] = zr
            pi_ref[rows(j), :] = zi
            zr, zi = _cmul(lr8, li8, zr, zi)

    u = u_ref[...]
    ub = _dot(perm_ref[...], u.astype(BF16)).astype(BF16)
    xr_ref[...] = _dot(ub, br_ref[...])
    xi_ref[...] = _dot(ub, bi_ref[...])
    er = xr_ref[rows(0), :]
    ei = xi_ref[rows(0), :]
    for j in range(1, seg):
        tr, ti = _cmul(lr8, li8, er, ei)
        er = xr_ref[rows(j), :] + tr
        ei = xi_ref[rows(j), :] + ti
        xr_ref[rows(j), :] = er
        xi_ref[rows(j), :] = ei
    mr, mi = lr, li
    for _ in range(seg.bit_length() - 1):
        mr, mi = _cmul(mr, mi, mr, mi)
    row8 = _iota2((HALO, 1), 0)
    cr0, ci0 = _cmul(mr, mi, sr_ref[...], si_ref[...])
    er = er + jnp.where(row8 == 0, cr0, 0.0)
    ei = ei + jnp.where(row8 == 0, ci0, 0.0)
    sh = 1
    while sh < HALO:
        yr, yi = _cmul(mr, mi, jnp.where(row8 < sh, 0.0, pltpu.roll(er, sh, 0)),
                       jnp.where(row8 < sh, 0.0, pltpu.roll(ei, sh, 0)))
        er, ei = er + yr, ei + yi
        mr, mi = _cmul(mr, mi, mr, mi)
        sh *= 2
    inr = jnp.where(row8 == 0, sr_ref[...], pltpu.roll(er, 1, 0))
    ini = jnp.where(row8 == 0, si_ref[...], pltpu.roll(ei, 1, 0))
    sr_ref[...] = er[HALO - 1:]
    si_ref[...] = ei[HALO - 1:]
    ar, ai = _cmul(pr_ref[...].reshape(seg, HALO, ns), pi_ref[...].reshape(seg, HALO, ns), inr[None], ini[None])
    xr = xr_ref[...] + ar.reshape(tc, ns)
    xi = xi_ref[...] + ai.reshape(tc, ns)
    yp = _bdot(xr, cr_ref[...]) - _bdot(xi, ci_ref[...])
    y = _dot_2x(unperm_ref[...], yp) + d_ref[...] * u
    z = 0.5 * y * (1.0 + jnp.tanh(math.sqrt(2.0 / math.pi) * (y + 0.044715 * (y * y * y))))
    o_ref[...] = z * _sigmoid(_bdot(z, wg_ref[...]) + bg_ref[...])


def s5(u, B, L, lr, li, br, bi, cr, ci, d, wg, bg, tc=S5_CHUNK):
    T = u.shape[0]
    nc = L // tc
    NS = S5_NSTATE
    seg = tc // HALO
    rho = jnp.arange(tc)
    perm = (jnp.arange(tc)[None, :] == (seg * (rho % HALO) + rho // HALO)[:, None]).astype(BF16)
    big = pltpu.VMEM((tc, NS), F32)
    return pl.pallas_call(
        _s5_kernel,
        out_shape=jax.ShapeDtypeStruct((T, S5_WIDTH), F32),
        grid=(B, nc),
        in_specs=[pl.BlockSpec((tc, S5_WIDTH), lambda b, c: (b * nc + c, 0)),
                  _const_spec((tc, tc)), _const_spec((tc, tc)),
                  _const_spec((1, NS)), _const_spec((1, NS)),
                  _const_spec((S5_WIDTH, NS)), _const_spec((S5_WIDTH, NS)),
                  _const_spec((NS, S5_WIDTH)), _const_spec((NS, S5_WIDTH)),
                  _const_spec((1, S5_WIDTH)), _const_spec((S5_WIDTH, S5_WIDTH)), _const_spec((1, S5_WIDTH))],
        out_specs=pl.BlockSpec((tc, S5_WIDTH), lambda b, c: (b * nc + c, 0)),
        scratch_shapes=[pltpu.VMEM((1, NS), F32), pltpu.VMEM((1, NS), F32), big, big, big, big],
        compiler_params=_cparams("parallel", "arbitrary"),
        name="s5",
    )(u, perm, perm.T, lr, li, br, bi, cr, ci, d, wg, bg)


def _rwkv_prep_kernel(p_ref, ph_ref, mu_ref, wlr_ref, w0a0_ref, g2_ref, kk_ref, ka_ref, rk_ref, bd_ref,
                      r_ref, a_ref, v_ref, lw_ref, g_ref, bon_ref, kt_ref, bt_ref, lwt_ref, *, tiles_per_seq):
    tm = p_ref.shape[0]
    W = RW_WIDTH
    p = p_ref[...]
    pa = jnp.concatenate([ph_ref[...], p], axis=0)
    prev = pltpu.roll(pa, 1, 0)[HALO:]
    first_dead = jnp.where((pl.program_id(0) % tiles_per_seq) == 0, 1, 0)
    prev = jnp.where(_iota2((tm, 1), 0) < first_dead, 0.0, prev)
    pm = p + (prev - p) * mu_ref[...]
    r = pm[:, :W]
    k = pm[:, W:2 * W]
    v = pm[:, 2 * W:3 * W]
    xwa = pm[:, 3 * W:3 * W + LANE]
    xg = pm[:, 3 * W + LANE:]
    xwa = jnp.where(_iota2((tm, LANE), 1) < 64, jnp.tanh(xwa), xwa)
    wa = _dot_x3(xwa, wlr_ref[...]) + w0a0_ref[...]
    wlog = -_softplus(-wa[:, :W]) - 0.5
    lw = -jnp.exp(wlog)
    a = _sigmoid(wa[:, W:])
    g = _bdot(_sigmoid(xg), g2_ref[...])
    bd = bd_ref[...]
    kk = k * kk_ref[...]
    kk = kk / jnp.maximum(jnp.sqrt(_dot_x2(kk * kk, bd)), 1e-12)
    k2 = k * (1.0 + (a - 1.0) * ka_ref[...])
    r_ref[...] = r
    a_ref[...] = -kk
    v_ref[...] = v
    lw_ref[...] = lw
    g_ref[...] = g
    bon_ref[...] = _dot_x2(r * k2 * rk_ref[...], bd) * v
    kt_ref[...] = k2.T
    bt_ref[...] = (kk * a).T
    lwt_ref[...] = lw.T


def rwkv_prep(p, L, mu, wlr, w0a0, g2, kk, ka, rk, bd, tm=512):
    T = p.shape[0]
    W = RW_WIDTH
    hb = tm // HALO
    tok = jax.ShapeDtypeStruct((T, W), F32)
    chn = jax.ShapeDtypeStruct((W, T), F32)
    tok_spec = pl.BlockSpec((tm, W), lambda i: (i, 0))
    chn_spec = pl.BlockSpec((W, tm), lambda i: (0, i))
    kern = functools.partial(_rwkv_prep_kernel, tiles_per_seq=L // tm)
    return pl.pallas_call(
        kern,
        out_shape=[tok] * 6 + [chn] * 3,
        grid=(T // tm,),
        in_specs=[pl.BlockSpec((tm, RW_COLS), lambda i: (i, 0)),
                  pl.BlockSpec((HALO, RW_COLS), lambda i: (jnp.maximum(i * hb - 1, 0), 0)),
                  _const_spec((1, RW_COLS)), _const_spec((LANE, 2 * W)), _const_spec((1, 2 * W)),
                  _const_spec((LANE, W)), _const_spec((1, W)), _const_spec((1, W)), _const_spec((1, W)),
                  _const_spec((W, W))],
        out_specs=[tok_spec] * 6 + [chn_spec] * 3,
        compiler_params=_cparams("parallel"),
        name="rwkv_prep",
    )(p, p, mu, wlr, w0a0, g2, kk, ka, rk, bd)


def _wkv_local_kernel(r_ref, a_ref, v_ref, lw_ref, kt_ref, bt_ref, lwt_ref, wr_ref, y0_ref, m_ref, hl_ref):
    C = RW_CHUNK
    nch = r_ref.shape[0] // C
    ri = _iota2((C, C), 0)
    ci = _iota2((C, C), 1)
    incl = ri >= ci
    strict = ri > ci
    tril_b = jnp.where(incl, 1.0, 0.0).astype(BF16)
    triu_b = jnp.where(ri <= ci, 1.0, 0.0).astype(BF16)
    m0 = ci < RW_HEAD
    bdm = (ri // RW_HEAD) == (ci // RW_HEAD)
    swap = lambda t: pltpu.roll(t, RW_HEAD, 1)
    sls = [slice(j * C, (j + 1) * C) for j in range(nch)]
    chunks = range(nch)
    pairs = [(j, h) for j in chunks for h in range(2)]
    r = [r_ref[sl, :] for sl in sls]
    a = [a_ref[sl, :] for sl in sls]
    lw = [lw_ref[sl, :] for sl in sls]
    kt = [kt_ref[:, sl] for sl in sls]
    bt = [bt_ref[:, sl] for sl in sls]
    vb = [v_ref[sl, :].astype(BF16) for sl in sls]
    c = [_dot_2x(tril_b, lw[j]) for j in chunks]
    ct = [_dot_x2(lwt_ref[:, sls[j]], triu_b) for j in chunks]
    aa = []
    for j in chunks:
        c_mid = c[j][C // 2 - 1:C // 2]
        ct_mid = ct[j][:, C // 2 - 1:C // 2]
        at = a[j] * jnp.exp(c[j] - lw[j] - c_mid)
        rt = r[j] * jnp.exp(c[j] - c_mid)
        e_mid = jnp.exp(ct_mid - ct[j])
        lhs = jnp.concatenate([jnp.where(m0, at, 0.0), jnp.where(m0, 0.0, at),
                               jnp.where(m0, rt, 0.0), jnp.where(m0, 0.0, rt)], axis=0)
        rhs = jnp.concatenate([bt[j] * e_mid, kt[j] * e_mid], axis=1)
        aa.append(_bdot(lhs, rhs))
    n = {(j, h): jnp.where(strict, aa[j][h * C:(h + 1) * C, :C], 0.0) for j, h in pairs}
    ak = {(j, h): jnp.where(strict, aa[j][h * C:(h + 1) * C, C:], 0.0).astype(BF16) for j, h in pairs}
    rb = {(j, h): jnp.where(incl, aa[j][(2 + h) * C:(3 + h) * C, :C], 0.0).astype(BF16) for j, h in pairs}
    rk = {(j, h): jnp.where(incl, aa[j][(2 + h) * C:(3 + h) * C, C:], 0.0).astype(BF16) for j, h in pairs}
    x = {}
    for j in chunks:
        a_abs = a[j] * jnp.exp(c[j] - lw[j])
        akv = jnp.where(m0, _dot(ak[j, 0], vb[j]), _dot(ak[j, 1], vb[j]))
        x[j, 0] = jnp.where(m0, a_abs, swap(akv))
        x[j, 1] = jnp.where(m0, swap(a_abs), akv)
    sh = 1
    while sh < C:
        nb = {p: n[p].astype(BF16) for p in pairs}
        x = {p: x[p] + _dot(nb[p], x[p].astype(BF16)) for p in pairs}
        sh *= 2
        if sh < C:
            n = {p: _dot(nb[p], nb[p]) for p in pairs}
    g = {p: _dot(rb[p], x[p].astype(BF16)) for p in pairs}
    for j in chunks:
        wa = jnp.where(m0, x[j, 0], swap(x[j, 1]))
        u0 = jnp.where(m0, swap(x[j, 0]), x[j, 1])
        wr_ref[sls[j], :] = (r[j] * jnp.exp(c[j]) + jnp.where(m0, g[j, 0], swap(g[j, 1]))).astype(BF16)
        y0_ref[sls[j], :] = (jnp.where(m0, swap(g[j, 0]), g[j, 1])
                             + jnp.where(m0, _dot(rk[j, 0], vb[j]), _dot(rk[j, 1], vb[j])))
        ct_last = ct[j][:, C - 1:C]
        e_last = jnp.exp(ct_last - ct[j])
        bh = (bt[j] * e_last).astype(BF16)
        kh = (kt[j] * e_last).astype(BF16)
        m = _dot(bh, wa.astype(BF16))
        hl = _dot(jnp.concatenate([bh, kh], axis=1), jnp.concatenate([u0.astype(BF16), vb[j]], axis=0))
        m_ref[j, 0] = jnp.where(bdm, m, 0.0) + jnp.where(ri == ci, jnp.exp(ct_last), 0.0)
        hl_ref[j, 0] = jnp.where(bdm, hl, 0.0)


def wkv_local(r, a, v, lw, kt, bt, lwt, tq=512):
    T = r.shape[0]
    C = RW_CHUNK
    npair = RW_WIDTH // LANE
    tok = pl.BlockSpec((tq, LANE), lambda i, h: (i, h))
    chn = pl.BlockSpec((LANE, tq), lambda i, h: (h, i))
    mat = pl.BlockSpec((tq // C, 1, LANE, LANE), lambda i, h: (i, h, 0, 0))
    mat_shape = jax.ShapeDtypeStruct((T // C, npair, LANE, LANE), F32)
    return pl.pallas_call(
        _wkv_local_kernel,
        out_shape=[jax.ShapeDtypeStruct((T, RW_WIDTH), BF16), jax.ShapeDtypeStruct((T, RW_WIDTH), F32),
                   mat_shape, mat_shape],
        grid=(T // tq, npair),
        in_specs=[tok] * 4 + [chn] * 3,
        out_specs=[tok, tok, mat, mat],
        compiler_params=_cparams("parallel", "parallel"),
        name="wkv_local",
    )(r, a, v, lw, kt, bt, lwt)


def _wkv_scan_kernel(wr_ref, y0_ref, m_ref, hl_ref, g_ref, bon_ref, lnw_ref, lnb_ref, o_ref, h_ref):
    @pl.when(pl.program_id(1) == 0)
    def _():
        h_ref[...] = jnp.zeros_like(h_ref)

    C = RW_CHUNK
    nch = wr_ref.shape[0] // C
    npair = h_ref.shape[0]
    bdm = (_iota2((C, C), 0) // RW_HEAD) == (_iota2((C, C), 1) // RW_HEAD)
    gn = jnp.where(bdm, 1.0 / RW_HEAD, 0.0).astype(BF16)
    h = [h_ref[p] for p in range(npair)]
    for j in range(nch):
        sl = slice(j * C, (j + 1) * C)
        ys = []
        for p in range(npair):
            ln = slice(p * LANE, (p + 1) * LANE)
            ys.append(_dot(wr_ref[sl, ln], h[p].astype(BF16)) + y0_ref[sl, ln])
        h = [_dot_x3(m_ref[j, p], h[p]) + hl_ref[j, p] for p in range(npair)]
        for p in range(npair):
            ln = slice(p * LANE, (p + 1) * LANE)
            mean = _dot_x2(ys[p], gn)
            yc = ys[p] - mean
            var = _dot_x2(yc * yc, gn)
            yn = yc * lax.rsqrt(var + RW_GN_EPS) * lnw_ref[:, ln] + lnb_ref[:, ln]
            o_ref[sl, ln] = (yn + bon_ref[sl, ln]) * g_ref[sl, ln]
    for p in range(npair):
        h_ref[p] = h[p]


def wkv_scan(wr, y0, m, hl, g, bon, lnw, lnb, B, L, tq=512):
    T, W = wr.shape
    C = RW_CHUNK
    nq = L // tq
    npair = W // LANE
    tok = pl.BlockSpec((tq, W), lambda b, c: (b * nq + c, 0))
    mat = pl.BlockSpec((tq // C, npair, LANE, LANE), lambda b, c: (b * nq + c, 0, 0, 0))
    return pl.pallas_call(
        _wkv_scan_kernel,
        out_shape=jax.ShapeDtypeStruct((T, W), F32),
        grid=(B, nq),
        in_specs=[tok, tok, mat, mat, tok, tok, _const_spec((1, W)), _const_spec((1, W))],
        out_specs=tok,
        scratch_shapes=[pltpu.VMEM((npair, LANE, LANE), F32)],
        compiler_params=_cparams("parallel", "arbitrary"),
        name="wkv_scan",
    )(wr, y0, m, hl, g, bon, lnw, lnb)


def wkv(r, a, v, lw, g, bon, kt, bt, lwt, lnw, lnb, B, L):
    wr, y0, m, hl = wkv_local(r, a, v, lw, kt, bt, lwt)
    return wkv_scan(wr, y0, m, hl, g, bon, lnw, lnb, B, L)


def _mamba_kernel(z_ref, xbc_ref, xh_ref, dt_ref, cw_ref, cb_ref, dtb_ref, an_ref, ex_ref, dsk_ref, nw_ref,
                  o_ref, st_ref):
    @pl.when(pl.program_id(1) == 0)
    def _():
        st_ref[...] = jnp.zeros_like(st_ref)

    C = MB_CHUNK
    W = MB_WIDTH
    N = MB_STATE
    tq = z_ref.shape[0]
    hpg = MB_HEADS // MB_GROUPS
    gw = W // MB_GROUPS
    xbc = xbc_ref[...]
    xa = jnp.concatenate([xh_ref[...], xbc], axis=0)
    halo_dead = jnp.where(pl.program_id(1) == 0, HALO, 0)
    xa = jnp.where(_iota2((HALO + tq, 1), 0) < halo_dead, 0.0, xa)
    cw = cw_ref[...]
    conv = cw[3:4] * xbc + cb_ref[...]
    for kk in range(1, MB_CONV):
        conv = conv + cw[3 - kk:4 - kk] * pltpu.roll(xa, kk, 0)[HALO:]
    xbc = _silu(conv)
    xs = xbc[:, :W]
    dt = _softplus(dt_ref[...] + dtb_ref[...])
    a = dt * an_ref[...]
    ex = ex_ref[...]
    xdt = xs * _dot_x2(dt, ex)
    xdt_b = xdt.astype(BF16)
    ri = _iota2((C, C), 0)
    ci = _iota2((C, C), 1)
    causal = ri >= ci
    tril_b = jnp.where(causal, 1.0, 0.0).astype(BF16)
    head_g = _iota2((C, gw), 1) // MB_HEADDIM
    sls = [slice(j * C, (j + 1) * C) for j in range(tq // C)]
    y_loc, upd, cdec, acs_xs, cms = [], [], [], [], []
    for sl in sls:
        acs = _dot_2x(tril_b, a[sl])
        acs_t = acs.T
        acs_x = _dot_x2(acs, ex)
        ys, cm_g = [], []
        for gi in range(MB_GROUPS):
            bm = xbc[sl, W + gi * N:W + (gi + 1) * N]
            cm = xbc[sl, W + MB_GROUPS * N + gi * N:W + MB_GROUPS * N + (gi + 1) * N].astype(BF16)
            cb = _dot_nt(cm, bm.astype(BF16))
            lm = []
            for e in range(hpg):
                hd = gi * hpg + e
                lmat = jnp.exp(jnp.where(causal, acs[:, hd:hd + 1] - acs_t[hd:hd + 1, :], -jnp.inf))
                lm.append((cb * lmat).astype(BF16))
            res = _dot(jnp.concatenate(lm, axis=0), xdt_b[sl, gi * gw:(gi + 1) * gw])
            yg = res[:C]
            for e in range(1, hpg):
                yg = jnp.where(head_g == e, res[e * C:(e + 1) * C], yg)
            ys.append(yg)
            cm_g.append(cm)
        y_loc.append(ys)
        cms.append(cm_g)
        acs_xs.append(acs_x)
        acs_last = acs_x[C - 1:C]
        xw_t = (xdt[sl] * jnp.exp(acs_last - acs_x)).T
        upd.append(jnp.concatenate(
            [_bdot(xw_t[gi * gw:(gi + 1) * gw], xbc[sl, W + gi * N:W + (gi + 1) * N]) for gi in range(MB_GROUPS)],
            axis=0))
        cdec.append(jnp.exp(acs_x.T[:, C - 1:C]))
    st = st_ref[...]
    sts = []
    for j in range(len(sls)):
        sts.append(st.astype(BF16))
        st = cdec[j] * st + upd[j]
    st_ref[...] = st
    zs = _silu(z_ref[...])
    for j, sl in enumerate(sls):
        e_x = jnp.exp(acs_xs[j])
        parts = []
        for gi in range(MB_GROUPS):
            gl = slice(gi * gw, (gi + 1) * gw)
            off = _dot_nt(cms[j][gi], sts[j][gl])
            yg = (y_loc[j][gi] + off * e_x[:, gl] + xs[sl, gl] * dsk_ref[:, gl]) * zs[sl, gl]
            parts.append(yg * lax.rsqrt(jnp.mean(yg * yg, axis=-1, keepdims=True) + EPS))
        o_ref[sl, :] = jnp.concatenate(parts, axis=1) * nw_ref[...]


def mamba(z, xbc, dt, B, L, cw, cb, dtb, an, ex, dsk, nw, tq=512):
    T = z.shape[0]
    nc = L // tq
    hb = tq // HALO
    return pl.pallas_call(
        _mamba_kernel,
        out_shape=jax.ShapeDtypeStruct((T, MB_WIDTH), F32),
        grid=(B, nc),
        in_specs=[pl.BlockSpec((tq, MB_WIDTH), lambda b, c: (b * nc + c, 0)),
                  pl.BlockSpec((tq, MB_XBC), lambda b, c: (b * nc + c, 0)),
                  pl.BlockSpec((HALO, MB_XBC), lambda b, c: (jnp.maximum((b * nc + c) * hb - 1, 0), 0)),
                  pl.BlockSpec((tq, LANE), lambda b, c: (b * nc + c, 0)),
                  _const_spec((MB_CONV, MB_XBC)), _const_spec((1, MB_XBC)), _const_spec((1, LANE)),
                  _const_spec((1, LANE)), _const_spec((LANE, MB_WIDTH)), _const_spec((1, MB_WIDTH)),
                  _const_spec((1, MB_WIDTH))],
        out_specs=pl.BlockSpec((tq, MB_WIDTH), lambda b, c: (b * nc + c, 0)),
        scratch_shapes=[pltpu.VMEM((MB_WIDTH, MB_STATE), F32)],
        compiler_params=_cparams("parallel", "arbitrary"),
        name="mamba2",
    )(z, xbc, xbc, dt, cw, cb, dtb, an, ex, dsk, nw)


def _pack_even(w_in, wa2, ba, gnorm, w_out):
    nk = GLA_HEADS * GLA_DK
    nv = GLA_HEADS * GLA_DV
    pad_v = lambda w: jnp.pad(w.reshape(D_MODEL, GLA_HEADS, GLA_DV),
                              ((0, 0), (0, 0), (0, GLA_DVP - GLA_DV))).reshape(D_MODEL, GLA_HEADS * GLA_DVP)
    wq = w_in[:, :2 * nk]
    wv = pad_v(w_in[:, 2 * nk:2 * nk + nv])
    wg = pad_v(w_in[:, 2 * nk + nv:2 * nk + 2 * nv])
    walo = jnp.pad(w_in[:, 2 * nk + 2 * nv:2 * nk + 2 * nv + GLA_RANK], ((0, 0), (0, LANE - GLA_RANK)))
    wu = w_in[:, 2 * nk + 2 * nv + GLA_RANK:]
    w_cat = jnp.concatenate([wq, wv, wg, wu, walo], axis=1).astype(BF16)
    wa = jnp.pad(wa2.reshape(GLA_RANK, GLA_HEADS, GLA_DK).transpose(1, 0, 2),
                 ((0, 0), (0, LANE - GLA_RANK), (0, 0)))
    bah = ba.reshape(GLA_HEADS, 1, GLA_DK)
    nw = jnp.pad(gnorm, (0, GLA_DVP - GLA_DV)).reshape(1, GLA_DVP)
    woa = jnp.pad(w_out[:nv].reshape(GLA_HEADS, GLA_DV, D_MODEL),
                  ((0, 0), (0, GLA_DVP - GLA_DV), (0, 0))).reshape(GLA_HEADS * GLA_DVP, D_MODEL).astype(BF16)
    wob = w_out[nv:].astype(BF16)
    return w_cat, wa, bah, nw, woa, wob


def _pack_s5(lam_re, lam_im, log_dt, b_re, b_im, c_re, c_im):
    dt = jnp.exp(log_dt)[:, None]
    mag = jnp.exp(lam_re * dt)
    ang = lam_im * dt
    lb_re = mag * jnp.cos(ang)
    lb_im = mag * jnp.sin(ang)
    den = lam_re * lam_re + lam_im * lam_im
    nr = lb_re - 1.0
    f_re = (nr * lam_re + lb_im * lam_im) / den
    f_im = (lb_im * lam_re - nr * lam_im) / den
    bb_re = f_re[..., None] * b_re - f_im[..., None] * b_im
    bb_im = f_re[..., None] * b_im + f_im[..., None] * b_re
    eye = jnp.eye(S5_GROUPS, dtype=F32)
    bd_in = lambda m: jnp.einsum('gpc,gh->gchp', m, eye).reshape(S5_WIDTH, S5_NSTATE).astype(BF16)
    bd_out = lambda m: jnp.einsum('gcp,gh->gphc', m, eye).reshape(S5_NSTATE, S5_WIDTH).astype(BF16)
    return (lb_re.reshape(1, S5_NSTATE), lb_im.reshape(1, S5_NSTATE),
            bd_in(bb_re), bd_in(bb_im), bd_out(c_re), bd_out(c_im))


def _pack_odd(w_in, w0, w2, a0, a2):
    w_cat = jnp.pad(w_in, ((0, 0), (0, PROJ_COLS - w_in.shape[1]))).astype(BF16)
    W = RW_WIDTH
    wlr = jnp.zeros((LANE, 2 * W), F32).at[:64, :W].set(w2).at[64:, W:].set(a2)
    w0a0 = jnp.concatenate([w0, a0]).reshape(1, 2 * W)
    return w_cat, wlr, w0a0


def kernel(x, norm_mix, norm_ffn, norm_final, e_w_in, e_gla_wa2, e_gla_ba, e_gla_norm, e_s5_lambda_re, e_s5_lambda_im, e_s5_log_dt, e_s5_b_re, e_s5_b_im, e_s5_c_re, e_s5_c_im, e_s5_d, e_s5_w_glu, e_s5_b_glu, e_w_out, o_w_in, o_rw_mu, o_rw_w0, o_rw_w2, o_rw_a0, o_rw_a2, o_rw_g2, o_rw_k_k, o_rw_k_a, o_rw_r_k, o_rw_ln_w, o_rw_ln_b, o_mb_conv_w, o_mb_conv_b, o_mb_dt_bias, o_mb_a_log, o_mb_d, o_mb_norm, o_w_out, ffn_w_up, ffn_conv_w, ffn_conv_b, ffn_w_down):
    B, L, D = x.shape
    T = B * L
    depth = norm_mix.shape[0]
    x2 = x.reshape(T, D)
    row = lambda t: t.reshape(1, -1)
    for i in range(depth):
        j = i // 2
        if i % 2 == 0:
            w_cat, wa, bah, gnw, woa, wob = _pack_even(e_w_in[j], e_gla_wa2[j], e_gla_ba[j], e_gla_norm[j],
                                                       e_w_out[j])
            nkq = GLA_HEADS * GLA_DK
            nvp = GLA_HEADS * GLA_DVP
            q, k, v, g, u, alo = norm_proj(x2, row(norm_mix[i]), w_cat, (nkq, nkq, nvp, nvp, S5_WIDTH, LANE))
            ya = gla(q, k, v, g, alo, B, L, wa, bah, gnw)
            lr, li, br, bi, cr, ci = _pack_s5(e_s5_lambda_re[j], e_s5_lambda_im[j], e_s5_log_dt[j],
                                              e_s5_b_re[j], e_s5_b_im[j], e_s5_c_re[j], e_s5_c_im[j])
            yb = s5(u, B, L, lr, li, br, bi, cr, ci, row(e_s5_d[j]), e_s5_w_glu[j].astype(BF16),
                    row(e_s5_b_glu[j]))
            mix = (ya, yb, woa, wob)
        else:
            W = RW_WIDTH
            w_cat, wlr, w0a0 = _pack_odd(o_w_in[j], o_rw_w0[j], o_rw_w2[j], o_rw_a0[j], o_rw_a2[j])
            p_rw, p_z, p_xbc, p_dt = norm_proj(x2, row(norm_mix[i]), w_cat, (RW_COLS, MB_WIDTH, MB_XBC, LANE))
            head_of = jnp.arange(W) // RW_HEAD
            bd = (head_of[:, None] == head_of[None, :]).astype(BF16)
            r, a, v, lw, g, bon, kt, bt, lwt = rwkv_prep(
                p_rw, L, row(o_rw_mu[j]), wlr, w0a0, o_rw_g2[j], row(o_rw_k_k[j]), row(o_rw_k_a[j]),
                row(o_rw_r_k[j]), bd)
            yc = wkv(r, a, v, lw, g, bon, kt, bt, lwt, row(o_rw_ln_w[j]), row(o_rw_ln_b[j]), B, L)
            an = jnp.pad(-jnp.exp(o_mb_a_log[j]), (0, LANE - MB_HEADS)).reshape(1, LANE)
            dtb = jnp.pad(o_mb_dt_bias[j], (0, LANE - MB_HEADS)).reshape(1, LANE)
            ex = (jnp.arange(LANE)[:, None] == (jnp.arange(MB_WIDTH) // MB_HEADDIM)[None, :]).astype(BF16)
            dsk = jnp.repeat(o_mb_d[j], MB_HEADDIM).reshape(1, MB_WIDTH)
            yd = mamba(p_z, p_xbc, p_dt, B, L, o_mb_conv_w[j], row(o_mb_conv_b[j]), dtb, an, ex, dsk, row(o_mb_norm[j]))
            mix = (yc, yd, o_w_out[j][:W].astype(BF16), o_w_out[j][W:].astype(BF16))
        wup = ffn_w_up[i]
        x2 = ffn(x2, mix[0], mix[1], L, mix[2], mix[3], row(norm_ffn[i]), wup[:, :D_FF].astype(BF16),
                 wup[:, D_FF:].astype(BF16),
                 ffn_conv_w[i], row(ffn_conv_b[i]), ffn_w_down[i].astype(BF16), row(norm_final),
                 final_norm=(i == depth - 1))
    return x2.reshape(B, L, D)
```

```python
import functools
import math

import jax
import jax.numpy as jnp
from jax import lax
from jax.experimental import pallas as pl
from jax.experimental.pallas import tpu as pltpu

F32 = jnp.float32
BF16 = jnp.bfloat16
HI = lax.Precision.HIGHEST

D_MODEL = 1024
D_FF = 2816
EPS = 1e-6
LANE = 128
HALO = 8

GLA_HEADS = 4
GLA_DK = 128
GLA_DV = 192
GLA_DVP = 256
GLA_RANK = 16
GLA_TAU = 16.0
GLA_CHUNK = 128
S5_WIDTH = 256
S5_GROUPS = 16
S5_GROUP = 16
S5_STATE = 64
S5_NSTATE = S5_GROUPS * S5_STATE
S5_CHUNK = 256
RW_WIDTH = 512
RW_HEAD = 64
RW_COLS = 1792
RW_GN_EPS = 64e-5
RW_CHUNK = 128
MB_WIDTH = 512
MB_HEADS = 8
MB_HEADDIM = 64
MB_GROUPS = 2
MB_STATE = 128
MB_CONV = 4
MB_CHUNK = 128
MB_XBC = 1024
PROJ_COLS = 3456

VMEM_LIMIT = 56 * 1024 * 1024


def _cparams(*sem):
    return pltpu.CompilerParams(dimension_semantics=sem, vmem_limit_bytes=VMEM_LIMIT)


def _dot(a, b, precision=None):
    return jnp.dot(a, b, preferred_element_type=F32, precision=precision)


def _dot_nt(a, b, precision=None):
    return lax.dot_general(a, b, (((1,), (1,)), ((), ())), preferred_element_type=F32, precision=precision)


def _bdot(a, b):
    return _dot(a.astype(BF16), b.astype(BF16))


def _bdot_nt(a, b):
    return _dot_nt(a.astype(BF16), b.astype(BF16))


def _split(x):
    hi = x.astype(BF16)
    return hi, (x - hi.astype(F32)).astype(BF16)


def _dot_x2(x, e, pack=False):
    hi, lo = _split(x)
    if pack:
        return _dot(jnp.concatenate([hi, lo], axis=1), jnp.concatenate([e, e], axis=0))
    return _dot(hi, e) + _dot(lo, e)


def _dot_2x(e, x, pack=False):
    hi, lo = _split(x)
    if pack:
        return _dot(jnp.concatenate([e, e], axis=1), jnp.concatenate([hi, lo], axis=0))
    return _dot(e, hi) + _dot(e, lo)


def _dot_x3(a, b):
    ah, al = _split(a)
    bh, bl = _split(b)
    return _dot(ah, bh) + (_dot(al, bh) + _dot(ah, bl))


def _rms(x, w):
    return x * lax.rsqrt(jnp.mean(x * x, axis=-1, keepdims=True) + EPS) * w


def _sigmoid(x):
    return 1.0 / (1.0 + jnp.exp(-x))


def _silu(x):
    return x * _sigmoid(x)


def _softplus(x):
    return jnp.maximum(x, 0.0) + jnp.log(1.0 + jnp.exp(-jnp.abs(x)))


def _iota2(shape, dim):
    return lax.broadcasted_iota(jnp.int32, shape, dim)


def _const_spec(shape):
    nd = len(shape)
    return pl.BlockSpec(shape, lambda *_: (0,) * nd)


def _norm_proj_kernel(x_ref, nw_ref, w_ref, *o_refs):
    h = _rms(x_ref[...], nw_ref[...])
    y = _dot(h.astype(BF16), w_ref[...])
    off = 0
    for o_ref in o_refs:
        n = o_ref.shape[1]
        o_ref[...] = y[:, off:off + n]
        off += n


def norm_proj(x2, nw, w, splits, tm=512):
    T, D = x2.shape
    N = w.shape[1]
    assert sum(splits) == N
    return pl.pallas_call(
        _norm_proj_kernel,
        out_shape=[jax.ShapeDtypeStruct((T, n), F32) for n in splits],
        grid=(T // tm,),
        in_specs=[pl.BlockSpec((tm, D), lambda i: (i, 0)), _const_spec((1, D)), _const_spec((D, N))],
        out_specs=[pl.BlockSpec((tm, n), lambda i: (i, 0)) for n in splits],
        compiler_params=_cparams("parallel"),
        name="norm_proj",
    )(x2, nw, w)


def _ffn_kernel(x_ref, xh_ref, ya_ref, yah_ref, yb_ref, ybh_ref, wa_ref, wb_ref, nw_ref, wg_ref, wu_ref,
                cw_ref, cb_ref, wd_ref, fnw_ref, o_ref, *, tiles_per_seq, final_norm):
    tm = x_ref.shape[0]
    ya = jnp.concatenate([yah_ref[...], ya_ref[...]], axis=0).astype(BF16)
    yb = jnp.concatenate([ybh_ref[...], yb_ref[...]], axis=0).astype(BF16)
    xa = (jnp.concatenate([xh_ref[...], x_ref[...]], axis=0)
          + _dot(ya, wa_ref[...]) + _dot(yb, wb_ref[...]))
    x = xa[HALO:]
    h = _rms(xa, nw_ref[...]).astype(BF16)
    gate = _dot(h, wg_ref[...])
    halo_dead = jnp.where((pl.program_id(0) % tiles_per_seq) == 0, HALO, 0)
    gate = jnp.where(_iota2((HALO + tm, 1), 0) < halo_dead, 0.0, gate)
    cw = cw_ref[...]
    conv = (cw[2:3] * gate[HALO:]
            + cw[1:2] * pltpu.roll(gate, 1, 0)[HALO:]
            + cw[0:1] * pltpu.roll(gate, 2, 0)[HALO:]) + cb_ref[...]
    up = _dot(h[HALO:], wu_ref[...])
    act = (_silu(conv) * up).astype(BF16)
    y = x + _dot(act, wd_ref[...])
    if final_norm:
        y = _rms(y, fnw_ref[...])
    o_ref[...] = y


def ffn(x2, ya, yb, seq_len, wa, wb, nw, wg, wu, cw, cb, wd, fnw, final_norm, tm=256):
    T, D = x2.shape
    FF = wg.shape[1]
    ka, kb = ya.shape[1], yb.shape[1]
    hb = tm // HALO
    tile = lambda n: pl.BlockSpec((tm, n), lambda i: (i, 0))
    halo = lambda n: pl.BlockSpec((HALO, n), lambda i: (jnp.maximum(i * hb - 1, 0), 0))
    kern = functools.partial(_ffn_kernel, tiles_per_seq=seq_len // tm, final_norm=final_norm)
    return pl.pallas_call(
        kern,
        out_shape=jax.ShapeDtypeStruct((T, D), F32),
        grid=(T // tm,),
        in_specs=[tile(D), halo(D), tile(ka), halo(ka), tile(kb), halo(kb),
                  _const_spec((ka, D)), _const_spec((kb, D)),
                  _const_spec((1, D)), _const_spec((D, FF)), _const_spec((D, FF)),
                  _const_spec((3, FF)), _const_spec((1, FF)), _const_spec((FF, D)), _const_spec((1, D))],
        out_specs=tile(D),
        compiler_params=_cparams("parallel"),
        name="ffn",
    )(x2, x2, ya, ya, yb, yb, wa, wb, nw, wg, wu, cw, cb, wd, fnw)


def _gla_kernel(q_ref, k_ref, v_ref, g_ref, alo_ref, wa_ref, ba_ref, nw_ref, o_ref, st_ref):
    @pl.when(pl.program_id(2) == 0)
    def _():
        st_ref[...] = jnp.zeros_like(st_ref)

    C = GLA_CHUNK
    tq = q_ref.shape[0]
    nh = st_ref.shape[0]
    ri = _iota2((C, C), 0)
    ci = _iota2((C, C), 1)
    tril = (ri >= ci)
    tril_b = jnp.where(tril, 1.0, 0.0).astype(BF16)
    scale = GLA_DK ** -0.5
    alo = alo_ref[...].astype(BF16)
    sls = [slice(j * C, (j + 1) * C) for j in range(tq // C)]
    o_in, kdv, qd, dec = {}, {}, {}, {}
    for h in range(nh):
        kl = slice(h * GLA_DK, (h + 1) * GLA_DK)
        vl = slice(h * GLA_DVP, (h + 1) * GLA_DVP)
        xg = _dot(alo, wa_ref[h].astype(BF16)) + ba_ref[h]
        la = (jnp.minimum(xg, 0.0) - jnp.log(1.0 + jnp.exp(-jnp.abs(xg)))) * (1.0 / GLA_TAU)
        for j, sl in enumerate(sls):
            b = _dot_2x(tril_b, la[sl])
            b_mid = b[C // 2 - 1:C // 2]
            b_last = b[C - 1:C]
            q = q_ref[sl, kl] * scale
            k = k_ref[sl, kl]
            v = v_ref[sl, vl]
            qe = q * jnp.exp(b - b_mid)
            ke = k * jnp.exp(jnp.minimum(b_mid - b, 80.0))
            att = jnp.where(tril, _bdot_nt(qe, ke), 0.0)
            o_in[h, j] = _bdot(att, v)
            kdv[h, j] = _bdot(v.T, k * jnp.exp(b_last - b))
            qd[h, j] = (q * jnp.exp(b)).astype(BF16)
            dec[h, j] = jnp.exp(b_last)
    sts = {}
    for h in range(nh):
        st = st_ref[h]
        for j in range(len(sls)):
            sts[h, j] = st.astype(BF16)
            st = st * dec[h, j] + kdv[h, j]
        st_ref[h] = st
    for h in range(nh):
        vl = slice(h * GLA_DVP, (h + 1) * GLA_DVP)
        for j, sl in enumerate(sls):
            o = o_in[h, j] + _dot_nt(qd[h, j], sts[h, j])
            ms = jnp.sum(o * o, axis=-1, keepdims=True) * (1.0 / GLA_DV)
            o = o * lax.rsqrt(ms + EPS) * nw_ref[...]
            o_ref[sl, vl] = o * _silu(g_ref[sl, vl])


def gla(q, k, v, g, alo, B, L, wa, ba, nw, tq=512, nh=4):
    T = q.shape[0]
    nq = L // tq
    heads = lambda w: pl.BlockSpec((tq, nh * w), lambda b, h, c: (b * nq + c, h))
    return pl.pallas_call(
        _gla_kernel,
        out_shape=jax.ShapeDtypeStruct((T, GLA_HEADS * GLA_DVP), F32),
        grid=(B, GLA_HEADS // nh, nq),
        in_specs=[heads(GLA_DK), heads(GLA_DK), heads(GLA_DVP), heads(GLA_DVP),
                  pl.BlockSpec((tq, LANE), lambda b, h, c: (b * nq + c, 0)),
                  pl.BlockSpec((nh, LANE, GLA_DK), lambda b, h, c: (h, 0, 0)),
                  pl.BlockSpec((nh, 1, GLA_DK), lambda b, h, c: (h, 0, 0)),
                  _const_spec((1, GLA_DVP))],
        out_specs=heads(GLA_DVP),
        scratch_shapes=[pltpu.VMEM((nh, GLA_DVP, GLA_DK), F32)],
        compiler_params=_cparams("parallel", "parallel", "arbitrary"),
        name="gla",
    )(q, k, v, g, alo, wa, ba, nw)


def _cmul(ar, ai, br, bi):
    return ar * br - ai * bi, ar * bi + ai * br


def _s5_kernel(u_ref, perm_ref, unperm_ref, lr_ref, li_ref, br_ref, bi_ref, cr_ref, ci_ref, d_ref, wg_ref,
               bg_ref, o_ref, sr_ref, si_ref, pr_ref, pi_ref, xr_ref, xi_ref):
    tc = u_ref.shape[0]
    seg = tc // HALO
    ns = lr_ref.shape[1]
    lr = lr_ref[...]
    li = li_ref[...]
    lr8 = jnp.broadcast_to(lr, (HALO, ns))
    li8 = jnp.broadcast_to(li, (HALO, ns))
    rows = lambda j: slice(j * HALO, (j + 1) * HALO)

    @pl.when(pl.program_id(1) == 0)
    def _():
        sr_ref[...] = jnp.zeros_like(sr_ref)
        si_ref[...] = jnp.zeros_like(si_ref)
        zr, zi = lr8, li8
        for j in range(seg):
            pr_ref[rows(j), :] = zr
            pi_ref[rows(j), :] = zi
            zr, zi = _cmul(lr8, li8, zr, zi)

    u = u_ref[...]
    ub = _dot(perm_ref[...], u.astype(BF16)).astype(BF16)
    xr_ref[...] = _dot(ub, br_ref[...])
    xi_ref[...] = _dot(ub, bi_ref[...])
    er = xr_ref[rows(0), :]
    ei = xi_ref[rows(0), :]
    for j in range(1, seg):
        tr, ti = _cmul(lr8, li8, er, ei)
        er = xr_ref[rows(j), :] + tr
        ei = xi_ref[rows(j), :] + ti
        xr_ref[rows(j), :] = er
        xi_ref[rows(j), :] = ei
    mr, mi = lr, li
    for _ in range(seg.bit_length() - 1):
        mr, mi = _cmul(mr, mi, mr, mi)
    row8 = _iota2((HALO, 1), 0)
    cr0, ci0 = _cmul(mr, mi, sr_ref[...], si_ref[...])
    er = er + jnp.where(row8 == 0, cr0, 0.0)
    ei = ei + jnp.where(row8 == 0, ci0, 0.0)
    sh = 1
    while sh < HALO:
        yr, yi = _cmul(mr, mi, jnp.where(row8 < sh, 0.0, pltpu.roll(er, sh, 0)),
                       jnp.where(row8 < sh, 0.0, pltpu.roll(ei, sh, 0)))
        er, ei = er + yr, ei + yi
        mr, mi = _cmul(mr, mi, mr, mi)
        sh *= 2
    inr = jnp.where(row8 == 0, sr_ref[...], pltpu.roll(er, 1, 0))
    ini = jnp.where(row8 == 0, si_ref[...], pltpu.roll(ei, 1, 0))
    sr_ref[...] = er[HALO - 1:]
    si_ref[...] = ei[HALO - 1:]
    ar, ai = _cmul(pr_ref[...].reshape(seg, HALO, ns), pi_ref[...].reshape(seg, HALO, ns), inr[None], ini[None])
    xr = xr_ref[...] + ar.reshape(tc, ns)
    xi = xi_ref[...] + ai.reshape(tc, ns)
    yp = _bdot(xr, cr_ref[...]) - _bdot(xi, ci_ref[...])
    y = _dot_2x(unperm_ref[...], yp) + d_ref[...] * u
    z = 0.5 * y * (1.0 + jnp.tanh(math.sqrt(2.0 / math.pi) * (y + 0.044715 * (y * y * y))))
    o_ref[...] = z * _sigmoid(_bdot(z, wg_ref[...]) + bg_ref[...])


def s5(u, B, L, lr, li, br, bi, cr, ci, d, wg, bg, tc=S5_CHUNK):
    T = u.shape[0]
    nc = L // tc
    NS = S5_NSTATE
    seg = tc // HALO
    rho = jnp.arange(tc)
    perm = (jnp.arange(tc)[None, :] == (seg * (rho % HALO) + rho // HALO)[:, None]).astype(BF16)
    big = pltpu.VMEM((tc, NS), F32)
    return pl.pallas_call(
        _s5_kernel,
        out_shape=jax.ShapeDtypeStruct((T, S5_WIDTH), F32),
        grid=(B, nc),
        in_specs=[pl.BlockSpec((tc, S5_WIDTH), lambda b, c: (b * nc + c, 0)),
                  _const_spec((tc, tc)), _const_spec((tc, tc)),
                  _const_spec((1, NS)), _const_spec((1, NS)),
                  _const_spec((S5_WIDTH, NS)), _const_spec((S5_WIDTH, NS)),
                  _const_spec((NS, S5_WIDTH)), _const_spec((NS, S5_WIDTH)),
                  _const_spec((1, S5_WIDTH)), _const_spec((S5_WIDTH, S5_WIDTH)), _const_spec((1, S5_WIDTH))],
        out_specs=pl.BlockSpec((tc, S5_WIDTH), lambda b, c: (b * nc + c, 0)),
        scratch_shapes=[pltpu.VMEM((1, NS), F32), pltpu.VMEM((1, NS), F32), big, big, big, big],
        compiler_params=_cparams("parallel", "arbitrary"),
        name="s5",
    )(u, perm, perm.T, lr, li, br, bi, cr, ci, d, wg, bg)


def _rwkv_prep_kernel(p_ref, ph_ref, mu_ref, wlr_ref, w0a0_ref, g2_ref, kk_ref, ka_ref, rk_ref, bd_ref,
                      r_ref, a_ref, v_ref, lw_ref, g_ref, bon_ref, kt_ref, bt_ref, lwt_ref, *, tiles_per_seq):
    tm = p_ref.shape[0]
    W = RW_WIDTH
    p = p_ref[...]
    pa = jnp.concatenate([ph_ref[...], p], axis=0)
    prev = pltpu.roll(pa, 1, 0)[HALO:]
    first_dead = jnp.where((pl.program_id(0) % tiles_per_seq) == 0, 1, 0)
    prev = jnp.where(_iota2((tm, 1), 0) < first_dead, 0.0, prev)
    pm = p + (prev - p) * mu_ref[...]
    r = pm[:, :W]
    k = pm[:, W:2 * W]
    v = pm[:, 2 * W:3 * W]
    xwa = pm[:, 3 * W:3 * W + LANE]
    xg = pm[:, 3 * W + LANE:]
    xwa = jnp.where(_iota2((tm, LANE), 1) < 64, jnp.tanh(xwa), xwa)
    wa = _dot_x3(xwa, wlr_ref[...]) + w0a0_ref[...]
    wlog = -_softplus(-wa[:, :W]) - 0.5
    lw = -jnp.exp(wlog)
    a = _sigmoid(wa[:, W:])
    g = _bdot(_sigmoid(xg), g2_ref[...])
    bd = bd_ref[...]
    kk = k * kk_ref[...]
    kk = kk / jnp.maximum(jnp.sqrt(_dot_x2(kk * kk, bd)), 1e-12)
    k2 = k * (1.0 + (a - 1.0) * ka_ref[...])
    r_ref[...] = r
    a_ref[...] = -kk
    v_ref[...] = v
    lw_ref[...] = lw
    g_ref[...] = g.astype(BF16)
    bon_ref[...] = (_dot_x2(r * k2 * rk_ref[...], bd) * v).astype(BF16)
    kt_ref[...] = k2.T
    bt_ref[...] = (kk * a).T
    lwt_ref[...] = lw.T


def rwkv_prep(p, L, mu, wlr, w0a0, g2, kk, ka, rk, bd, tm=512):
    T = p.shape[0]
    W = RW_WIDTH
    hb = tm // HALO
    tok = jax.ShapeDtypeStruct((T, W), F32)
    chn = jax.ShapeDtypeStruct((W, T), F32)
    tok_spec = pl.BlockSpec((tm, W), lambda i: (i, 0))
    chn_spec = pl.BlockSpec((W, tm), lambda i: (0, i))
    kern = functools.partial(_rwkv_prep_kernel, tiles_per_seq=L // tm)
    return pl.pallas_call(
        kern,
        out_shape=[tok] * 4 + [jax.ShapeDtypeStruct((T, W), BF16)] * 2 + [chn] * 3,
        grid=(T // tm,),
        in_specs=[pl.BlockSpec((tm, RW_COLS), lambda i: (i, 0)),
                  pl.BlockSpec((HALO, RW_COLS), lambda i: (jnp.maximum(i * hb - 1, 0), 0)),
                  _const_spec((1, RW_COLS)), _const_spec((LANE, 2 * W)), _const_spec((1, 2 * W)),
                  _const_spec((LANE, W)), _const_spec((1, W)), _const_spec((1, W)), _const_spec((1, W)),
                  _const_spec((W, W))],
        out_specs=[tok_spec] * 6 + [chn_spec] * 3,
        compiler_params=_cparams("parallel"),
        name="rwkv_prep",
    )(p, p, mu, wlr, w0a0, g2, kk, ka, rk, bd)


def _wkv_local_kernel(r_ref, a_ref, v_ref, lw_ref, kt_ref, bt_ref, lwt_ref, wr_ref, y0_ref, m_ref, hl_ref):
    C = RW_CHUNK
    nch = r_ref.shape[0] // C
    ri = _iota2((C, C), 0)
    ci = _iota2((C, C), 1)
    incl = ri >= ci
    strict = ri > ci
    tril_b = jnp.where(incl, 1.0, 0.0).astype(BF16)
    triu_b = jnp.where(ri <= ci, 1.0, 0.0).astype(BF16)
    m0 = ci < RW_HEAD
    bdm = (ri // RW_HEAD) == (ci // RW_HEAD)
    swap = lambda t: pltpu.roll(t, RW_HEAD, 1)
    sls = [slice(j * C, (j + 1) * C) for j in range(nch)]
    chunks = range(nch)
    pairs = [(j, h) for j in chunks for h in range(2)]
    r = [r_ref[sl, :] for sl in sls]
    a = [a_ref[sl, :] for sl in sls]
    lw = [lw_ref[sl, :] for sl in sls]
    kt = [kt_ref[:, sl] for sl in sls]
    bt = [bt_ref[:, sl] for sl in sls]
    v = [v_ref[sl, :] for sl in sls]
    vb = [t.astype(BF16) for t in v]
    c = [_dot_2x(tril_b, lw[j], pack=True) for j in chunks]
    ct = [_dot_x2(lwt_ref[:, sls[j]], triu_b, pack=True) for j in chunks]
    aa = []
    for j in chunks:
        c_mid = c[j][C // 2 - 1:C // 2]
        ct_mid = ct[j][:, C // 2 - 1:C // 2]
        at = a[j] * jnp.exp(c[j] - lw[j] - c_mid)
        rt = r[j] * jnp.exp(c[j] - c_mid)
        e_mid = jnp.exp(ct_mid - ct[j])
        lhs = jnp.concatenate([jnp.where(m0, at, 0.0), jnp.where(m0, 0.0, at),
                               jnp.where(m0, rt, 0.0), jnp.where(m0, 0.0, rt)], axis=0)
        rhs = jnp.concatenate([bt[j] * e_mid, kt[j] * e_mid], axis=1)
        aa.append(_bdot(lhs, rhs))
    n = {(j, h): jnp.where(strict, aa[j][h * C:(h + 1) * C, :C], 0.0) for j, h in pairs}
    ak = {(j, h): jnp.where(strict, aa[j][h * C:(h + 1) * C, C:], 0.0).astype(BF16) for j, h in pairs}
    rb = {(j, h): jnp.where(incl, aa[j][(2 + h) * C:(3 + h) * C, :C], 0.0).astype(BF16) for j, h in pairs}
    rk = {(j, h): jnp.where(incl, aa[j][(2 + h) * C:(3 + h) * C, C:], 0.0).astype(BF16) for j, h in pairs}
    vh = {}
    x = {}
    for j in chunks:
        a_abs = a[j] * jnp.exp(c[j] - lw[j])
        vh[j, 0] = jnp.where(m0, 0.0, swap(v[j])).astype(BF16)
        vh[j, 1] = jnp.where(m0, 0.0, v[j]).astype(BF16)
        x[j, 0] = jnp.where(m0, a_abs, _dot(ak[j, 0], vh[j, 0]))
        x[j, 1] = jnp.where(m0, swap(a_abs), _dot(ak[j, 1], vh[j, 1]))
    sh = 1
    while sh < C:
        top = sh if sh % (2 * HALO) == 0 else 0
        nb = {p: n[p].astype(BF16) for p in pairs}
        if 2 * sh < C:
            res = {p: _dot(nb[p][top:], jnp.concatenate([x[p].astype(BF16), nb[p]], axis=1)) for p in pairs}
            n = {p: res[p][:, C:] for p in pairs}
        else:
            res = {p: _dot(nb[p][top:], x[p].astype(BF16)) for p in pairs}
        if top:
            x = {p: jnp.concatenate([x[p][:top], x[p][top:] + res[p][:, :C]], axis=0) for p in pairs}
            if 2 * sh < C:
                n = {p: jnp.concatenate([jnp.zeros((top, C), F32), n[p]], axis=0) for p in pairs}
        else:
            x = {p: x[p] + res[p][:, :C] for p in pairs}
        sh *= 2
    g = {p: _dot(jnp.concatenate([rb[p], rk[p]], axis=1), jnp.concatenate([x[p].astype(BF16), vh[p]], axis=0))
         for p in pairs}
    for j in chunks:
        wa = jnp.where(m0, x[j, 0], swap(x[j, 1]))
        u0 = jnp.where(m0, swap(x[j, 0]), x[j, 1])
        wr_ref[sls[j], :] = (r[j] * jnp.exp(c[j]) + jnp.where(m0, g[j, 0], swap(g[j, 1]))).astype(BF16)
        y0_ref[sls[j], :] = jnp.where(m0, swap(g[j, 0]), g[j, 1])
        ct_last = ct[j][:, C - 1:C]
        e_last = jnp.exp(ct_last - ct[j])
        bh = (bt[j] * e_last).astype(BF16)
        kh = (kt[j] * e_last).astype(BF16)
        rhs = jnp.concatenate([jnp.concatenate([wa.astype(BF16), u0.astype(BF16)], axis=1),
                               jnp.concatenate([jnp.zeros((C, C), BF16), vb[j]], axis=1)], axis=0)
        mh = _dot(jnp.concatenate([bh, kh], axis=1), rhs)
        m_ref[j, 0] = jnp.where(bdm, mh[:, :C], 0.0) + jnp.where(ri == ci, jnp.exp(ct_last), 0.0)
        hl_ref[j, 0] = jnp.where(bdm, mh[:, C:], 0.0)


def wkv_local(r, a, v, lw, kt, bt, lwt, tq=512):
    T = r.shape[0]
    C = RW_CHUNK
    npair = RW_WIDTH // LANE
    tok = pl.BlockSpec((tq, LANE), lambda i, h: (i, h))
    chn = pl.BlockSpec((LANE, tq), lambda i, h: (h, i))
    mat = pl.BlockSpec((tq // C, 1, LANE, LANE), lambda i, h: (i, h, 0, 0))
    mat_shape = jax.ShapeDtypeStruct((T // C, npair, LANE, LANE), F32)
    return pl.pallas_call(
        _wkv_local_kernel,
        out_shape=[jax.ShapeDtypeStruct((T, RW_WIDTH), BF16), jax.ShapeDtypeStruct((T, RW_WIDTH), F32),
                   mat_shape, mat_shape],
        grid=(T // tq, npair),
        in_specs=[tok] * 4 + [chn] * 3,
        out_specs=[tok, tok, mat, mat],
        compiler_params=_cparams("parallel", "parallel"),
        name="wkv_local",
    )(r, a, v, lw, kt, bt, lwt)


def _wkv_scan_kernel(wr_ref, y0_ref, m_ref, hl_ref, g_ref, bon_ref, lnw_ref, lnb_ref, o_ref, h_ref):
    @pl.when(pl.program_id(1) == 0)
    def _():
        h_ref[...] = jnp.zeros_like(h_ref)

    C = RW_CHUNK
    nch = wr_ref.shape[0] // C
    npair = h_ref.shape[0]
    bdm = (_iota2((C, C), 0) // RW_HEAD) == (_iota2((C, C), 1) // RW_HEAD)
    gn = jnp.where(bdm, 1.0 / RW_HEAD, 0.0).astype(BF16)
    h = [h_ref[p] for p in range(npair)]
    for j in range(nch):
        sl = slice(j * C, (j + 1) * C)
        ys = []
        for p in range(npair):
            ln = slice(p * LANE, (p + 1) * LANE)
            ys.append(_dot(wr_ref[sl, ln], h[p].astype(BF16)) + y0_ref[sl, ln])
        h = [_dot_x3(m_ref[j, p], h[p]) + hl_ref[j, p] for p in range(npair)]
        for p in range(npair):
            ln = slice(p * LANE, (p + 1) * LANE)
            mean = _dot_x2(ys[p], gn, pack=True)
            yc = ys[p] - mean
            var = _dot_x2(yc * yc, gn, pack=True)
            yn = yc * lax.rsqrt(var + RW_GN_EPS) * lnw_ref[:, ln] + lnb_ref[:, ln]
            o_ref[sl, ln] = (yn + bon_ref[sl, ln]) * g_ref[sl, ln]
    for p in range(npair):
        h_ref[p] = h[p]


def wkv_scan(wr, y0, m, hl, g, bon, lnw, lnb, B, L, tq=512):
    T, W = wr.shape
    C = RW_CHUNK
    nq = L // tq
    npair = W // LANE
    tok = pl.BlockSpec((tq, W), lambda b, c: (b * nq + c, 0))
    mat = pl.BlockSpec((tq // C, npair, LANE, LANE), lambda b, c: (b * nq + c, 0, 0, 0))
    return pl.pallas_call(
        _wkv_scan_kernel,
        out_shape=jax.ShapeDtypeStruct((T, W), F32),
        grid=(B, nq),
        in_specs=[tok, tok, mat, mat, tok, tok, _const_spec((1, W)), _const_spec((1, W))],
        out_specs=tok,
        scratch_shapes=[pltpu.VMEM((npair, LANE, LANE), F32)],
        compiler_params=_cparams("parallel", "arbitrary"),
        name="wkv_scan",
    )(wr, y0, m, hl, g, bon, lnw, lnb)


def wkv(r, a, v, lw, g, bon, kt, bt, lwt, lnw, lnb, B, L):
    wr, y0, m, hl = wkv_local(r, a, v, lw, kt, bt, lwt)
    return wkv_scan(wr, y0, m, hl, g, bon, lnw, lnb, B, L)


def _mamba_kernel(z_ref, xbc_ref, xh_ref, dt_ref, cw_ref, cb_ref, dtb_ref, an_ref, ex_ref, dsk_ref, nw_ref,
                  o_ref, st_ref):
    @pl.when(pl.program_id(1) == 0)
    def _():
        st_ref[...] = jnp.zeros_like(st_ref)

    C = MB_CHUNK
    W = MB_WIDTH
    N = MB_STATE
    tq = z_ref.shape[0]
    hpg = MB_HEADS // MB_GROUPS
    gw = W // MB_GROUPS
    xbc = xbc_ref[...]
    xa = jnp.concatenate([xh_ref[...], xbc], axis=0)
    halo_dead = jnp.where(pl.program_id(1) == 0, HALO, 0)
    xa = jnp.where(_iota2((HALO + tq, 1), 0) < halo_dead, 0.0, xa)
    cw = cw_ref[...]
    conv = cw[3:4] * xbc + cb_ref[...]
    for kk in range(1, MB_CONV):
        conv = conv + cw[3 - kk:4 - kk] * pltpu.roll(xa, kk, 0)[HALO:]
    xbc = _silu(conv)
    xs = xbc[:, :W]
    dt = _softplus(dt_ref[...] + dtb_ref[...])
    a = dt * an_ref[...]
    ex = ex_ref[...]
    xdt = xs * _dot_x2(dt, ex, pack=True)
    xdt_b = xdt.astype(BF16)
    ri = _iota2((C, C), 0)
    ci = _iota2((C, C), 1)
    causal = ri >= ci
    tril_b = jnp.where(causal, 1.0, 0.0).astype(BF16)
    head_g = _iota2((C, gw), 1) // MB_HEADDIM
    sls = [slice(j * C, (j + 1) * C) for j in range(tq // C)]
    y_loc, upd, cdec, acs_xs, cms = [], [], [], [], []
    for sl in sls:
        acs = _dot_2x(tril_b, a[sl], pack=True)
        acs_t = acs.T
        acs_x = _dot_x2(acs, ex, pack=True)
        ys, cm_g = [], []
        for gi in range(MB_GROUPS):
            bm = xbc[sl, W + gi * N:W + (gi + 1) * N]
            cm = xbc[sl, W + MB_GROUPS * N + gi * N:W + MB_GROUPS * N + (gi + 1) * N].astype(BF16)
            cb = _dot_nt(cm, bm.astype(BF16))
            lm = []
            for e in range(hpg):
                hd = gi * hpg + e
                lmat = jnp.exp(jnp.where(causal, acs[:, hd:hd + 1] - acs_t[hd:hd + 1, :], -jnp.inf))
                lm.append((cb * lmat).astype(BF16))
            res = _dot(jnp.concatenate(lm, axis=0), xdt_b[sl, gi * gw:(gi + 1) * gw])
            yg = res[:C]
            for e in range(1, hpg):
                yg = jnp.where(head_g == e, res[e * C:(e + 1) * C], yg)
            ys.append(yg)
            cm_g.append(cm)
        y_loc.append(ys)
        cms.append(cm_g)
        acs_xs.append(acs_x)
        acs_last = acs_x[C - 1:C]
        xw_t = (xdt[sl] * jnp.exp(acs_last - acs_x)).T
        upd.append(jnp.concatenate(
            [_bdot(xw_t[gi * gw:(gi + 1) * gw], xbc[sl, W + gi * N:W + (gi + 1) * N]) for gi in range(MB_GROUPS)],
            axis=0))
        cdec.append(jnp.exp(acs_x.T[:, C - 1:C]))
    st = st_ref[...]
    sts = []
    for j in range(len(sls)):
        sts.append(st.astype(BF16))
        st = cdec[j] * st + upd[j]
    st_ref[...] = st
    zs = _silu(z_ref[...])
    for j, sl in enumerate(sls):
        e_x = jnp.exp(acs_xs[j])
        parts = []
        for gi in range(MB_GROUPS):
            gl = slice(gi * gw, (gi + 1) * gw)
            off = _dot_nt(cms[j][gi], sts[j][gl])
            yg = (y_loc[j][gi] + off * e_x[:, gl] + xs[sl, gl] * dsk_ref[:, gl]) * zs[sl, gl]
            parts.append(yg * lax.rsqrt(jnp.mean(yg * yg, axis=-1, keepdims=True) + EPS))
        o_ref[sl, :] = jnp.concatenate(parts, axis=1) * nw_ref[...]


def mamba(z, xbc, dt, B, L, cw, cb, dtb, an, ex, dsk, nw, tq=512):
    T = z.shape[0]
    nc = L // tq
    hb = tq // HALO
    return pl.pallas_call(
        _mamba_kernel,
        out_shape=jax.ShapeDtypeStruct((T, MB_WIDTH), F32),
        grid=(B, nc),
        in_specs=[pl.BlockSpec((tq, MB_WIDTH), lambda b, c: (b * nc + c, 0)),
                  pl.BlockSpec((tq, MB_XBC), lambda b, c: (b * nc + c, 0)),
                  pl.BlockSpec((HALO, MB_XBC), lambda b, c: (jnp.maximum((b * nc + c) * hb - 1, 0), 0)),
                  pl.BlockSpec((tq, LANE), lambda b, c: (b * nc + c, 0)),
                  _const_spec((MB_CONV, MB_XBC)), _const_spec((1, MB_XBC)), _const_spec((1, LANE)),
                  _const_spec((1, LANE)), _const_spec((LANE, MB_WIDTH)), _const_spec((1, MB_WIDTH)),
                  _const_spec((1, MB_WIDTH))],
        out_specs=pl.BlockSpec((tq, MB_WIDTH), lambda b, c: (b * nc + c, 0)),
        scratch_shapes=[pltpu.VMEM((MB_WIDTH, MB_STATE), F32)],
        compiler_params=_cparams("parallel", "arbitrary"),
        name="mamba2",
    )(z, xbc, xbc, dt, cw, cb, dtb, an, ex, dsk, nw)


def _pack_even(w_in, wa2, ba, gnorm, w_out):
    nk = GLA_HEADS * GLA_DK
    nv = GLA_HEADS * GLA_DV
    pad_v = lambda w: jnp.pad(w.reshape(D_MODEL, GLA_HEADS, GLA_DV),
                              ((0, 0), (0, 0), (0, GLA_DVP - GLA_DV))).reshape(D_MODEL, GLA_HEADS * GLA_DVP)
    wq = w_in[:, :2 * nk]
    wv = pad_v(w_in[:, 2 * nk:2 * nk + nv])
    wg = pad_v(w_in[:, 2 * nk + nv:2 * nk + 2 * nv])
    walo = jnp.pad(w_in[:, 2 * nk + 2 * nv:2 * nk + 2 * nv + GLA_RANK], ((0, 0), (0, LANE - GLA_RANK)))
    wu = w_in[:, 2 * nk + 2 * nv + GLA_RANK:]
    w_cat = jnp.concatenate([wq, wv, wg, wu, walo], axis=1).astype(BF16)
    wa = jnp.pad(wa2.reshape(GLA_RANK, GLA_HEADS, GLA_DK).transpose(1, 0, 2),
                 ((0, 0), (0, LANE - GLA_RANK), (0, 0)))
    bah = ba.reshape(GLA_HEADS, 1, GLA_DK)
    nw = jnp.pad(gnorm, (0, GLA_DVP - GLA_DV)).reshape(1, GLA_DVP)
    woa = jnp.pad(w_out[:nv].reshape(GLA_HEADS, GLA_DV, D_MODEL),
                  ((0, 0), (0, GLA_DVP - GLA_DV), (0, 0))).reshape(GLA_HEADS * GLA_DVP, D_MODEL).astype(BF16)
    wob = w_out[nv:].astype(BF16)
    return w_cat, wa, bah, nw, woa, wob


def _pack_s5(lam_re, lam_im, log_dt, b_re, b_im, c_re, c_im):
    dt = jnp.exp(log_dt)[:, None]
    mag = jnp.exp(lam_re * dt)
    ang = lam_im * dt
    lb_re = mag * jnp.cos(ang)
    lb_im = mag * jnp.sin(ang)
    den = lam_re * lam_re + lam_im * lam_im
    nr = lb_re - 1.0
    f_re = (nr * lam_re + lb_im * lam_im) / den
    f_im = (lb_im * lam_re - nr * lam_im) / den
    bb_re = f_re[..., None] * b_re - f_im[..., None] * b_im
    bb_im = f_re[..., None] * b_im + f_im[..., None] * b_re
    eye = jnp.eye(S5_GROUPS, dtype=F32)
    bd_in = lambda m: jnp.einsum('gpc,gh->gchp', m, eye).reshape(S5_WIDTH, S5_NSTATE).astype(BF16)
    bd_out = lambda m: jnp.einsum('gcp,gh->gphc', m, eye).reshape(S5_NSTATE, S5_WIDTH).astype(BF16)
    return (lb_re.reshape(1, S5_NSTATE), lb_im.reshape(1, S5_NSTATE),
            bd_in(bb_re), bd_in(bb_im), bd_out(c_re), bd_out(c_im))


def _pack_odd(w_in, w0, w2, a0, a2):
    w_cat = jnp.pad(w_in, ((0, 0), (0, PROJ_COLS - w_in.shape[1]))).astype(BF16)
    W = RW_WIDTH
    wlr = jnp.zeros((LANE, 2 * W), F32).at[:64, :W].set(w2).at[64:, W:].set(a2)
    w0a0 = jnp.concatenate([w0, a0]).reshape(1, 2 * W)
    return w_cat, wlr, w0a0


def kernel(x, norm_mix, norm_ffn, norm_final, e_w_in, e_gla_wa2, e_gla_ba, e_gla_norm, e_s5_lambda_re, e_s5_lambda_im, e_s5_log_dt, e_s5_b_re, e_s5_b_im, e_s5_c_re, e_s5_c_im, e_s5_d, e_s5_w_glu, e_s5_b_glu, e_w_out, o_w_in, o_rw_mu, o_rw_w0, o_rw_w2, o_rw_a0, o_rw_a2, o_rw_g2, o_rw_k_k, o_rw_k_a, o_rw_r_k, o_rw_ln_w, o_rw_ln_b, o_mb_conv_w, o_mb_conv_b, o_mb_dt_bias, o_mb_a_log, o_mb_d, o_mb_norm, o_w_out, ffn_w_up, ffn_conv_w, ffn_conv_b, ffn_w_down):
    B, L, D = x.shape
    T = B * L
    depth = norm_mix.shape[0]
    x2 = x.reshape(T, D)
    row = lambda t: t.reshape(1, -1)
    for i in range(depth):
        j = i // 2
        if i % 2 == 0:
            w_cat, wa, bah, gnw, woa, wob = _pack_even(e_w_in[j], e_gla_wa2[j], e_gla_ba[j], e_gla_norm[j],
                                                       e_w_out[j])
            nkq = GLA_HEADS * GLA_DK
            nvp = GLA_HEADS * GLA_DVP
            q, k, v, g, u, alo = norm_proj(x2, row(norm_mix[i]), w_cat, (nkq, nkq, nvp, nvp, S5_WIDTH, LANE))
            ya = gla(q, k, v, g, alo, B, L, wa, bah, gnw)
            lr, li, br, bi, cr, ci = _pack_s5(e_s5_lambda_re[j], e_s5_lambda_im[j], e_s5_log_dt[j],
                                              e_s5_b_re[j], e_s5_b_im[j], e_s5_c_re[j], e_s5_c_im[j])
            yb = s5(u, B, L, lr, li, br, bi, cr, ci, row(e_s5_d[j]), e_s5_w_glu[j].astype(BF16),
                    row(e_s5_b_glu[j]))
            mix = (ya, yb, woa, wob)
        else:
            W = RW_WIDTH
            w_cat, wlr, w0a0 = _pack_odd(o_w_in[j], o_rw_w0[j], o_rw_w2[j], o_rw_a0[j], o_rw_a2[j])
            p_rw, p_z, p_xbc, p_dt = norm_proj(x2, row(norm_mix[i]), w_cat, (RW_COLS, MB_WIDTH, MB_XBC, LANE))
            head_of = jnp.arange(W) // RW_HEAD
            bd = (head_of[:, None] == head_of[None, :]).astype(BF16)
            r, a, v, lw, g, bon, kt, bt, lwt = rwkv_prep(
                p_rw, L, row(o_rw_mu[j]), wlr, w0a0, o_rw_g2[j], row(o_rw_k_k[j]), row(o_rw_k_a[j]),
                row(o_rw_r_k[j]), bd)
            yc = wkv(r, a, v, lw, g, bon, kt, bt, lwt, row(o_rw_ln_w[j]), row(o_rw_ln_b[j]), B, L)
            an = jnp.pad(-jnp.exp(o_mb_a_log[j]), (0, LANE - MB_HEADS)).reshape(1, LANE)
            dtb = jnp.pad(o_mb_dt_bias[j], (0, LANE - MB_HEADS)).reshape(1, LANE)
            ex = (jnp.arange(LANE)[:, None] == (jnp.arange(MB_WIDTH) // MB_HEADDIM)[None, :]).astype(BF16)
            dsk = jnp.repeat(o_mb_d[j], MB_HEADDIM).reshape(1, MB_WIDTH)
            yd = mamba(p_z, p_xbc, p_dt, B, L, o_mb_conv_w[j], row(o_mb_conv_b[j]), dtb, an, ex, dsk, row(o_mb_norm[j]))
            mix = (yc, yd, o_w_out[j][:W].astype(BF16), o_w_out[j][W:].astype(BF16))
        wup = ffn_w_up[i]
        x2 = ffn(x2, mix[0], mix[1], L, mix[2], mix[3], row(norm_ffn[i]), wup[:, :D_FF].astype(BF16),
                 wup[:, D_FF:].astype(BF16),
                 ffn_conv_w[i], row(ffn_conv_b[i]), ffn_w_down[i].astype(BF16), row(norm_final),
                 final_norm=(i == depth - 1))
    return x2.reshape(B, L, D)
```

```python
import functools
import math

import jax
import jax.numpy as jnp
from jax import lax
from jax.experimental import pallas as pl
from jax.experimental.pallas import tpu as pltpu

F32 = jnp.float32
BF16 = jnp.bfloat16
HI = lax.Precision.HIGHEST

D_MODEL = 1024
D_FF = 2816
EPS = 1e-6
LANE = 128
HALO = 8

GLA_HEADS = 4
GLA_DK = 128
GLA_DV = 192
GLA_DVP = 256
GLA_RANK = 16
GLA_TAU = 16.0
GLA_CHUNK = 128
S5_WIDTH = 256
S5_GROUPS = 16
S5_GROUP = 16
S5_STATE = 64
S5_NSTATE = S5_GROUPS * S5_STATE
S5_CHUNK = 512
RW_WIDTH = 512
RW_HEAD = 64
RW_COLS = 1792
RW_GN_EPS = 64e-5
RW_CHUNK = 128
MB_WIDTH = 512
MB_HEADS = 8
MB_HEADDIM = 64
MB_GROUPS = 2
MB_STATE = 128
MB_CONV = 4
MB_CHUNK = 128
MB_XBC = 1024
PROJ_COLS = 3456

VMEM_LIMIT = 56 * 1024 * 1024


def _cparams(*sem):
    return pltpu.CompilerParams(dimension_semantics=sem, vmem_limit_bytes=VMEM_LIMIT)


def _dot(a, b, precision=None):
    return jnp.dot(a, b, preferred_element_type=F32, precision=precision)


def _dot_nt(a, b, precision=None):
    return lax.dot_general(a, b, (((1,), (1,)), ((), ())), preferred_element_type=F32, precision=precision)


def _bdot(a, b):
    return _dot(a.astype(BF16), b.astype(BF16))


def _bdot_nt(a, b):
    return _dot_nt(a.astype(BF16), b.astype(BF16))


def _split(x):
    hi = x.astype(BF16)
    return hi, (x - hi.astype(F32)).astype(BF16)


def _dot_x2(x, e, pack=False):
    hi, lo = _split(x)
    if pack:
        return _dot(jnp.concatenate([hi, lo], axis=1), jnp.concatenate([e, e], axis=0))
    return _dot(hi, e) + _dot(lo, e)


def _dot_2x(e, x, pack=False):
    hi, lo = _split(x)
    if pack:
        return _dot(jnp.concatenate([e, e], axis=1), jnp.concatenate([hi, lo], axis=0))
    return _dot(e, hi) + _dot(e, lo)


def _dot_x3(a, b):
    ah, al = _split(a)
    bh, bl = _split(b)
    return _dot(ah, bh) + (_dot(al, bh) + _dot(ah, bl))


def _rms(x, w):
    return x * lax.rsqrt(jnp.mean(x * x, axis=-1, keepdims=True) + EPS) * w


def _sigmoid(x):
    return 1.0 / (1.0 + jnp.exp(-x))


def _silu(x):
    return x * _sigmoid(x)


def _softplus(x):
    return jnp.maximum(x, 0.0) + jnp.log(1.0 + jnp.exp(-jnp.abs(x)))


def _iota2(shape, dim):
    return lax.broadcasted_iota(jnp.int32, shape, dim)


def _const_spec(shape):
    nd = len(shape)
    return pl.BlockSpec(shape, lambda *_: (0,) * nd)


def _norm_proj_kernel(x_ref, nw_ref, w_ref, *o_refs):
    h = _rms(x_ref[...], nw_ref[...])
    y = _dot(h.astype(BF16), w_ref[...])
    off = 0
    for o_ref in o_refs:
        n = o_ref.shape[1]
        o_ref[...] = y[:, off:off + n]
        off += n


def norm_proj(x2, nw, w, splits, tm=512):
    T, D = x2.shape
    N = w.shape[1]
    assert sum(splits) == N
    return pl.pallas_call(
        _norm_proj_kernel,
        out_shape=[jax.ShapeDtypeStruct((T, n), F32) for n in splits],
        grid=(T // tm,),
        in_specs=[pl.BlockSpec((tm, D), lambda i: (i, 0)), _const_spec((1, D)), _const_spec((D, N))],
        out_specs=[pl.BlockSpec((tm, n), lambda i: (i, 0)) for n in splits],
        compiler_params=_cparams("parallel"),
        name="norm_proj",
    )(x2, nw, w)


def _ffn_kernel(x_ref, xh_ref, ya_ref, yah_ref, yb_ref, ybh_ref, wa_ref, wb_ref, nw_ref, wg_ref, wu_ref,
                cw_ref, cb_ref, wd_ref, fnw_ref, o_ref, *, tiles_per_seq, final_norm):
    tm = x_ref.shape[0]
    ya = jnp.concatenate([yah_ref[...], ya_ref[...]], axis=0).astype(BF16)
    yb = jnp.concatenate([ybh_ref[...], yb_ref[...]], axis=0).astype(BF16)
    xa = (jnp.concatenate([xh_ref[...], x_ref[...]], axis=0)
          + _dot(ya, wa_ref[...]) + _dot(yb, wb_ref[...]))
    x = xa[HALO:]
    h = _rms(xa, nw_ref[...]).astype(BF16)
    gate = _dot(h, wg_ref[...])
    halo_dead = jnp.where((pl.program_id(0) % tiles_per_seq) == 0, HALO, 0)
    gate = jnp.where(_iota2((HALO + tm, 1), 0) < halo_dead, 0.0, gate)
    cw = cw_ref[...]
    conv = (cw[2:3] * gate[HALO:]
            + cw[1:2] * pltpu.roll(gate, 1, 0)[HALO:]
            + cw[0:1] * pltpu.roll(gate, 2, 0)[HALO:]) + cb_ref[...]
    up = _dot(h[HALO:], wu_ref[...])
    act = (_silu(conv) * up).astype(BF16)
    y = x + _dot(act, wd_ref[...])
    if final_norm:
        y = _rms(y, fnw_ref[...])
    o_ref[...] = y


def ffn(x2, ya, yb, seq_len, wa, wb, nw, wg, wu, cw, cb, wd, fnw, final_norm, tm=256):
    T, D = x2.shape
    FF = wg.shape[1]
    ka, kb = ya.shape[1], yb.shape[1]
    hb = tm // HALO
    tile = lambda n: pl.BlockSpec((tm, n), lambda i: (i, 0))
    halo = lambda n: pl.BlockSpec((HALO, n), lambda i: (jnp.maximum(i * hb - 1, 0), 0))
    kern = functools.partial(_ffn_kernel, tiles_per_seq=seq_len // tm, final_norm=final_norm)
    return pl.pallas_call(
        kern,
        out_shape=jax.ShapeDtypeStruct((T, D), F32),
        grid=(T // tm,),
        in_specs=[tile(D), halo(D), tile(ka), halo(ka), tile(kb), halo(kb),
                  _const_spec((ka, D)), _const_spec((kb, D)),
                  _const_spec((1, D)), _const_spec((D, FF)), _const_spec((D, FF)),
                  _const_spec((3, FF)), _const_spec((1, FF)), _const_spec((FF, D)), _const_spec((1, D))],
        out_specs=tile(D),
        compiler_params=_cparams("parallel"),
        name="ffn",
    )(x2, x2, ya, ya, yb, yb, wa, wb, nw, wg, wu, cw, cb, wd, fnw)


def _gla_kernel(q_ref, k_ref, v_ref, g_ref, alo_ref, wa_ref, ba_ref, nw_ref, o_ref, st_ref):
    @pl.when(pl.program_id(2) == 0)
    def _():
        st_ref[...] = jnp.zeros_like(st_ref)

    C = GLA_CHUNK
    tq = q_ref.shape[0]
    nh = st_ref.shape[0]
    ri = _iota2((C, C), 0)
    ci = _iota2((C, C), 1)
    tril = (ri >= ci)
    tril_b = jnp.where(tril, 1.0, 0.0).astype(BF16)
    scale = GLA_DK ** -0.5
    alo = alo_ref[...].astype(BF16)
    sls = [slice(j * C, (j + 1) * C) for j in range(tq // C)]
    o_in, kdv, qd, dec = {}, {}, {}, {}
    for h in range(nh):
        kl = slice(h * GLA_DK, (h + 1) * GLA_DK)
        vl = slice(h * GLA_DVP, (h + 1) * GLA_DVP)
        xg = _dot(alo, wa_ref[h].astype(BF16)) + ba_ref[h]
        la = (jnp.minimum(xg, 0.0) - jnp.log(1.0 + jnp.exp(-jnp.abs(xg)))) * (1.0 / GLA_TAU)
        for j, sl in enumerate(sls):
            b = _dot_2x(tril_b, la[sl])
            b_mid = b[C // 2 - 1:C // 2]
            b_last = b[C - 1:C]
            q = q_ref[sl, kl] * scale
            k = k_ref[sl, kl]
            v = v_ref[sl, vl]
            qe = q * jnp.exp(b - b_mid)
            ke = k * jnp.exp(jnp.minimum(b_mid - b, 80.0))
            att = jnp.where(tril, _bdot_nt(qe, ke), 0.0)
            o_in[h, j] = _bdot(att, v)
            kdv[h, j] = _bdot(v.T, k * jnp.exp(b_last - b))
            qd[h, j] = (q * jnp.exp(b)).astype(BF16)
            dec[h, j] = jnp.exp(b_last)
    sts = {}
    for h in range(nh):
        st = st_ref[h]
        for j in range(len(sls)):
            sts[h, j] = st.astype(BF16)
            st = st * dec[h, j] + kdv[h, j]
        st_ref[h] = st
    for h in range(nh):
        vl = slice(h * GLA_DVP, (h + 1) * GLA_DVP)
        for j, sl in enumerate(sls):
            o = o_in[h, j] + _dot_nt(qd[h, j], sts[h, j])
            ms = jnp.sum(o * o, axis=-1, keepdims=True) * (1.0 / GLA_DV)
            o = o * lax.rsqrt(ms + EPS) * nw_ref[...]
            o_ref[sl, vl] = o * _silu(g_ref[sl, vl])


def gla(q, k, v, g, alo, B, L, wa, ba, nw, tq=512, nh=4):
    T = q.shape[0]
    nq = L // tq
    heads = lambda w: pl.BlockSpec((tq, nh * w), lambda b, h, c: (b * nq + c, h))
    return pl.pallas_call(
        _gla_kernel,
        out_shape=jax.ShapeDtypeStruct((T, GLA_HEADS * GLA_DVP), F32),
        grid=(B, GLA_HEADS // nh, nq),
        in_specs=[heads(GLA_DK), heads(GLA_DK), heads(GLA_DVP), heads(GLA_DVP),
                  pl.BlockSpec((tq, LANE), lambda b, h, c: (b * nq + c, 0)),
                  pl.BlockSpec((nh, LANE, GLA_DK), lambda b, h, c: (h, 0, 0)),
                  pl.BlockSpec((nh, 1, GLA_DK), lambda b, h, c: (h, 0, 0)),
                  _const_spec((1, GLA_DVP))],
        out_specs=heads(GLA_DVP),
        scratch_shapes=[pltpu.VMEM((nh, GLA_DVP, GLA_DK), F32)],
        compiler_params=_cparams("parallel", "parallel", "arbitrary"),
        name="gla",
    )(q, k, v, g, alo, wa, ba, nw)


def _cmul(ar, ai, br, bi):
    return ar * br - ai * bi, ar * bi + ai * br


def _s5_kernel(u_ref, perm_ref, unperm_ref, lr_ref, li_ref, br_ref, bi_ref, cr_ref, ci_ref, d_ref, wg_ref,
               bg_ref, o_ref, sr_ref, si_ref, pr_ref, pi_ref, xr_ref, xi_ref):
    tc = u_ref.shape[0]
    seg = tc // HALO
    ns = lr_ref.shape[1]
    lr = lr_ref[...]
    li = li_ref[...]
    lr8 = jnp.broadcast_to(lr, (HALO, ns))
    li8 = jnp.broadcast_to(li, (HALO, ns))
    rows = lambda j: slice(j * HALO, (j + 1) * HALO)

    @pl.when(pl.program_id(1) == 0)
    def _():
        sr_ref[...] = jnp.zeros_like(sr_ref)
        si_ref[...] = jnp.zeros_like(si_ref)
        zr, zi = lr8, li8
        for j in range(seg):
            pr_ref[rows(j), :] = zr
            pi_ref[rows(j), :] = zi
            zr, zi = _cmul(lr8, li8, zr, zi)

    u = u_ref[...]
    ub = _dot(perm_ref[...], u.astype(BF16)).astype(BF16)
    xr_ref[...] = _dot(ub, br_ref[...])
    xi_ref[...] = _dot(ub, bi_ref[...])
    er = xr_ref[rows(0), :]
    ei = xi_ref[rows(0), :]
    for j in range(1, seg):
        tr, ti = _cmul(lr8, li8, er, ei)
        er = xr_ref[rows(j), :] + tr
        ei = xi_ref[rows(j), :] + ti
        xr_ref[rows(j), :] = er
        xi_ref[rows(j), :] = ei
    mr, mi = lr, li
    for _ in range(seg.bit_length() - 1):
        mr, mi = _cmul(mr, mi, mr, mi)
    row8 = _iota2((HALO, 1), 0)
    cr0, ci0 = _cmul(mr, mi, sr_ref[...], si_ref[...])
    er = er + jnp.where(row8 == 0, cr0, 0.0)
    ei = ei + jnp.where(row8 == 0, ci0, 0.0)
    sh = 1
    while sh < HALO:
        yr, yi = _cmul(mr, mi, jnp.where(row8 < sh, 0.0, pltpu.roll(er, sh, 0)),
                       jnp.where(row8 < sh, 0.0, pltpu.roll(ei, sh, 0)))
        er, ei = er + yr, ei + yi
        mr, mi = _cmul(mr, mi, mr, mi)
        sh *= 2
    inr = jnp.where(row8 == 0, sr_ref[...], pltpu.roll(er, 1, 0))
    ini = jnp.where(row8 == 0, si_ref[...], pltpu.roll(ei, 1, 0))
    sr_ref[...] = er[HALO - 1:]
    si_ref[...] = ei[HALO - 1:]
    ar, ai = _cmul(pr_ref[...].reshape(seg, HALO, ns), pi_ref[...].reshape(seg, HALO, ns), inr[None], ini[None])
    xr = xr_ref[...] + ar.reshape(tc, ns)
    xi = xi_ref[...] + ai.reshape(tc, ns)
    yp = _bdot(xr, cr_ref[...]) - _bdot(xi, ci_ref[...])
    y = _dot_2x(unperm_ref[...], yp) + d_ref[...] * u
    z = 0.5 * y * (1.0 + jnp.tanh(math.sqrt(2.0 / math.pi) * (y + 0.044715 * (y * y * y))))
    o_ref[...] = z * _sigmoid(_bdot(z, wg_ref[...]) + bg_ref[...])


def s5(u, B, L, lr, li, br, bi, cr, ci, d, wg, bg, tc=S5_CHUNK):
    T = u.shape[0]
    nc = L // tc
    NS = S5_NSTATE
    seg = tc // HALO
    rho = jnp.arange(tc)
    perm = (jnp.arange(tc)[None, :] == (seg * (rho % HALO) + rho // HALO)[:, None]).astype(BF16)
    big = pltpu.VMEM((tc, NS), F32)
    return pl.pallas_call(
        _s5_kernel,
        out_shape=jax.ShapeDtypeStruct((T, S5_WIDTH), F32),
        grid=(B, nc),
        in_specs=[pl.BlockSpec((tc, S5_WIDTH), lambda b, c: (b * nc + c, 0)),
                  _const_spec((tc, tc)), _const_spec((tc, tc)),
                  _const_spec((1, NS)), _const_spec((1, NS)),
                  _const_spec((S5_WIDTH, NS)), _const_spec((S5_WIDTH, NS)),
                  _const_spec((NS, S5_WIDTH)), _const_spec((NS, S5_WIDTH)),
                  _const_spec((1, S5_WIDTH)), _const_spec((S5_WIDTH, S5_WIDTH)), _const_spec((1, S5_WIDTH))],
        out_specs=pl.BlockSpec((tc, S5_WIDTH), lambda b, c: (b * nc + c, 0)),
        scratch_shapes=[pltpu.VMEM((1, NS), F32), pltpu.VMEM((1, NS), F32), big, big, big, big],
        compiler_params=_cparams("parallel", "arbitrary"),
        name="s5",
    )(u, perm, perm.T, lr, li, br, bi, cr, ci, d, wg, bg)


def _rwkv_prep_kernel(p_ref, ph_ref, mu_ref, wlr_ref, w0a0_ref, g2_ref, kk_ref, ka_ref, rk_ref, bd_ref,
                      r_ref, a_ref, v_ref, lw_ref, g_ref, bon_ref, kt_ref, bt_ref, lwt_ref, *, tiles_per_seq):
    tm = p_ref.shape[0]
    W = RW_WIDTH
    p = p_ref[...]
    pa = jnp.concatenate([ph_ref[...], p], axis=0)
    prev = pltpu.roll(pa, 1, 0)[HALO:]
    first_dead = jnp.where((pl.program_id(0) % tiles_per_seq) == 0, 1, 0)
    prev = jnp.where(_iota2((tm, 1), 0) < first_dead, 0.0, prev)
    pm = p + (prev - p) * mu_ref[...]
    r = pm[:, :W]
    k = pm[:, W:2 * W]
    v = pm[:, 2 * W:3 * W]
    xwa = pm[:, 3 * W:3 * W + LANE]
    xg = pm[:, 3 * W + LANE:]
    xwa = jnp.where(_iota2((tm, LANE), 1) < 64, jnp.tanh(xwa), xwa)
    wa = _dot_x3(xwa, wlr_ref[...]) + w0a0_ref[...]
    wlog = -_softplus(-wa[:, :W]) - 0.5
    lw = -jnp.exp(wlog)
    a = _sigmoid(wa[:, W:])
    g = _bdot(_sigmoid(xg), g2_ref[...])
    bd = bd_ref[...]
    kk = k * kk_ref[...]
    kk = kk / jnp.maximum(jnp.sqrt(_dot_x2(kk * kk, bd)), 1e-12)
    k2 = k * (1.0 + (a - 1.0) * ka_ref[...])
    r_ref[...] = r
    a_ref[...] = -kk
    v_ref[...] = v
    lw_ref[...] = lw
    g_ref[...] = g.astype(BF16)
    bon_ref[...] = (_dot_x2(r * k2 * rk_ref[...], bd) * v).astype(BF16)
    kt_ref[...] = k2.T
    bt_ref[...] = (kk * a).T
    lwt_ref[...] = lw.T


def rwkv_prep(p, L, mu, wlr, w0a0, g2, kk, ka, rk, bd, tm=512):
    T = p.shape[0]
    W = RW_WIDTH
    hb = tm // HALO
    tok = jax.ShapeDtypeStruct((T, W), F32)
    chn = jax.ShapeDtypeStruct((W, T), F32)
    tok_spec = pl.BlockSpec((tm, W), lambda i: (i, 0))
    chn_spec = pl.BlockSpec((W, tm), lambda i: (0, i))
    kern = functools.partial(_rwkv_prep_kernel, tiles_per_seq=L // tm)
    return pl.pallas_call(
        kern,
        out_shape=[tok] * 4 + [jax.ShapeDtypeStruct((T, W), BF16)] * 2 + [chn] * 3,
        grid=(T // tm,),
        in_specs=[pl.BlockSpec((tm, RW_COLS), lambda i: (i, 0)),
                  pl.BlockSpec((HALO, RW_COLS), lambda i: (jnp.maximum(i * hb - 1, 0), 0)),
                  _const_spec((1, RW_COLS)), _const_spec((LANE, 2 * W)), _const_spec((1, 2 * W)),
                  _const_spec((LANE, W)), _const_spec((1, W)), _const_spec((1, W)), _const_spec((1, W)),
                  _const_spec((W, W))],
        out_specs=[tok_spec] * 6 + [chn_spec] * 3,
        compiler_params=_cparams("parallel"),
        name="rwkv_prep",
    )(p, p, mu, wlr, w0a0, g2, kk, ka, rk, bd)


def _wkv_local_kernel(r_ref, a_ref, v_ref, lw_ref, kt_ref, bt_ref, lwt_ref, wr_ref, y0_ref, m_ref, hl_ref):
    C = RW_CHUNK
    nch = r_ref.shape[0] // C
    ri = _iota2((C, C), 0)
    ci = _iota2((C, C), 1)
    incl = ri >= ci
    strict = ri > ci
    tril_b = jnp.where(incl, 1.0, 0.0).astype(BF16)
    triu_b = jnp.where(ri <= ci, 1.0, 0.0).astype(BF16)
    m0 = ci < RW_HEAD
    bdm = (ri // RW_HEAD) == (ci // RW_HEAD)
    swap = lambda t: pltpu.roll(t, RW_HEAD, 1)
    sls = [slice(j * C, (j + 1) * C) for j in range(nch)]
    chunks = range(nch)
    pairs = [(j, h) for j in chunks for h in range(2)]
    r = [r_ref[sl, :] for sl in sls]
    a = [a_ref[sl, :] for sl in sls]
    lw = [lw_ref[sl, :] for sl in sls]
    kt = [kt_ref[:, sl] for sl in sls]
    bt = [bt_ref[:, sl] for sl in sls]
    v = [v_ref[sl, :] for sl in sls]
    vb = [t.astype(BF16) for t in v]
    c = [_dot_2x(tril_b, lw[j], pack=True) for j in chunks]
    ct = [_dot_x2(lwt_ref[:, sls[j]], triu_b, pack=True) for j in chunks]
    aa = []
    for j in chunks:
        c_mid = c[j][C // 2 - 1:C // 2]
        ct_mid = ct[j][:, C // 2 - 1:C // 2]
        at = a[j] * jnp.exp(c[j] - lw[j] - c_mid)
        rt = r[j] * jnp.exp(c[j] - c_mid)
        e_mid = jnp.exp(ct_mid - ct[j])
        lhs = jnp.concatenate([jnp.where(m0, at, 0.0), jnp.where(m0, 0.0, at),
                               jnp.where(m0, rt, 0.0), jnp.where(m0, 0.0, rt)], axis=0)
        rhs = jnp.concatenate([bt[j] * e_mid, kt[j] * e_mid], axis=1)
        aa.append(_bdot(lhs, rhs))
    n = {(j, h): jnp.where(strict, aa[j][h * C:(h + 1) * C, :C], 0.0) for j, h in pairs}
    ak = {(j, h): jnp.where(strict, aa[j][h * C:(h + 1) * C, C:], 0.0).astype(BF16) for j, h in pairs}
    rb = {(j, h): jnp.where(incl, aa[j][(2 + h) * C:(3 + h) * C, :C], 0.0).astype(BF16) for j, h in pairs}
    rk = {(j, h): jnp.where(incl, aa[j][(2 + h) * C:(3 + h) * C, C:], 0.0).astype(BF16) for j, h in pairs}
    vh = {}
    x = {}
    for j in chunks:
        a_abs = a[j] * jnp.exp(c[j] - lw[j])
        vh[j, 0] = jnp.where(m0, 0.0, swap(v[j])).astype(BF16)
        vh[j, 1] = jnp.where(m0, 0.0, v[j]).astype(BF16)
        x[j, 0] = jnp.where(m0, a_abs, _dot(ak[j, 0], vh[j, 0]))
        x[j, 1] = jnp.where(m0, swap(a_abs), _dot(ak[j, 1], vh[j, 1]))
    sh = 1
    while sh < C:
        top = sh if sh % (2 * HALO) == 0 else 0
        nb = {p: n[p].astype(BF16) for p in pairs}
        if 2 * sh < C:
            res = {p: _dot(nb[p][top:], jnp.concatenate([x[p].astype(BF16), nb[p]], axis=1)) for p in pairs}
            n = {p: res[p][:, C:] for p in pairs}
        else:
            res = {p: _dot(nb[p][top:], x[p].astype(BF16)) for p in pairs}
        if top:
            x = {p: jnp.concatenate([x[p][:top], x[p][top:] + res[p][:, :C]], axis=0) for p in pairs}
            if 2 * sh < C:
                n = {p: jnp.concatenate([jnp.zeros((top, C), F32), n[p]], axis=0) for p in pairs}
        else:
            x = {p: x[p] + res[p][:, :C] for p in pairs}
        sh *= 2
    g = {p: _dot(jnp.concatenate([rb[p], rk[p]], axis=1), jnp.concatenate([x[p].astype(BF16), vh[p]], axis=0))
         for p in pairs}
    for j in chunks:
        wa = jnp.where(m0, x[j, 0], swap(x[j, 1]))
        u0 = jnp.where(m0, swap(x[j, 0]), x[j, 1])
        wr_ref[sls[j], :] = (r[j] * jnp.exp(c[j]) + jnp.where(m0, g[j, 0], swap(g[j, 1]))).astype(BF16)
        y0_ref[sls[j], :] = jnp.where(m0, swap(g[j, 0]), g[j, 1]).astype(BF16)
        ct_last = ct[j][:, C - 1:C]
        e_last = jnp.exp(ct_last - ct[j])
        bh = (bt[j] * e_last).astype(BF16)
        kh = (kt[j] * e_last).astype(BF16)
        rhs = jnp.concatenate([jnp.concatenate([wa.astype(BF16), u0.astype(BF16)], axis=1),
                               jnp.concatenate([jnp.zeros((C, C), BF16), vb[j]], axis=1)], axis=0)
        mh = _dot(jnp.concatenate([bh, kh], axis=1), rhs)
        m_ref[j, 0] = jnp.where(bdm, mh[:, :C], 0.0) + jnp.where(ri == ci, jnp.exp(ct_last), 0.0)
        hl_ref[j, 0] = jnp.where(bdm, mh[:, C:], 0.0)


def wkv_local(r, a, v, lw, kt, bt, lwt, tq=1024):
    T = r.shape[0]
    C = RW_CHUNK
    npair = RW_WIDTH // LANE
    tok = pl.BlockSpec((tq, LANE), lambda i, h: (i, h))
    chn = pl.BlockSpec((LANE, tq), lambda i, h: (h, i))
    mat = pl.BlockSpec((tq // C, 1, LANE, LANE), lambda i, h: (i, h, 0, 0))
    mat_shape = jax.ShapeDtypeStruct((T // C, npair, LANE, LANE), F32)
    return pl.pallas_call(
        _wkv_local_kernel,
        out_shape=[jax.ShapeDtypeStruct((T, RW_WIDTH), BF16)] * 2 + [mat_shape, mat_shape],
        grid=(T // tq, npair),
        in_specs=[tok] * 4 + [chn] * 3,
        out_specs=[tok, tok, mat, mat],
        compiler_params=_cparams("parallel", "parallel"),
        name="wkv_local",
    )(r, a, v, lw, kt, bt, lwt)


def _wkv_scan_kernel(wr_ref, y0_ref, m_ref, hl_ref, g_ref, bon_ref, lnw_ref, lnb_ref, o_ref, h_ref):
    @pl.when(pl.program_id(1) == 0)
    def _():
        h_ref[...] = jnp.zeros_like(h_ref)

    C = RW_CHUNK
    nch = wr_ref.shape[0] // C
    npair = h_ref.shape[0]
    bdm = (_iota2((C, C), 0) // RW_HEAD) == (_iota2((C, C), 1) // RW_HEAD)
    gn = jnp.where(bdm, 1.0 / RW_HEAD, 0.0).astype(BF16)
    h = [h_ref[p] for p in range(npair)]
    for j in range(nch):
        sl = slice(j * C, (j + 1) * C)
        ys = []
        for p in range(npair):
            ln = slice(p * LANE, (p + 1) * LANE)
            ys.append(_dot(wr_ref[sl, ln], h[p].astype(BF16)) + y0_ref[sl, ln])
        h = [_dot_x3(m_ref[j, p], h[p]) + hl_ref[j, p] for p in range(npair)]
        for p in range(npair):
            ln = slice(p * LANE, (p + 1) * LANE)
            mean = _dot_x2(ys[p], gn, pack=True)
            yc = ys[p] - mean
            var = _dot_x2(yc * yc, gn, pack=True)
            yn = yc * lax.rsqrt(var + RW_GN_EPS) * lnw_ref[:, ln] + lnb_ref[:, ln]
            o_ref[sl, ln] = (yn + bon_ref[sl, ln]) * g_ref[sl, ln]
    for p in range(npair):
        h_ref[p] = h[p]


def wkv_scan(wr, y0, m, hl, g, bon, lnw, lnb, B, L, tq=512):
    T, W = wr.shape
    C = RW_CHUNK
    nq = L // tq
    npair = W // LANE
    tok = pl.BlockSpec((tq, W), lambda b, c: (b * nq + c, 0))
    mat = pl.BlockSpec((tq // C, npair, LANE, LANE), lambda b, c: (b * nq + c, 0, 0, 0))
    return pl.pallas_call(
        _wkv_scan_kernel,
        out_shape=jax.ShapeDtypeStruct((T, W), F32),
        grid=(B, nq),
        in_specs=[tok, tok, mat, mat, tok, tok, _const_spec((1, W)), _const_spec((1, W))],
        out_specs=tok,
        scratch_shapes=[pltpu.VMEM((npair, LANE, LANE), F32)],
        compiler_params=_cparams("parallel", "arbitrary"),
        name="wkv_scan",
    )(wr, y0, m, hl, g, bon, lnw, lnb)


def wkv(r, a, v, lw, g, bon, kt, bt, lwt, lnw, lnb, B, L):
    wr, y0, m, hl = wkv_local(r, a, v, lw, kt, bt, lwt)
    return wkv_scan(wr, y0, m, hl, g, bon, lnw, lnb, B, L)


def _mamba_kernel(z_ref, xbc_ref, xh_ref, dt_ref, cw_ref, cb_ref, dtb_ref, an_ref, ex_ref, dsk_ref, nw_ref,
                  o_ref, st_ref):
    @pl.when(pl.program_id(1) == 0)
    def _():
        st_ref[...] = jnp.zeros_like(st_ref)

    C = MB_CHUNK
    W = MB_WIDTH
    N = MB_STATE
    tq = z_ref.shape[0]
    hpg = MB_HEADS // MB_GROUPS
    gw = W // MB_GROUPS
    xbc = xbc_ref[...]
    xa = jnp.concatenate([xh_ref[...], xbc], axis=0)
    halo_dead = jnp.where(pl.program_id(1) == 0, HALO, 0)
    xa = jnp.where(_iota2((HALO + tq, 1), 0) < halo_dead, 0.0, xa)
    cw = cw_ref[...]
    conv = cw[3:4] * xbc + cb_ref[...]
    for kk in range(1, MB_CONV):
        conv = conv + cw[3 - kk:4 - kk] * pltpu.roll(xa, kk, 0)[HALO:]
    xbc = _silu(conv)
    xs = xbc[:, :W]
    dt = _softplus(dt_ref[...] + dtb_ref[...])
    a = dt * an_ref[...]
    ex = ex_ref[...]
    xdt = xs * _dot_x2(dt, ex, pack=True)
    xdt_b = xdt.astype(BF16)
    ri = _iota2((C, C), 0)
    ci = _iota2((C, C), 1)
    causal = ri >= ci
    tril_b = jnp.where(causal, 1.0, 0.0).astype(BF16)
    head_g = _iota2((C, gw), 1) // MB_HEADDIM
    sls = [slice(j * C, (j + 1) * C) for j in range(tq // C)]
    y_loc, upd, cdec, acs_xs, cms = [], [], [], [], []
    for sl in sls:
        acs = _dot_2x(tril_b, a[sl], pack=True)
        acs_t = acs.T
        acs_x = _dot_x2(acs, ex, pack=True)
        ys, cm_g = [], []
        for gi in range(MB_GROUPS):
            bm = xbc[sl, W + gi * N:W + (gi + 1) * N]
            cm = xbc[sl, W + MB_GROUPS * N + gi * N:W + MB_GROUPS * N + (gi + 1) * N].astype(BF16)
            cb = _dot_nt(cm, bm.astype(BF16))
            lm = []
            for e in range(hpg):
                hd = gi * hpg + e
                lmat = jnp.exp(jnp.where(causal, acs[:, hd:hd + 1] - acs_t[hd:hd + 1, :], -jnp.inf))
                lm.append((cb * lmat).astype(BF16))
            res = _dot(jnp.concatenate(lm, axis=0), xdt_b[sl, gi * gw:(gi + 1) * gw])
            yg = res[:C]
            for e in range(1, hpg):
                yg = jnp.where(head_g == e, res[e * C:(e + 1) * C], yg)
            ys.append(yg)
            cm_g.append(cm)
        y_loc.append(ys)
        cms.append(cm_g)
        acs_xs.append(acs_x)
        acs_last = acs_x[C - 1:C]
        xw_t = (xdt[sl] * jnp.exp(acs_last - acs_x)).T
        upd.append(jnp.concatenate(
            [_bdot(xw_t[gi * gw:(gi + 1) * gw], xbc[sl, W + gi * N:W + (gi + 1) * N]) for gi in range(MB_GROUPS)],
            axis=0))
        cdec.append(jnp.exp(acs_x.T[:, C - 1:C]))
    st = st_ref[...]
    sts = []
    for j in range(len(sls)):
        sts.append(st.astype(BF16))
        st = cdec[j] * st + upd[j]
    st_ref[...] = st
    zs = _silu(z_ref[...])
    for j, sl in enumerate(sls):
        e_x = jnp.exp(acs_xs[j])
        parts = []
        for gi in range(MB_GROUPS):
            gl = slice(gi * gw, (gi + 1) * gw)
            off = _dot_nt(cms[j][gi], sts[j][gl])
            yg = (y_loc[j][gi] + off * e_x[:, gl] + xs[sl, gl] * dsk_ref[:, gl]) * zs[sl, gl]
            parts.append(yg * lax.rsqrt(jnp.mean(yg * yg, axis=-1, keepdims=True) + EPS))
        o_ref[sl, :] = jnp.concatenate(parts, axis=1) * nw_ref[...]


def mamba(z, xbc, dt, B, L, cw, cb, dtb, an, ex, dsk, nw, tq=512):
    T = z.shape[0]
    nc = L // tq
    hb = tq // HALO
    return pl.pallas_call(
        _mamba_kernel,
        out_shape=jax.ShapeDtypeStruct((T, MB_WIDTH), F32),
        grid=(B, nc),
        in_specs=[pl.BlockSpec((tq, MB_WIDTH), lambda b, c: (b * nc + c, 0)),
                  pl.BlockSpec((tq, MB_XBC), lambda b, c: (b * nc + c, 0)),
                  pl.BlockSpec((HALO, MB_XBC), lambda b, c: (jnp.maximum((b * nc + c) * hb - 1, 0), 0)),
                  pl.BlockSpec((tq, LANE), lambda b, c: (b * nc + c, 0)),
                  _const_spec((MB_CONV, MB_XBC)), _const_spec((1, MB_XBC)), _const_spec((1, LANE)),
                  _const_spec((1, LANE)), _const_spec((LANE, MB_WIDTH)), _const_spec((1, MB_WIDTH)),
                  _const_spec((1, MB_WIDTH))],
        out_specs=pl.BlockSpec((tq, MB_WIDTH), lambda b, c: (b * nc + c, 0)),
        scratch_shapes=[pltpu.VMEM((MB_WIDTH, MB_STATE), F32)],
        compiler_params=_cparams("parallel", "arbitrary"),
        name="mamba2",
    )(z, xbc, xbc, dt, cw, cb, dtb, an, ex, dsk, nw)


def _pack_even(w_in, wa2, ba, gnorm, w_out):
    nk = GLA_HEADS * GLA_DK
    nv = GLA_HEADS * GLA_DV
    pad_v = lambda w: jnp.pad(w.reshape(D_MODEL, GLA_HEADS, GLA_DV),
                              ((0, 0), (0, 0), (0, GLA_DVP - GLA_DV))).reshape(D_MODEL, GLA_HEADS * GLA_DVP)
    wq = w_in[:, :2 * nk]
    wv = pad_v(w_in[:, 2 * nk:2 * nk + nv])
    wg = pad_v(w_in[:, 2 * nk + nv:2 * nk + 2 * nv])
    walo = jnp.pad(w_in[:, 2 * nk + 2 * nv:2 * nk + 2 * nv + GLA_RANK], ((0, 0), (0, LANE - GLA_RANK)))
    wu = w_in[:, 2 * nk + 2 * nv + GLA_RANK:]
    w_cat = jnp.concatenate([wq, wv, wg, wu, walo], axis=1).astype(BF16)
    wa = jnp.pad(wa2.reshape(GLA_RANK, GLA_HEADS, GLA_DK).transpose(1, 0, 2),
                 ((0, 0), (0, LANE - GLA_RANK), (0, 0)))
    bah = ba.reshape(GLA_HEADS, 1, GLA_DK)
    nw = jnp.pad(gnorm, (0, GLA_DVP - GLA_DV)).reshape(1, GLA_DVP)
    woa = jnp.pad(w_out[:nv].reshape(GLA_HEADS, GLA_DV, D_MODEL),
                  ((0, 0), (0, GLA_DVP - GLA_DV), (0, 0))).reshape(GLA_HEADS * GLA_DVP, D_MODEL).astype(BF16)
    wob = w_out[nv:].astype(BF16)
    return w_cat, wa, bah, nw, woa, wob


def _pack_s5(lam_re, lam_im, log_dt, b_re, b_im, c_re, c_im):
    dt = jnp.exp(log_dt)[:, None]
    mag = jnp.exp(lam_re * dt)
    ang = lam_im * dt
    lb_re = mag * jnp.cos(ang)
    lb_im = mag * jnp.sin(ang)
    den = lam_re * lam_re + lam_im * lam_im
    nr = lb_re - 1.0
    f_re = (nr * lam_re + lb_im * lam_im) / den
    f_im = (lb_im * lam_re - nr * lam_im) / den
    bb_re = f_re[..., None] * b_re - f_im[..., None] * b_im
    bb_im = f_re[..., None] * b_im + f_im[..., None] * b_re
    eye = jnp.eye(S5_GROUPS, dtype=F32)
    bd_in = lambda m: jnp.einsum('gpc,gh->gchp', m, eye).reshape(S5_WIDTH, S5_NSTATE).astype(BF16)
    bd_out = lambda m: jnp.einsum('gcp,gh->gphc', m, eye).reshape(S5_NSTATE, S5_WIDTH).astype(BF16)
    return (lb_re.reshape(1, S5_NSTATE), lb_im.reshape(1, S5_NSTATE),
            bd_in(bb_re), bd_in(bb_im), bd_out(c_re), bd_out(c_im))


def _pack_odd(w_in, w0, w2, a0, a2):
    w_cat = jnp.pad(w_in, ((0, 0), (0, PROJ_COLS - w_in.shape[1]))).astype(BF16)
    W = RW_WIDTH
    wlr = jnp.zeros((LANE, 2 * W), F32).at[:64, :W].set(w2).at[64:, W:].set(a2)
    w0a0 = jnp.concatenate([w0, a0]).reshape(1, 2 * W)
    return w_cat, wlr, w0a0


def kernel(x, norm_mix, norm_ffn, norm_final, e_w_in, e_gla_wa2, e_gla_ba, e_gla_norm, e_s5_lambda_re, e_s5_lambda_im, e_s5_log_dt, e_s5_b_re, e_s5_b_im, e_s5_c_re, e_s5_c_im, e_s5_d, e_s5_w_glu, e_s5_b_glu, e_w_out, o_w_in, o_rw_mu, o_rw_w0, o_rw_w2, o_rw_a0, o_rw_a2, o_rw_g2, o_rw_k_k, o_rw_k_a, o_rw_r_k, o_rw_ln_w, o_rw_ln_b, o_mb_conv_w, o_mb_conv_b, o_mb_dt_bias, o_mb_a_log, o_mb_d, o_mb_norm, o_w_out, ffn_w_up, ffn_conv_w, ffn_conv_b, ffn_w_down):
    B, L, D = x.shape
    T = B * L
    depth = norm_mix.shape[0]
    x2 = x.reshape(T, D)
    row = lambda t: t.reshape(1, -1)
    for i in range(depth):
        j = i // 2
        if i % 2 == 0:
            w_cat, wa, bah, gnw, woa, wob = _pack_even(e_w_in[j], e_gla_wa2[j], e_gla_ba[j], e_gla_norm[j],
                                                       e_w_out[j])
            nkq = GLA_HEADS * GLA_DK
            nvp = GLA_HEADS * GLA_DVP
            q, k, v, g, u, alo = norm_proj(x2, row(norm_mix[i]), w_cat, (nkq, nkq, nvp, nvp, S5_WIDTH, LANE))
            ya = gla(q, k, v, g, alo, B, L, wa, bah, gnw)
            lr, li, br, bi, cr, ci = _pack_s5(e_s5_lambda_re[j], e_s5_lambda_im[j], e_s5_log_dt[j],
                                              e_s5_b_re[j], e_s5_b_im[j], e_s5_c_re[j], e_s5_c_im[j])
            yb = s5(u, B, L, lr, li, br, bi, cr, ci, row(e_s5_d[j]), e_s5_w_glu[j].astype(BF16),
                    row(e_s5_b_glu[j]))
            mix = (ya, yb, woa, wob)
        else:
            W = RW_WIDTH
            w_cat, wlr, w0a0 = _pack_odd(o_w_in[j], o_rw_w0[j], o_rw_w2[j], o_rw_a0[j], o_rw_a2[j])
            p_rw, p_z, p_xbc, p_dt = norm_proj(x2, row(norm_mix[i]), w_cat, (RW_COLS, MB_WIDTH, MB_XBC, LANE))
            head_of = jnp.arange(W) // RW_HEAD
            bd = (head_of[:, None] == head_of[None, :]).astype(BF16)
            r, a, v, lw, g, bon, kt, bt, lwt = rwkv_prep(
                p_rw, L, row(o_rw_mu[j]), wlr, w0a0, o_rw_g2[j], row(o_rw_k_k[j]), row(o_rw_k_a[j]),
                row(o_rw_r_k[j]), bd)
            yc = wkv(r, a, v, lw, g, bon, kt, bt, lwt, row(o_rw_ln_w[j]), row(o_rw_ln_b[j]), B, L)
            an = jnp.pad(-jnp.exp(o_mb_a_log[j]), (0, LANE - MB_HEADS)).reshape(1, LANE)
            dtb = jnp.pad(o_mb_dt_bias[j], (0, LANE - MB_HEADS)).reshape(1, LANE)
            ex = (jnp.arange(LANE)[:, None] == (jnp.arange(MB_WIDTH) // MB_HEADDIM)[None, :]).astype(BF16)
            dsk = jnp.repeat(o_mb_d[j], MB_HEADDIM).reshape(1, MB_WIDTH)
            yd = mamba(p_z, p_xbc, p_dt, B, L, o_mb_conv_w[j], row(o_mb_conv_b[j]), dtb, an, ex, dsk, row(o_mb_norm[j]))
            mix = (yc, yd, o_w_out[j][:W].astype(BF16), o_w_out[j][W:].astype(BF16))
        wup = ffn_w_up[i]
        x2 = ffn(x2, mix[0], mix[1], L, mix[2], mix[3], row(norm_ffn[i]), wup[:, :D_FF].astype(BF16),
                 wup[:, D_FF:].astype(BF16),
                 ffn_conv_w[i], row(ffn_conv_b[i]), ffn_w_down[i].astype(BF16), row(norm_final),
                 final_norm=(i == depth - 1))
    return x2.reshape(B, L, D)
```

```python
import functools
import math

import jax
import jax.numpy as jnp
from jax import lax
from jax.experimental import pallas as pl
from jax.experimental.pallas import tpu as pltpu

F32 = jnp.float32
BF16 = jnp.bfloat16
HI = lax.Precision.HIGHEST

D_MODEL = 1024
D_FF = 2816
EPS = 1e-6
LANE = 128
HALO = 8

GLA_HEADS = 4
GLA_DK = 128
GLA_DV = 192
GLA_DVP = 256
GLA_RANK = 16
GLA_TAU = 16.0
GLA_CHUNK = 128
S5_WIDTH = 256
S5_GROUPS = 16
S5_GROUP = 16
S5_STATE = 64
S5_NSTATE = S5_GROUPS * S5_STATE
S5_CHUNK = 512
RW_WIDTH = 512
RW_HEAD = 64
RW_COLS = 1792
RW_GN_EPS = 64e-5
RW_CHUNK = 128
MB_WIDTH = 512
MB_HEADS = 8
MB_HEADDIM = 64
MB_GROUPS = 2
MB_STATE = 128
MB_CONV = 4
MB_CHUNK = 128
MB_XBC = 1024
PROJ_COLS = 3456

VMEM_LIMIT = 56 * 1024 * 1024


def _cparams(*sem):
    return pltpu.CompilerParams(dimension_semantics=sem, vmem_limit_bytes=VMEM_LIMIT)


def _dot(a, b, precision=None):
    return jnp.dot(a, b, preferred_element_type=F32, precision=precision)


def _dot_nt(a, b, precision=None):
    return lax.dot_general(a, b, (((1,), (1,)), ((), ())), preferred_element_type=F32, precision=precision)


def _bdot(a, b):
    return _dot(a.astype(BF16), b.astype(BF16))


def _bdot_nt(a, b):
    return _dot_nt(a.astype(BF16), b.astype(BF16))


def _split(x):
    hi = x.astype(BF16)
    return hi, (x - hi.astype(F32)).astype(BF16)


def _dot_x2(x, e, pack=False):
    hi, lo = _split(x)
    if pack:
        return _dot(jnp.concatenate([hi, lo], axis=1), jnp.concatenate([e, e], axis=0))
    return _dot(hi, e) + _dot(lo, e)


def _dot_2x(e, x, pack=False):
    hi, lo = _split(x)
    if pack:
        return _dot(jnp.concatenate([e, e], axis=1), jnp.concatenate([hi, lo], axis=0))
    return _dot(e, hi) + _dot(e, lo)


def _dot_x3(a, b):
    ah, al = _split(a)
    bh, bl = _split(b)
    return _dot(ah, bh) + (_dot(al, bh) + _dot(ah, bl))


def _rms(x, w):
    return x * lax.rsqrt(jnp.mean(x * x, axis=-1, keepdims=True) + EPS) * w


def _sigmoid(x):
    return 1.0 / (1.0 + jnp.exp(-x))


def _silu(x):
    return x * _sigmoid(x)


def _softplus(x):
    return jnp.maximum(x, 0.0) + jnp.log(1.0 + jnp.exp(-jnp.abs(x)))


def _iota2(shape, dim):
    return lax.broadcasted_iota(jnp.int32, shape, dim)


def _const_spec(shape):
    nd = len(shape)
    return pl.BlockSpec(shape, lambda *_: (0,) * nd)


def _norm_proj_kernel(x_ref, nw_ref, w_ref, *o_refs):
    h = _rms(x_ref[...], nw_ref[...])
    y = _dot(h.astype(BF16), w_ref[...])
    off = 0
    for o_ref in o_refs:
        n = o_ref.shape[1]
        o_ref[...] = y[:, off:off + n]
        off += n


def norm_proj(x2, nw, w, splits, tm=512):
    T, D = x2.shape
    N = w.shape[1]
    assert sum(splits) == N
    return pl.pallas_call(
        _norm_proj_kernel,
        out_shape=[jax.ShapeDtypeStruct((T, n), F32) for n in splits],
        grid=(T // tm,),
        in_specs=[pl.BlockSpec((tm, D), lambda i: (i, 0)), _const_spec((1, D)), _const_spec((D, N))],
        out_specs=[pl.BlockSpec((tm, n), lambda i: (i, 0)) for n in splits],
        compiler_params=_cparams("parallel"),
        name="norm_proj",
    )(x2, nw, w)


def _ffn_kernel(x_ref, ya_ref, yb_ref, wa_ref, wb_ref, nw_ref, wg_ref, wu_ref, cw_ref, cb_ref, wd_ref, fnw_ref,
                o_ref, gtail_ref, *, tiles_per_seq, final_norm):
    tm = x_ref.shape[0]
    x = (x_ref[...] + _bdot(ya_ref[...], wa_ref[...]) + _bdot(yb_ref[...], wb_ref[...]))
    h = _rms(x, nw_ref[...]).astype(BF16)
    gate = _dot(h, wg_ref[...])
    seq_start = (pl.program_id(0) % tiles_per_seq) == 0
    tail = jnp.where(seq_start, 0.0, gtail_ref[...])
    gtail_ref[...] = gate[tm - HALO:]
    ga = jnp.concatenate([tail, gate], axis=0)
    cw = cw_ref[...]
    conv = (cw[2:3] * gate
            + cw[1:2] * pltpu.roll(ga, 1, 0)[HALO:]
            + cw[0:1] * pltpu.roll(ga, 2, 0)[HALO:]) + cb_ref[...]
    up = _dot(h, wu_ref[...])
    act = (_silu(conv) * up).astype(BF16)
    y = x + _dot(act, wd_ref[...])
    if final_norm:
        y = _rms(y, fnw_ref[...])
    o_ref[...] = y


def ffn(x2, ya, yb, seq_len, w_mix, nw, w_up, cw, cb, wd, fnw, final_norm, tm=256):
    T, D = x2.shape
    FF = w_up.shape[1] // 2
    ka, kb = ya.shape[1], yb.shape[1]
    assert ka % kb == 0 and w_mix.shape[0] == ka + kb
    tile = lambda n: pl.BlockSpec((tm, n), lambda i: (i, 0))
    kern = functools.partial(_ffn_kernel, tiles_per_seq=seq_len // tm, final_norm=final_norm)
    return pl.pallas_call(
        kern,
        out_shape=jax.ShapeDtypeStruct((T, D), F32),
        grid=(T // tm,),
        in_specs=[tile(D), tile(ka), tile(kb),
                  pl.BlockSpec((ka, D), lambda i: (0, 0)), pl.BlockSpec((kb, D), lambda i: (ka // kb, 0)),
                  _const_spec((1, D)),
                  pl.BlockSpec((D, FF), lambda i: (0, 0)), pl.BlockSpec((D, FF), lambda i: (0, 1)),
                  _const_spec((3, FF)), _const_spec((1, FF)), _const_spec((FF, D)), _const_spec((1, D))],
        out_specs=tile(D),
        scratch_shapes=[pltpu.VMEM((HALO, FF), F32)],
        compiler_params=_cparams("arbitrary"),
        name="ffn",
    )(x2, ya, yb, w_mix, w_mix, nw, w_up, w_up, cw, cb, wd, fnw)


def _gla_kernel(q_ref, k_ref, v_ref, g_ref, alo_ref, wa_ref, ba_ref, nw_ref, o_ref, st_ref):
    @pl.when(pl.program_id(2) == 0)
    def _():
        st_ref[...] = jnp.zeros_like(st_ref)

    C = GLA_CHUNK
    tq = q_ref.shape[0]
    nh = st_ref.shape[0]
    ri = _iota2((C, C), 0)
    ci = _iota2((C, C), 1)
    tril = (ri >= ci)
    tril_b = jnp.where(tril, 1.0, 0.0).astype(BF16)
    scale = GLA_DK ** -0.5
    alo = alo_ref[...].astype(BF16)
    sls = [slice(j * C, (j + 1) * C) for j in range(tq // C)]
    o_in, kdv, qd, dec = {}, {}, {}, {}
    for h in range(nh):
        kl = slice(h * GLA_DK, (h + 1) * GLA_DK)
        vl = slice(h * GLA_DVP, (h + 1) * GLA_DVP)
        xg = _dot(alo, wa_ref[h].astype(BF16)) + ba_ref[h]
        la = (jnp.minimum(xg, 0.0) - jnp.log(1.0 + jnp.exp(-jnp.abs(xg)))) * (1.0 / GLA_TAU)
        for j, sl in enumerate(sls):
            b = _dot_2x(tril_b, la[sl])
            b_mid = b[C // 2 - 1:C // 2]
            b_last = b[C - 1:C]
            q = q_ref[sl, kl] * scale
            k = k_ref[sl, kl]
            v = v_ref[sl, vl]
            qe = q * jnp.exp(b - b_mid)
            ke = k * jnp.exp(jnp.minimum(b_mid - b, 80.0))
            att = jnp.where(tril, _bdot_nt(qe, ke), 0.0)
            o_in[h, j] = _bdot(att, v)
            kdv[h, j] = _bdot(v.T, k * jnp.exp(b_last - b))
            qd[h, j] = (q * jnp.exp(b)).astype(BF16)
            dec[h, j] = jnp.exp(b_last)
    sts = {}
    for h in range(nh):
        st = st_ref[h]
        for j in range(len(sls)):
            sts[h, j] = st.astype(BF16)
            st = st * dec[h, j] + kdv[h, j]
        st_ref[h] = st
    for h in range(nh):
        vl = slice(h * GLA_DVP, (h + 1) * GLA_DVP)
        for j, sl in enumerate(sls):
            o = o_in[h, j] + _dot_nt(qd[h, j], sts[h, j])
            ms = jnp.sum(o * o, axis=-1, keepdims=True) * (1.0 / GLA_DV)
            o = o * lax.rsqrt(ms + EPS) * nw_ref[...]
            o_ref[sl, vl] = o * _silu(g_ref[sl, vl])


def gla(q, k, v, g, alo, B, L, wa, ba, nw, tq=512, nh=4):
    T = q.shape[0]
    nq = L // tq
    heads = lambda w: pl.BlockSpec((tq, nh * w), lambda b, h, c: (b * nq + c, h))
    return pl.pallas_call(
        _gla_kernel,
        out_shape=jax.ShapeDtypeStruct((T, GLA_HEADS * GLA_DVP), F32),
        grid=(B, GLA_HEADS // nh, nq),
        in_specs=[heads(GLA_DK), heads(GLA_DK), heads(GLA_DVP), heads(GLA_DVP),
                  pl.BlockSpec((tq, LANE), lambda b, h, c: (b * nq + c, 0)),
                  pl.BlockSpec((nh, LANE, GLA_DK), lambda b, h, c: (h, 0, 0)),
                  pl.BlockSpec((nh, 1, GLA_DK), lambda b, h, c: (h, 0, 0)),
                  _const_spec((1, GLA_DVP))],
        out_specs=heads(GLA_DVP),
        scratch_shapes=[pltpu.VMEM((nh, GLA_DVP, GLA_DK), F32)],
        compiler_params=_cparams("parallel", "parallel", "arbitrary"),
        name="gla",
    )(q, k, v, g, alo, wa, ba, nw)


def _cmul(ar, ai, br, bi):
    return ar * br - ai * bi, ar * bi + ai * br


def _s5_kernel(u_ref, perm_ref, unperm_ref, lr_ref, li_ref, br_ref, bi_ref, cr_ref, ci_ref, d_ref, wg_ref,
               bg_ref, o_ref, sr_ref, si_ref, pr_ref, pi_ref, xr_ref, xi_ref):
    tc = u_ref.shape[0]
    seg = tc // HALO
    ns = lr_ref.shape[1]
    lr = lr_ref[...]
    li = li_ref[...]
    lr8 = jnp.broadcast_to(lr, (HALO, ns))
    li8 = jnp.broadcast_to(li, (HALO, ns))
    rows = lambda j: slice(j * HALO, (j + 1) * HALO)

    @pl.when(pl.program_id(1) == 0)
    def _():
        sr_ref[...] = jnp.zeros_like(sr_ref)
        si_ref[...] = jnp.zeros_like(si_ref)
        zr, zi = lr8, li8
        for j in range(seg):
            pr_ref[rows(j), :] = zr
            pi_ref[rows(j), :] = zi
            zr, zi = _cmul(lr8, li8, zr, zi)

    u = u_ref[...]
    ub = _dot(perm_ref[...], u.astype(BF16)).astype(BF16)
    xr_ref[...] = _dot(ub, br_ref[...])
    xi_ref[...] = _dot(ub, bi_ref[...])
    er = xr_ref[rows(0), :]
    ei = xi_ref[rows(0), :]
    for j in range(1, seg):
        tr, ti = _cmul(lr8, li8, er, ei)
        er = xr_ref[rows(j), :] + tr
        ei = xi_ref[rows(j), :] + ti
        xr_ref[rows(j), :] = er
        xi_ref[rows(j), :] = ei
    mr, mi = lr, li
    for _ in range(seg.bit_length() - 1):
        mr, mi = _cmul(mr, mi, mr, mi)
    row8 = _iota2((HALO, 1), 0)
    cr0, ci0 = _cmul(mr, mi, sr_ref[...], si_ref[...])
    er = er + jnp.where(row8 == 0, cr0, 0.0)
    ei = ei + jnp.where(row8 == 0, ci0, 0.0)
    sh = 1
    while sh < HALO:
        yr, yi = _cmul(mr, mi, jnp.where(row8 < sh, 0.0, pltpu.roll(er, sh, 0)),
                       jnp.where(row8 < sh, 0.0, pltpu.roll(ei, sh, 0)))
        er, ei = er + yr, ei + yi
        mr, mi = _cmul(mr, mi, mr, mi)
        sh *= 2
    inr = jnp.where(row8 == 0, sr_ref[...], pltpu.roll(er, 1, 0))
    ini = jnp.where(row8 == 0, si_ref[...], pltpu.roll(ei, 1, 0))
    sr_ref[...] = er[HALO - 1:]
    si_ref[...] = ei[HALO - 1:]
    ar, ai = _cmul(pr_ref[...].reshape(seg, HALO, ns), pi_ref[...].reshape(seg, HALO, ns), inr[None], ini[None])
    xr = xr_ref[...] + ar.reshape(tc, ns)
    xi = xi_ref[...] + ai.reshape(tc, ns)
    yp = _bdot(xr, cr_ref[...]) - _bdot(xi, ci_ref[...])
    y = _dot_2x(unperm_ref[...], yp) + d_ref[...] * u
    z = 0.5 * y * (1.0 + jnp.tanh(math.sqrt(2.0 / math.pi) * (y + 0.044715 * (y * y * y))))
    o_ref[...] = z * _sigmoid(_bdot(z, wg_ref[...]) + bg_ref[...])


def s5(u, B, L, lr, li, br, bi, cr, ci, d, wg, bg, tc=S5_CHUNK):
    T = u.shape[0]
    nc = L // tc
    NS = S5_NSTATE
    seg = tc // HALO
    rho = jnp.arange(tc)
    perm = (jnp.arange(tc)[None, :] == (seg * (rho % HALO) + rho // HALO)[:, None]).astype(BF16)
    big = pltpu.VMEM((tc, NS), F32)
    return pl.pallas_call(
        _s5_kernel,
        out_shape=jax.ShapeDtypeStruct((T, S5_WIDTH), F32),
        grid=(B, nc),
        in_specs=[pl.BlockSpec((tc, S5_WIDTH), lambda b, c: (b * nc + c, 0)),
                  _const_spec((tc, tc)), _const_spec((tc, tc)),
                  _const_spec((1, NS)), _const_spec((1, NS)),
                  _const_spec((S5_WIDTH, NS)), _const_spec((S5_WIDTH, NS)),
                  _const_spec((NS, S5_WIDTH)), _const_spec((NS, S5_WIDTH)),
                  _const_spec((1, S5_WIDTH)), _const_spec((S5_WIDTH, S5_WIDTH)), _const_spec((1, S5_WIDTH))],
        out_specs=pl.BlockSpec((tc, S5_WIDTH), lambda b, c: (b * nc + c, 0)),
        scratch_shapes=[pltpu.VMEM((1, NS), F32), pltpu.VMEM((1, NS), F32), big, big, big, big],
        compiler_params=_cparams("parallel", "arbitrary"),
        name="s5",
    )(u, perm, perm.T, lr, li, br, bi, cr, ci, d, wg, bg)


def _rwkv_prep_kernel(p_ref, ph_ref, mu_ref, wlr_ref, w0a0_ref, g2_ref, kk_ref, ka_ref, rk_ref, bd_ref,
                      r_ref, a_ref, v_ref, lw_ref, g_ref, bon_ref, kt_ref, bt_ref, lwt_ref, *, tiles_per_seq):
    tm = p_ref.shape[0]
    W = RW_WIDTH
    p = p_ref[...]
    pa = jnp.concatenate([ph_ref[...], p], axis=0)
    prev = pltpu.roll(pa, 1, 0)[HALO:]
    first_dead = jnp.where((pl.program_id(0) % tiles_per_seq) == 0, 1, 0)
    prev = jnp.where(_iota2((tm, 1), 0) < first_dead, 0.0, prev)
    pm = p + (prev - p) * mu_ref[...]
    r = pm[:, :W]
    k = pm[:, W:2 * W]
    v = pm[:, 2 * W:3 * W]
    xwa = pm[:, 3 * W:3 * W + LANE]
    xg = pm[:, 3 * W + LANE:]
    xwa = jnp.where(_iota2((tm, LANE), 1) < 64, jnp.tanh(xwa), xwa)
    wa = _dot_x3(xwa, wlr_ref[...]) + w0a0_ref[...]
    wlog = -_softplus(-wa[:, :W]) - 0.5
    lw = -jnp.exp(wlog)
    a = _sigmoid(wa[:, W:])
    g = _bdot(_sigmoid(xg), g2_ref[...])
    bd = bd_ref[...]
    kk = k * kk_ref[...]
    kk = kk / jnp.maximum(jnp.sqrt(_dot_x2(kk * kk, bd)), 1e-12)
    k2 = k * (1.0 + (a - 1.0) * ka_ref[...])
    r_ref[...] = r
    a_ref[...] = -kk
    v_ref[...] = v
    lw_ref[...] = lw
    g_ref[...] = g.astype(BF16)
    bon_ref[...] = (_dot_x2(r * k2 * rk_ref[...], bd) * v).astype(BF16)
    kt_ref[...] = k2.T
    bt_ref[...] = (kk * a).T
    lwt_ref[...] = lw.T


def rwkv_prep(p, L, mu, wlr, w0a0, g2, kk, ka, rk, bd, tm=512):
    T = p.shape[0]
    W = RW_WIDTH
    hb = tm // HALO
    tok = jax.ShapeDtypeStruct((T, W), F32)
    chn = jax.ShapeDtypeStruct((W, T), F32)
    tok_spec = pl.BlockSpec((tm, W), lambda i: (i, 0))
    chn_spec = pl.BlockSpec((W, tm), lambda i: (0, i))
    kern = functools.partial(_rwkv_prep_kernel, tiles_per_seq=L // tm)
    return pl.pallas_call(
        kern,
        out_shape=[tok] * 4 + [jax.ShapeDtypeStruct((T, W), BF16)] * 2 + [chn] * 3,
        grid=(T // tm,),
        in_specs=[pl.BlockSpec((tm, RW_COLS), lambda i: (i, 0)),
                  pl.BlockSpec((HALO, RW_COLS), lambda i: (jnp.maximum(i * hb - 1, 0), 0)),
                  _const_spec((1, RW_COLS)), _const_spec((LANE, 2 * W)), _const_spec((1, 2 * W)),
                  _const_spec((LANE, W)), _const_spec((1, W)), _const_spec((1, W)), _const_spec((1, W)),
                  _const_spec((W, W))],
        out_specs=[tok_spec] * 6 + [chn_spec] * 3,
        compiler_params=_cparams("parallel"),
        name="rwkv_prep",
    )(p, p, mu, wlr, w0a0, g2, kk, ka, rk, bd)


def _wkv_local_kernel(r_ref, a_ref, v_ref, lw_ref, kt_ref, bt_ref, lwt_ref, wr_ref, y0_ref, m_ref, hl_ref):
    C = RW_CHUNK
    nch = r_ref.shape[0] // C
    ri = _iota2((C, C), 0)
    ci = _iota2((C, C), 1)
    incl = ri >= ci
    strict = ri > ci
    tril_b = jnp.where(incl, 1.0, 0.0).astype(BF16)
    triu_b = jnp.where(ri <= ci, 1.0, 0.0).astype(BF16)
    m0 = ci < RW_HEAD
    bdm = (ri // RW_HEAD) == (ci // RW_HEAD)
    swap = lambda t: pltpu.roll(t, RW_HEAD, 1)
    sls = [slice(j * C, (j + 1) * C) for j in range(nch)]
    chunks = range(nch)
    pairs = [(j, h) for j in chunks for h in range(2)]
    r = [r_ref[sl, :] for sl in sls]
    a = [a_ref[sl, :] for sl in sls]
    lw = [lw_ref[sl, :] for sl in sls]
    kt = [kt_ref[:, sl] for sl in sls]
    bt = [bt_ref[:, sl] for sl in sls]
    v = [v_ref[sl, :] for sl in sls]
    vb = [t.astype(BF16) for t in v]
    c = [_dot_2x(tril_b, lw[j], pack=True) for j in chunks]
    ct = [_dot_x2(lwt_ref[:, sls[j]], triu_b, pack=True) for j in chunks]
    aa = []
    for j in chunks:
        c_mid = c[j][C // 2 - 1:C // 2]
        ct_mid = ct[j][:, C // 2 - 1:C // 2]
        at = a[j] * jnp.exp(c[j] - lw[j] - c_mid)
        rt = r[j] * jnp.exp(c[j] - c_mid)
        e_mid = jnp.exp(ct_mid - ct[j])
        lhs = jnp.concatenate([jnp.where(m0, at, 0.0), jnp.where(m0, 0.0, at),
                               jnp.where(m0, rt, 0.0), jnp.where(m0, 0.0, rt)], axis=0)
        rhs = jnp.concatenate([bt[j] * e_mid, kt[j] * e_mid], axis=1)
        aa.append(_bdot(lhs, rhs))
    n = {(j, h): jnp.where(strict, aa[j][h * C:(h + 1) * C, :C], 0.0) for j, h in pairs}
    ak = {(j, h): jnp.where(strict, aa[j][h * C:(h + 1) * C, C:], 0.0).astype(BF16) for j, h in pairs}
    rb = {(j, h): jnp.where(incl, aa[j][(2 + h) * C:(3 + h) * C, :C], 0.0).astype(BF16) for j, h in pairs}
    rk = {(j, h): jnp.where(incl, aa[j][(2 + h) * C:(3 + h) * C, C:], 0.0).astype(BF16) for j, h in pairs}
    vh = {}
    x = {}
    for j in chunks:
        a_abs = a[j] * jnp.exp(c[j] - lw[j])
        vh[j, 0] = jnp.where(m0, 0.0, swap(v[j])).astype(BF16)
        vh[j, 1] = jnp.where(m0, 0.0, v[j]).astype(BF16)
        x[j, 0] = jnp.where(m0, a_abs, _dot(ak[j, 0], vh[j, 0]))
        x[j, 1] = jnp.where(m0, swap(a_abs), _dot(ak[j, 1], vh[j, 1]))
    sh = 1
    while sh < C:
        top = sh if sh % (2 * HALO) == 0 else 0
        nb = {p: n[p].astype(BF16) for p in pairs}
        if 2 * sh < C:
            res = {p: _dot(nb[p][top:], jnp.concatenate([x[p].astype(BF16), nb[p]], axis=1)) for p in pairs}
            n = {p: res[p][:, C:] for p in pairs}
        else:
            res = {p: _dot(nb[p][top:], x[p].astype(BF16)) for p in pairs}
        if top:
            x = {p: jnp.concatenate([x[p][:top], x[p][top:] + res[p][:, :C]], axis=0) for p in pairs}
            if 2 * sh < C:
                n = {p: jnp.concatenate([jnp.zeros((top, C), F32), n[p]], axis=0) for p in pairs}
        else:
            x = {p: x[p] + res[p][:, :C] for p in pairs}
        sh *= 2
    g = {p: _dot(jnp.concatenate([rb[p], rk[p]], axis=1), jnp.concatenate([x[p].astype(BF16), vh[p]], axis=0))
         for p in pairs}
    for j in chunks:
        wa = jnp.where(m0, x[j, 0], swap(x[j, 1]))
        u0 = jnp.where(m0, swap(x[j, 0]), x[j, 1])
        wr_ref[sls[j], :] = (r[j] * jnp.exp(c[j]) + jnp.where(m0, g[j, 0], swap(g[j, 1]))).astype(BF16)
        y0_ref[sls[j], :] = jnp.where(m0, swap(g[j, 0]), g[j, 1]).astype(BF16)
        ct_last = ct[j][:, C - 1:C]
        e_last = jnp.exp(ct_last - ct[j])
        bh = (bt[j] * e_last).astype(BF16)
        kh = (kt[j] * e_last).astype(BF16)
        rhs = jnp.concatenate([jnp.concatenate([wa.astype(BF16), u0.astype(BF16)], axis=1),
                               jnp.concatenate([jnp.zeros((C, C), BF16), vb[j]], axis=1)], axis=0)
        mh = _dot(jnp.concatenate([bh, kh], axis=1), rhs)
        m_ref[j, 0] = jnp.where(bdm, mh[:, :C], 0.0) + jnp.where(ri == ci, jnp.exp(ct_last), 0.0)
        hl_ref[j, 0] = jnp.where(bdm, mh[:, C:], 0.0)


def wkv_local(r, a, v, lw, kt, bt, lwt, tq=1024):
    T = r.shape[0]
    C = RW_CHUNK
    npair = RW_WIDTH // LANE
    tok = pl.BlockSpec((tq, LANE), lambda i, h: (i, h))
    chn = pl.BlockSpec((LANE, tq), lambda i, h: (h, i))
    mat = pl.BlockSpec((tq // C, 1, LANE, LANE), lambda i, h: (i, h, 0, 0))
    mat_shape = jax.ShapeDtypeStruct((T // C, npair, LANE, LANE), F32)
    return pl.pallas_call(
        _wkv_local_kernel,
        out_shape=[jax.ShapeDtypeStruct((T, RW_WIDTH), BF16)] * 2 + [mat_shape, mat_shape],
        grid=(T // tq, npair),
        in_specs=[tok] * 4 + [chn] * 3,
        out_specs=[tok, tok, mat, mat],
        compiler_params=_cparams("parallel", "parallel"),
        name="wkv_local",
    )(r, a, v, lw, kt, bt, lwt)


def _wkv_scan_kernel(wr_ref, y0_ref, m_ref, hl_ref, g_ref, bon_ref, lnw_ref, lnb_ref, o_ref, h_ref):
    @pl.when(pl.program_id(1) == 0)
    def _():
        h_ref[...] = jnp.zeros_like(h_ref)

    C = RW_CHUNK
    nch = wr_ref.shape[0] // C
    npair = h_ref.shape[0]
    bdm = (_iota2((C, C), 0) // RW_HEAD) == (_iota2((C, C), 1) // RW_HEAD)
    gn = jnp.where(bdm, 1.0 / RW_HEAD, 0.0).astype(BF16)
    h = [h_ref[p] for p in range(npair)]
    for j in range(nch):
        sl = slice(j * C, (j + 1) * C)
        ys = []
        for p in range(npair):
            ln = slice(p * LANE, (p + 1) * LANE)
            ys.append(_dot(wr_ref[sl, ln], h[p].astype(BF16)) + y0_ref[sl, ln])
        h = [_dot_x3(m_ref[j, p], h[p]) + hl_ref[j, p] for p in range(npair)]
        for p in range(npair):
            ln = slice(p * LANE, (p + 1) * LANE)
            mean = _dot_x2(ys[p], gn, pack=True)
            yc = ys[p] - mean
            var = _dot_x2(yc * yc, gn, pack=True)
            yn = yc * lax.rsqrt(var + RW_GN_EPS) * lnw_ref[:, ln] + lnb_ref[:, ln]
            o_ref[sl, ln] = (yn + bon_ref[sl, ln]) * g_ref[sl, ln]
    for p in range(npair):
        h_ref[p] = h[p]


def wkv_scan(wr, y0, m, hl, g, bon, lnw, lnb, B, L, tq=512):
    T, W = wr.shape
    C = RW_CHUNK
    nq = L // tq
    npair = W // LANE
    tok = pl.BlockSpec((tq, W), lambda b, c: (b * nq + c, 0))
    mat = pl.BlockSpec((tq // C, npair, LANE, LANE), lambda b, c: (b * nq + c, 0, 0, 0))
    return pl.pallas_call(
        _wkv_scan_kernel,
        out_shape=jax.ShapeDtypeStruct((T, W), F32),
        grid=(B, nq),
        in_specs=[tok, tok, mat, mat, tok, tok, _const_spec((1, W)), _const_spec((1, W))],
        out_specs=tok,
        scratch_shapes=[pltpu.VMEM((npair, LANE, LANE), F32)],
        compiler_params=_cparams("parallel", "arbitrary"),
        name="wkv_scan",
    )(wr, y0, m, hl, g, bon, lnw, lnb)


def wkv(r, a, v, lw, g, bon, kt, bt, lwt, lnw, lnb, B, L):
    wr, y0, m, hl = wkv_local(r, a, v, lw, kt, bt, lwt)
    return wkv_scan(wr, y0, m, hl, g, bon, lnw, lnb, B, L)


def _mamba_kernel(z_ref, xbc_ref, xh_ref, dt_ref, cw_ref, cb_ref, dtb_ref, an_ref, ex_ref, dsk_ref, nw_ref,
                  o_ref, st_ref):
    @pl.when(pl.program_id(1) == 0)
    def _():
        st_ref[...] = jnp.zeros_like(st_ref)

    C = MB_CHUNK
    W = MB_WIDTH
    N = MB_STATE
    tq = z_ref.shape[0]
    hpg = MB_HEADS // MB_GROUPS
    gw = W // MB_GROUPS
    xbc = xbc_ref[...]
    xa = jnp.concatenate([xh_ref[...], xbc], axis=0)
    halo_dead = jnp.where(pl.program_id(1) == 0, HALO, 0)
    xa = jnp.where(_iota2((HALO + tq, 1), 0) < halo_dead, 0.0, xa)
    cw = cw_ref[...]
    conv = cw[3:4] * xbc + cb_ref[...]
    for kk in range(1, MB_CONV):
        conv = conv + cw[3 - kk:4 - kk] * pltpu.roll(xa, kk, 0)[HALO:]
    xbc = _silu(conv)
    xs = xbc[:, :W]
    dt = _softplus(dt_ref[...] + dtb_ref[...])
    a = dt * an_ref[...]
    ex = ex_ref[...]
    xdt = xs * _dot_x2(dt, ex, pack=True)
    xdt_b = xdt.astype(BF16)
    ri = _iota2((C, C), 0)
    ci = _iota2((C, C), 1)
    causal = ri >= ci
    tril_b = jnp.where(causal, 1.0, 0.0).astype(BF16)
    head_g = _iota2((C, gw), 1) // MB_HEADDIM
    sls = [slice(j * C, (j + 1) * C) for j in range(tq // C)]
    y_loc, upd, cdec, acs_xs, cms = [], [], [], [], []
    for sl in sls:
        acs = _dot_2x(tril_b, a[sl], pack=True)
        acs_t = acs.T
        acs_x = _dot_x2(acs, ex, pack=True)
        ys, cm_g = [], []
        for gi in range(MB_GROUPS):
            bm = xbc[sl, W + gi * N:W + (gi + 1) * N]
            cm = xbc[sl, W + MB_GROUPS * N + gi * N:W + MB_GROUPS * N + (gi + 1) * N].astype(BF16)
            cb = _dot_nt(cm, bm.astype(BF16))
            lm = []
            for e in range(hpg):
                hd = gi * hpg + e
                lmat = jnp.exp(jnp.where(causal, acs[:, hd:hd + 1] - acs_t[hd:hd + 1, :], -jnp.inf))
                lm.append((cb * lmat).astype(BF16))
            res = _dot(jnp.concatenate(lm, axis=0), xdt_b[sl, gi * gw:(gi + 1) * gw])
            yg = res[:C]
            for e in range(1, hpg):
                yg = jnp.where(head_g == e, res[e * C:(e + 1) * C], yg)
            ys.append(yg)
            cm_g.append(cm)
        y_loc.append(ys)
        cms.append(cm_g)
        acs_xs.append(acs_x)
        acs_last = acs_x[C - 1:C]
        xw_t = (xdt[sl] * jnp.exp(acs_last - acs_x)).T
        upd.append(jnp.concatenate(
            [_bdot(xw_t[gi * gw:(gi + 1) * gw], xbc[sl, W + gi * N:W + (gi + 1) * N]) for gi in range(MB_GROUPS)],
            axis=0))
        cdec.append(jnp.exp(acs_x.T[:, C - 1:C]))
    st = st_ref[...]
    sts = []
    for j in range(len(sls)):
        sts.append(st.astype(BF16))
        st = cdec[j] * st + upd[j]
    st_ref[...] = st
    zs = _silu(z_ref[...])
    for j, sl in enumerate(sls):
        e_x = jnp.exp(acs_xs[j])
        parts = []
        for gi in range(MB_GROUPS):
            gl = slice(gi * gw, (gi + 1) * gw)
            off = _dot_nt(cms[j][gi], sts[j][gl])
            yg = (y_loc[j][gi] + off * e_x[:, gl] + xs[sl, gl] * dsk_ref[:, gl]) * zs[sl, gl]
            parts.append(yg * lax.rsqrt(jnp.mean(yg * yg, axis=-1, keepdims=True) + EPS))
        o_ref[sl, :] = jnp.concatenate(parts, axis=1) * nw_ref[...]


def mamba(z, xbc, dt, B, L, cw, cb, dtb, an, ex, dsk, nw, tq=512):
    T = z.shape[0]
    nc = L // tq
    hb = tq // HALO
    return pl.pallas_call(
        _mamba_kernel,
        out_shape=jax.ShapeDtypeStruct((T, MB_WIDTH), F32),
        grid=(B, nc),
        in_specs=[pl.BlockSpec((tq, MB_WIDTH), lambda b, c: (b * nc + c, 0)),
                  pl.BlockSpec((tq, MB_XBC), lambda b, c: (b * nc + c, 0)),
                  pl.BlockSpec((HALO, MB_XBC), lambda b, c: (jnp.maximum((b * nc + c) * hb - 1, 0), 0)),
                  pl.BlockSpec((tq, LANE), lambda b, c: (b * nc + c, 0)),
                  _const_spec((MB_CONV, MB_XBC)), _const_spec((1, MB_XBC)), _const_spec((1, LANE)),
                  _const_spec((1, LANE)), _const_spec((LANE, MB_WIDTH)), _const_spec((1, MB_WIDTH)),
                  _const_spec((1, MB_WIDTH))],
        out_specs=pl.BlockSpec((tq, MB_WIDTH), lambda b, c: (b * nc + c, 0)),
        scratch_shapes=[pltpu.VMEM((MB_WIDTH, MB_STATE), F32)],
        compiler_params=_cparams("parallel", "arbitrary"),
        name="mamba2",
    )(z, xbc, xbc, dt, cw, cb, dtb, an, ex, dsk, nw)


def _pack_even(w_in, wa2, ba, gnorm, w_out):
    nk = GLA_HEADS * GLA_DK
    nv = GLA_HEADS * GLA_DV
    pad_v = lambda w: jnp.pad(w.reshape(D_MODEL, GLA_HEADS, GLA_DV),
                              ((0, 0), (0, 0), (0, GLA_DVP - GLA_DV))).reshape(D_MODEL, GLA_HEADS * GLA_DVP)
    wq = w_in[:, :2 * nk]
    wv = pad_v(w_in[:, 2 * nk:2 * nk + nv])
    wg = pad_v(w_in[:, 2 * nk + nv:2 * nk + 2 * nv])
    walo = jnp.pad(w_in[:, 2 * nk + 2 * nv:2 * nk + 2 * nv + GLA_RANK], ((0, 0), (0, LANE - GLA_RANK)))
    wu = w_in[:, 2 * nk + 2 * nv + GLA_RANK:]
    w_cat = jnp.concatenate([wq, wv, wg, wu, walo], axis=1).astype(BF16)
    wa = jnp.pad(wa2.reshape(GLA_RANK, GLA_HEADS, GLA_DK).transpose(1, 0, 2),
                 ((0, 0), (0, LANE - GLA_RANK), (0, 0)))
    bah = ba.reshape(GLA_HEADS, 1, GLA_DK)
    nw = jnp.pad(gnorm, (0, GLA_DVP - GLA_DV)).reshape(1, GLA_DVP)
    woa = jnp.pad(w_out[:nv].reshape(GLA_HEADS, GLA_DV, D_MODEL),
                  ((0, 0), (0, GLA_DVP - GLA_DV), (0, 0))).reshape(GLA_HEADS * GLA_DVP, D_MODEL).astype(BF16)
    wob = w_out[nv:].astype(BF16)
    return w_cat, wa, bah, nw, woa, wob


def _pack_s5(lam_re, lam_im, log_dt, b_re, b_im, c_re, c_im):
    dt = jnp.exp(log_dt)[:, None]
    mag = jnp.exp(lam_re * dt)
    ang = lam_im * dt
    lb_re = mag * jnp.cos(ang)
    lb_im = mag * jnp.sin(ang)
    den = lam_re * lam_re + lam_im * lam_im
    nr = lb_re - 1.0
    f_re = (nr * lam_re + lb_im * lam_im) / den
    f_im = (lb_im * lam_re - nr * lam_im) / den
    bb_re = f_re[..., None] * b_re - f_im[..., None] * b_im
    bb_im = f_re[..., None] * b_im + f_im[..., None] * b_re
    eye = jnp.eye(S5_GROUPS, dtype=F32)
    bd_in = lambda m: jnp.einsum('gpc,gh->gchp', m, eye).reshape(S5_WIDTH, S5_NSTATE).astype(BF16)
    bd_out = lambda m: jnp.einsum('gcp,gh->gphc', m, eye).reshape(S5_NSTATE, S5_WIDTH).astype(BF16)
    return (lb_re.reshape(1, S5_NSTATE), lb_im.reshape(1, S5_NSTATE),
            bd_in(bb_re), bd_in(bb_im), bd_out(c_re), bd_out(c_im))


def _pack_odd(w_in, w0, w2, a0, a2):
    w_cat = jnp.pad(w_in, ((0, 0), (0, PROJ_COLS - w_in.shape[1]))).astype(BF16)
    W = RW_WIDTH
    wlr = jnp.zeros((LANE, 2 * W), F32).at[:64, :W].set(w2).at[64:, W:].set(a2)
    w0a0 = jnp.concatenate([w0, a0]).reshape(1, 2 * W)
    return w_cat, wlr, w0a0


def kernel(x, norm_mix, norm_ffn, norm_final, e_w_in, e_gla_wa2, e_gla_ba, e_gla_norm, e_s5_lambda_re, e_s5_lambda_im, e_s5_log_dt, e_s5_b_re, e_s5_b_im, e_s5_c_re, e_s5_c_im, e_s5_d, e_s5_w_glu, e_s5_b_glu, e_w_out, o_w_in, o_rw_mu, o_rw_w0, o_rw_w2, o_rw_a0, o_rw_a2, o_rw_g2, o_rw_k_k, o_rw_k_a, o_rw_r_k, o_rw_ln_w, o_rw_ln_b, o_mb_conv_w, o_mb_conv_b, o_mb_dt_bias, o_mb_a_log, o_mb_d, o_mb_norm, o_w_out, ffn_w_up, ffn_conv_w, ffn_conv_b, ffn_w_down):
    B, L, D = x.shape
    T = B * L
    depth = norm_mix.shape[0]
    x2 = x.reshape(T, D)
    row = lambda t: t.reshape(1, -1)
    for i in range(depth):
        j = i // 2
        if i % 2 == 0:
            w_cat, wa, bah, gnw, woa, wob = _pack_even(e_w_in[j], e_gla_wa2[j], e_gla_ba[j], e_gla_norm[j],
                                                       e_w_out[j])
            nkq = GLA_HEADS * GLA_DK
            nvp = GLA_HEADS * GLA_DVP
            q, k, v, g, u, alo = norm_proj(x2, row(norm_mix[i]), w_cat, (nkq, nkq, nvp, nvp, S5_WIDTH, LANE))
            ya = gla(q, k, v, g, alo, B, L, wa, bah, gnw)
            lr, li, br, bi, cr, ci = _pack_s5(e_s5_lambda_re[j], e_s5_lambda_im[j], e_s5_log_dt[j],
                                              e_s5_b_re[j], e_s5_b_im[j], e_s5_c_re[j], e_s5_c_im[j])
            yb = s5(u, B, L, lr, li, br, bi, cr, ci, row(e_s5_d[j]), e_s5_w_glu[j].astype(BF16),
                    row(e_s5_b_glu[j]))
            mix = (ya, yb, jnp.concatenate([woa, wob], axis=0))
        else:
            W = RW_WIDTH
            w_cat, wlr, w0a0 = _pack_odd(o_w_in[j], o_rw_w0[j], o_rw_w2[j], o_rw_a0[j], o_rw_a2[j])
            p_rw, p_z, p_xbc, p_dt = norm_proj(x2, row(norm_mix[i]), w_cat, (RW_COLS, MB_WIDTH, MB_XBC, LANE))
            head_of = jnp.arange(W) // RW_HEAD
            bd = (head_of[:, None] == head_of[None, :]).astype(BF16)
            r, a, v, lw, g, bon, kt, bt, lwt = rwkv_prep(
                p_rw, L, row(o_rw_mu[j]), wlr, w0a0, o_rw_g2[j], row(o_rw_k_k[j]), row(o_rw_k_a[j]),
                row(o_rw_r_k[j]), bd)
            yc = wkv(r, a, v, lw, g, bon, kt, bt, lwt, row(o_rw_ln_w[j]), row(o_rw_ln_b[j]), B, L)
            an = jnp.pad(-jnp.exp(o_mb_a_log[j]), (0, LANE - MB_HEADS)).reshape(1, LANE)
            dtb = jnp.pad(o_mb_dt_bias[j], (0, LANE - MB_HEADS)).reshape(1, LANE)
            ex = (jnp.arange(LANE)[:, None] == (jnp.arange(MB_WIDTH) // MB_HEADDIM)[None, :]).astype(BF16)
            dsk = jnp.repeat(o_mb_d[j], MB_HEADDIM).reshape(1, MB_WIDTH)
            yd = mamba(p_z, p_xbc, p_dt, B, L, o_mb_conv_w[j], row(o_mb_conv_b[j]), dtb, an, ex, dsk, row(o_mb_norm[j]))
            mix = (yc, yd, o_w_out[j].astype(BF16))
        x2 = ffn(x2, mix[0], mix[1], L, mix[2], row(norm_ffn[i]), ffn_w_up[i].astype(BF16), ffn_conv_w[i],
                 row(ffn_conv_b[i]), ffn_w_down[i].astype(BF16), row(norm_final), final_norm=(i == depth - 1))
    return x2.reshape(B, L, D)
```

```python
import functools
import math

import jax
import jax.numpy as jnp
from jax import lax
from jax.experimental import pallas as pl
from jax.experimental.pallas import tpu as pltpu

F32 = jnp.float32
BF16 = jnp.bfloat16
HI = lax.Precision.HIGHEST

D_MODEL = 1024
D_FF = 2816
EPS = 1e-6
LANE = 128
HALO = 8

GLA_HEADS = 4
GLA_DK = 128
GLA_DV = 192
GLA_DVP = 256
GLA_RANK = 16
GLA_TAU = 16.0
GLA_CHUNK = 128
S5_WIDTH = 256
S5_GROUPS = 16
S5_GROUP = 16
S5_STATE = 64
S5_NSTATE = S5_GROUPS * S5_STATE
S5_CHUNK = 512
RW_WIDTH = 512
RW_HEAD = 64
RW_COLS = 1792
RW_GN_EPS = 64e-5
RW_CHUNK = 128
MB_WIDTH = 512
MB_HEADS = 8
MB_HEADDIM = 64
MB_GROUPS = 2
MB_STATE = 128
MB_CONV = 4
MB_CHUNK = 128
MB_XBC = 1024
PROJ_COLS = 3456

VMEM_LIMIT = 56 * 1024 * 1024


def _cparams(*sem):
    return pltpu.CompilerParams(dimension_semantics=sem, vmem_limit_bytes=VMEM_LIMIT)


def _dot(a, b, precision=None):
    return jnp.dot(a, b, preferred_element_type=F32, precision=precision)


def _dot_nt(a, b, precision=None):
    return lax.dot_general(a, b, (((1,), (1,)), ((), ())), preferred_element_type=F32, precision=precision)


def _bdot(a, b):
    return _dot(a.astype(BF16), b.astype(BF16))


def _bdot_nt(a, b):
    return _dot_nt(a.astype(BF16), b.astype(BF16))


def _split(x):
    hi = x.astype(BF16)
    return hi, (x - hi.astype(F32)).astype(BF16)


def _dot_x2(x, e, pack=False):
    hi, lo = _split(x)
    if pack:
        return _dot(jnp.concatenate([hi, lo], axis=1), jnp.concatenate([e, e], axis=0))
    return _dot(hi, e) + _dot(lo, e)


def _dot_2x(e, x, pack=False):
    hi, lo = _split(x)
    if pack:
        return _dot(jnp.concatenate([e, e], axis=1), jnp.concatenate([hi, lo], axis=0))
    return _dot(e, hi) + _dot(e, lo)


def _dot_x3(a, b):
    ah, al = _split(a)
    bh, bl = _split(b)
    return _dot(ah, bh) + (_dot(al, bh) + _dot(ah, bl))


def _rms(x, w):
    return x * lax.rsqrt(jnp.mean(x * x, axis=-1, keepdims=True) + EPS) * w


def _sigmoid(x):
    return 1.0 / (1.0 + jnp.exp(-x))


def _silu(x):
    return x * _sigmoid(x)


def _softplus(x):
    return jnp.maximum(x, 0.0) + jnp.log(1.0 + jnp.exp(-jnp.abs(x)))


def _iota2(shape, dim):
    return lax.broadcasted_iota(jnp.int32, shape, dim)


def _const_spec(shape):
    nd = len(shape)
    return pl.BlockSpec(shape, lambda *_: (0,) * nd)


def _norm_proj_kernel(x_ref, nw_ref, w_ref, *o_refs):
    h = _rms(x_ref[...], nw_ref[...])
    y = _dot(h.astype(BF16), w_ref[...])
    off = 0
    for o_ref in o_refs:
        n = o_ref.shape[1]
        o_ref[...] = y[:, off:off + n]
        off += n


def norm_proj(x2, nw, w, splits, tm=512):
    T, D = x2.shape
    N = w.shape[1]
    assert sum(splits) == N
    return pl.pallas_call(
        _norm_proj_kernel,
        out_shape=[jax.ShapeDtypeStruct((T, n), F32) for n in splits],
        grid=(T // tm,),
        in_specs=[pl.BlockSpec((tm, D), lambda i: (i, 0)), _const_spec((1, D)), _const_spec((D, N))],
        out_specs=[pl.BlockSpec((tm, n), lambda i: (i, 0)) for n in splits],
        compiler_params=_cparams("parallel"),
        name="norm_proj",
    )(x2, nw, w)


def _ffn_kernel(x_ref, ya_ref, yb_ref, wa_ref, wb_ref, nw_ref, wg_ref, wu_ref, cw_ref, cb_ref, wd_ref, fnw_ref,
                o_ref, gtail_ref, *, tiles_per_seq, final_norm):
    tm = x_ref.shape[0]
    x = (x_ref[...] + _bdot(ya_ref[...], wa_ref[...]) + _bdot(yb_ref[...], wb_ref[...]))
    h = _rms(x, nw_ref[...]).astype(BF16)
    gate = _dot(h, wg_ref[...])
    seq_start = (pl.program_id(0) % tiles_per_seq) == 0
    tail = jnp.where(seq_start, 0.0, gtail_ref[...])
    gtail_ref[...] = gate[tm - HALO:]
    ga = jnp.concatenate([tail, gate], axis=0)
    cw = cw_ref[...]
    conv = (cw[2:3] * gate
            + cw[1:2] * pltpu.roll(ga, 1, 0)[HALO:]
            + cw[0:1] * pltpu.roll(ga, 2, 0)[HALO:]) + cb_ref[...]
    up = _dot(h, wu_ref[...])
    act = (_silu(conv) * up).astype(BF16)
    y = x + _dot(act, wd_ref[...])
    if final_norm:
        y = _rms(y, fnw_ref[...])
    o_ref[...] = y


def ffn(x2, ya, yb, seq_len, w_mix, nw, w_up, cw, cb, wd, fnw, final_norm, tm=256):
    T, D = x2.shape
    FF = w_up.shape[1] // 2
    ka, kb = ya.shape[1], yb.shape[1]
    assert ka % kb == 0 and w_mix.shape[0] == ka + kb
    tile = lambda n: pl.BlockSpec((tm, n), lambda i: (i, 0))
    kern = functools.partial(_ffn_kernel, tiles_per_seq=seq_len // tm, final_norm=final_norm)
    return pl.pallas_call(
        kern,
        out_shape=jax.ShapeDtypeStruct((T, D), F32),
        grid=(T // tm,),
        in_specs=[tile(D), tile(ka), tile(kb),
                  pl.BlockSpec((ka, D), lambda i: (0, 0)), pl.BlockSpec((kb, D), lambda i: (ka // kb, 0)),
                  _const_spec((1, D)),
                  pl.BlockSpec((D, FF), lambda i: (0, 0)), pl.BlockSpec((D, FF), lambda i: (0, 1)),
                  _const_spec((3, FF)), _const_spec((1, FF)), _const_spec((FF, D)), _const_spec((1, D))],
        out_specs=tile(D),
        scratch_shapes=[pltpu.VMEM((HALO, FF), F32)],
        compiler_params=_cparams("arbitrary"),
        name="ffn",
    )(x2, ya, yb, w_mix, w_mix, nw, w_up, w_up, cw, cb, wd, fnw)


def _gla_kernel(q_ref, k_ref, v_ref, g_ref, alo_ref, wa_ref, ba_ref, nw_ref, o_ref, st_ref):
    @pl.when(pl.program_id(2) == 0)
    def _():
        st_ref[...] = jnp.zeros_like(st_ref)

    C = GLA_CHUNK
    tq = q_ref.shape[0]
    nh = st_ref.shape[0]
    ri = _iota2((C, C), 0)
    ci = _iota2((C, C), 1)
    tril = (ri >= ci)
    tril_b = jnp.where(tril, 1.0, 0.0).astype(BF16)
    scale = GLA_DK ** -0.5
    alo = alo_ref[...].astype(BF16)
    sls = [slice(j * C, (j + 1) * C) for j in range(tq // C)]
    o_in, kdv, qd, dec = {}, {}, {}, {}
    for h in range(nh):
        kl = slice(h * GLA_DK, (h + 1) * GLA_DK)
        vl = slice(h * GLA_DVP, (h + 1) * GLA_DVP)
        xg = _dot(alo, wa_ref[h].astype(BF16)) + ba_ref[h]
        la = (jnp.minimum(xg, 0.0) - jnp.log(1.0 + jnp.exp(-jnp.abs(xg)))) * (1.0 / GLA_TAU)
        for j, sl in enumerate(sls):
            b = _dot_2x(tril_b, la[sl])
            b_mid = b[C // 2 - 1:C // 2]
            b_last = b[C - 1:C]
            q = q_ref[sl, kl] * scale
            k = k_ref[sl, kl]
            v = v_ref[sl, vl]
            qe = q * jnp.exp(b - b_mid)
            ke = k * jnp.exp(jnp.minimum(b_mid - b, 80.0))
            att = jnp.where(tril, _bdot_nt(qe, ke), 0.0)
            o_in[h, j] = _bdot(att, v)
            kdv[h, j] = _bdot(v.T, k * jnp.exp(b_last - b))
            qd[h, j] = (q * jnp.exp(b)).astype(BF16)
            dec[h, j] = jnp.exp(b_last)
    sts = {}
    for h in range(nh):
        st = st_ref[h]
        for j in range(len(sls)):
            sts[h, j] = st.astype(BF16)
            st = st * dec[h, j] + kdv[h, j]
        st_ref[h] = st
    for h in range(nh):
        vl = slice(h * GLA_DVP, (h + 1) * GLA_DVP)
        for j, sl in enumerate(sls):
            o = o_in[h, j] + _dot_nt(qd[h, j], sts[h, j])
            ms = jnp.sum(o * o, axis=-1, keepdims=True) * (1.0 / GLA_DV)
            o = o * lax.rsqrt(ms + EPS) * nw_ref[...]
            o_ref[sl, vl] = o * _silu(g_ref[sl, vl])


def gla(q, k, v, g, alo, B, L, wa, ba, nw, tq=512, nh=4):
    T = q.shape[0]
    nq = L // tq
    heads = lambda w: pl.BlockSpec((tq, nh * w), lambda b, h, c: (b * nq + c, h))
    return pl.pallas_call(
        _gla_kernel,
        out_shape=jax.ShapeDtypeStruct((T, GLA_HEADS * GLA_DVP), F32),
        grid=(B, GLA_HEADS // nh, nq),
        in_specs=[heads(GLA_DK), heads(GLA_DK), heads(GLA_DVP), heads(GLA_DVP),
                  pl.BlockSpec((tq, LANE), lambda b, h, c: (b * nq + c, 0)),
                  pl.BlockSpec((nh, LANE, GLA_DK), lambda b, h, c: (h, 0, 0)),
                  pl.BlockSpec((nh, 1, GLA_DK), lambda b, h, c: (h, 0, 0)),
                  _const_spec((1, GLA_DVP))],
        out_specs=heads(GLA_DVP),
        scratch_shapes=[pltpu.VMEM((nh, GLA_DVP, GLA_DK), F32)],
        compiler_params=_cparams("parallel", "parallel", "arbitrary"),
        name="gla",
    )(q, k, v, g, alo, wa, ba, nw)


def _cmul(ar, ai, br, bi):
    return ar * br - ai * bi, ar * bi + ai * br


def _s5_kernel(u_ref, perm_ref, unperm_ref, lr_ref, li_ref, br_ref, bi_ref, cr_ref, ci_ref, d_ref, wg_ref,
               bg_ref, o_ref, sr_ref, si_ref, pr_ref, pi_ref, xr_ref, xi_ref):
    tc = u_ref.shape[0]
    seg = tc // HALO
    ns = lr_ref.shape[1]
    lr = lr_ref[...]
    li = li_ref[...]
    lr8 = jnp.broadcast_to(lr, (HALO, ns))
    li8 = jnp.broadcast_to(li, (HALO, ns))
    rows = lambda j: slice(j * HALO, (j + 1) * HALO)

    @pl.when(pl.program_id(1) == 0)
    def _():
        sr_ref[...] = jnp.zeros_like(sr_ref)
        si_ref[...] = jnp.zeros_like(si_ref)
        zr, zi = lr8, li8
        for j in range(seg):
            pr_ref[rows(j), :] = zr
            pi_ref[rows(j), :] = zi
            zr, zi = _cmul(lr8, li8, zr, zi)

    u = u_ref[...]
    ub = _dot(perm_ref[...], u.astype(BF16)).astype(BF16)
    xr_ref[...] = _dot(ub, br_ref[...])
    xi_ref[...] = _dot(ub, bi_ref[...])
    er = xr_ref[rows(0), :]
    ei = xi_ref[rows(0), :]
    for j in range(1, seg):
        tr, ti = _cmul(lr8, li8, er, ei)
        er = xr_ref[rows(j), :] + tr
        ei = xi_ref[rows(j), :] + ti
        xr_ref[rows(j), :] = er
        xi_ref[rows(j), :] = ei
    mr, mi = lr, li
    for _ in range(seg.bit_length() - 1):
        mr, mi = _cmul(mr, mi, mr, mi)
    row8 = _iota2((HALO, 1), 0)
    cr0, ci0 = _cmul(mr, mi, sr_ref[...], si_ref[...])
    er = er + jnp.where(row8 == 0, cr0, 0.0)
    ei = ei + jnp.where(row8 == 0, ci0, 0.0)
    sh = 1
    while sh < HALO:
        yr, yi = _cmul(mr, mi, jnp.where(row8 < sh, 0.0, pltpu.roll(er, sh, 0)),
                       jnp.where(row8 < sh, 0.0, pltpu.roll(ei, sh, 0)))
        er, ei = er + yr, ei + yi
        mr, mi = _cmul(mr, mi, mr, mi)
        sh *= 2
    inr = jnp.where(row8 == 0, sr_ref[...], pltpu.roll(er, 1, 0))
    ini = jnp.where(row8 == 0, si_ref[...], pltpu.roll(ei, 1, 0))
    sr_ref[...] = er[HALO - 1:]
    si_ref[...] = ei[HALO - 1:]
    ar, ai = _cmul(pr_ref[...].reshape(seg, HALO, ns), pi_ref[...].reshape(seg, HALO, ns), inr[None], ini[None])
    xr = xr_ref[...] + ar.reshape(tc, ns)
    xi = xi_ref[...] + ai.reshape(tc, ns)
    yp = _bdot(xr, cr_ref[...]) - _bdot(xi, ci_ref[...])
    y = _dot_2x(unperm_ref[...], yp) + d_ref[...] * u
    z = 0.5 * y * (1.0 + jnp.tanh(math.sqrt(2.0 / math.pi) * (y + 0.044715 * (y * y * y))))
    o_ref[...] = z * _sigmoid(_bdot(z, wg_ref[...]) + bg_ref[...])


def s5(u, B, L, lr, li, br, bi, cr, ci, d, wg, bg, tc=S5_CHUNK):
    T = u.shape[0]
    nc = L // tc
    NS = S5_NSTATE
    seg = tc // HALO
    rho = jnp.arange(tc)
    perm = (jnp.arange(tc)[None, :] == (seg * (rho % HALO) + rho // HALO)[:, None]).astype(BF16)
    big = pltpu.VMEM((tc, NS), F32)
    return pl.pallas_call(
        _s5_kernel,
        out_shape=jax.ShapeDtypeStruct((T, S5_WIDTH), F32),
        grid=(B, nc),
        in_specs=[pl.BlockSpec((tc, S5_WIDTH), lambda b, c: (b * nc + c, 0)),
                  _const_spec((tc, tc)), _const_spec((tc, tc)),
                  _const_spec((1, NS)), _const_spec((1, NS)),
                  _const_spec((S5_WIDTH, NS)), _const_spec((S5_WIDTH, NS)),
                  _const_spec((NS, S5_WIDTH)), _const_spec((NS, S5_WIDTH)),
                  _const_spec((1, S5_WIDTH)), _const_spec((S5_WIDTH, S5_WIDTH)), _const_spec((1, S5_WIDTH))],
        out_specs=pl.BlockSpec((tc, S5_WIDTH), lambda b, c: (b * nc + c, 0)),
        scratch_shapes=[pltpu.VMEM((1, NS), F32), pltpu.VMEM((1, NS), F32), big, big, big, big],
        compiler_params=_cparams("parallel", "arbitrary"),
        name="s5",
    )(u, perm, perm.T, lr, li, br, bi, cr, ci, d, wg, bg)


def _rwkv_prep_kernel(p_ref, ph_ref, mu_ref, wlr_ref, w0a0_ref, g2_ref, kk_ref, ka_ref, rk_ref, bd_ref,
                      r_ref, a_ref, v_ref, lw_ref, g_ref, bon_ref, kt_ref, bt_ref, lwt_ref, *, tiles_per_seq):
    tm = p_ref.shape[0]
    W = RW_WIDTH
    p = p_ref[...]
    pa = jnp.concatenate([ph_ref[...], p], axis=0)
    prev = pltpu.roll(pa, 1, 0)[HALO:]
    first_dead = jnp.where((pl.program_id(0) % tiles_per_seq) == 0, 1, 0)
    prev = jnp.where(_iota2((tm, 1), 0) < first_dead, 0.0, prev)
    pm = p + (prev - p) * mu_ref[...]
    r = pm[:, :W]
    k = pm[:, W:2 * W]
    v = pm[:, 2 * W:3 * W]
    xwa = pm[:, 3 * W:3 * W + LANE]
    xg = pm[:, 3 * W + LANE:]
    xwa = jnp.where(_iota2((tm, LANE), 1) < 64, jnp.tanh(xwa), xwa)
    wa = _dot_x3(xwa, wlr_ref[...]) + w0a0_ref[...]
    wlog = -_softplus(-wa[:, :W]) - 0.5
    lw = -jnp.exp(wlog)
    a = _sigmoid(wa[:, W:])
    g = _bdot(_sigmoid(xg), g2_ref[...])
    bd = bd_ref[...]
    kk = k * kk_ref[...]
    kk = kk / jnp.maximum(jnp.sqrt(_dot_x2(kk * kk, bd)), 1e-12)
    k2 = k * (1.0 + (a - 1.0) * ka_ref[...])
    r_ref[...] = r
    a_ref[...] = -kk
    v_ref[...] = v
    lw_ref[...] = lw
    g_ref[...] = g.astype(BF16)
    bon_ref[...] = (_dot_x2(r * k2 * rk_ref[...], bd) * v).astype(BF16)
    kt_ref[...] = k2.T
    bt_ref[...] = (kk * a).T
    lwt_ref[...] = lw.T


def rwkv_prep(p, L, mu, wlr, w0a0, g2, kk, ka, rk, bd, tm=512):
    T = p.shape[0]
    W = RW_WIDTH
    hb = tm // HALO
    tok = jax.ShapeDtypeStruct((T, W), F32)
    chn = jax.ShapeDtypeStruct((W, T), F32)
    tok_spec = pl.BlockSpec((tm, W), lambda i: (i, 0))
    chn_spec = pl.BlockSpec((W, tm), lambda i: (0, i))
    kern = functools.partial(_rwkv_prep_kernel, tiles_per_seq=L // tm)
    return pl.pallas_call(
        kern,
        out_shape=[tok] * 4 + [jax.ShapeDtypeStruct((T, W), BF16)] * 2 + [chn] * 3,
        grid=(T // tm,),
        in_specs=[pl.BlockSpec((tm, RW_COLS), lambda i: (i, 0)),
                  pl.BlockSpec((HALO, RW_COLS), lambda i: (jnp.maximum(i * hb - 1, 0), 0)),
                  _const_spec((1, RW_COLS)), _const_spec((LANE, 2 * W)), _const_spec((1, 2 * W)),
                  _const_spec((LANE, W)), _const_spec((1, W)), _const_spec((1, W)), _const_spec((1, W)),
                  _const_spec((W, W))],
        out_specs=[tok_spec] * 6 + [chn_spec] * 3,
        compiler_params=_cparams("parallel"),
        name="rwkv_prep",
    )(p, p, mu, wlr, w0a0, g2, kk, ka, rk, bd)


def _wkv_local_kernel(r_ref, a_ref, v_ref, lw_ref, kt_ref, bt_ref, lwt_ref, wr_ref, y0_ref, m_ref, hl_ref):
    C = RW_CHUNK
    nch = r_ref.shape[0] // C
    ri = _iota2((C, C), 0)
    ci = _iota2((C, C), 1)
    incl = ri >= ci
    strict = ri > ci
    tril_b = jnp.where(incl, 1.0, 0.0).astype(BF16)
    triu_b = jnp.where(ri <= ci, 1.0, 0.0).astype(BF16)
    m0 = ci < RW_HEAD
    bdm = (ri // RW_HEAD) == (ci // RW_HEAD)
    swap = lambda t: pltpu.roll(t, RW_HEAD, 1)
    sls = [slice(j * C, (j + 1) * C) for j in range(nch)]
    chunks = range(nch)
    pairs = [(j, h) for j in chunks for h in range(2)]
    r = [r_ref[sl, :] for sl in sls]
    a = [a_ref[sl, :] for sl in sls]
    lw = [lw_ref[sl, :] for sl in sls]
    kt = [kt_ref[:, sl] for sl in sls]
    bt = [bt_ref[:, sl] for sl in sls]
    v = [v_ref[sl, :] for sl in sls]
    vb = [t.astype(BF16) for t in v]
    c = [_dot_2x(tril_b, lw[j], pack=True) for j in chunks]
    ct = [_dot_x2(lwt_ref[:, sls[j]], triu_b, pack=True) for j in chunks]
    aa = []
    for j in chunks:
        c_mid = c[j][C // 2 - 1:C // 2]
        ct_mid = ct[j][:, C // 2 - 1:C // 2]
        at = a[j] * jnp.exp(c[j] - lw[j] - c_mid)
        rt = r[j] * jnp.exp(c[j] - c_mid)
        e_mid = jnp.exp(ct_mid - ct[j])
        lhs = jnp.concatenate([jnp.where(m0, at, 0.0), jnp.where(m0, 0.0, at),
                               jnp.where(m0, rt, 0.0), jnp.where(m0, 0.0, rt)], axis=0)
        rhs = jnp.concatenate([bt[j] * e_mid, kt[j] * e_mid], axis=1)
        aa.append(_bdot(lhs, rhs))
    n = {(j, h): jnp.where(strict, aa[j][h * C:(h + 1) * C, :C], 0.0) for j, h in pairs}
    ak = {(j, h): jnp.where(strict, aa[j][h * C:(h + 1) * C, C:], 0.0).astype(BF16) for j, h in pairs}
    rb = {(j, h): jnp.where(incl, aa[j][(2 + h) * C:(3 + h) * C, :C], 0.0).astype(BF16) for j, h in pairs}
    rk = {(j, h): jnp.where(incl, aa[j][(2 + h) * C:(3 + h) * C, C:], 0.0).astype(BF16) for j, h in pairs}
    vh = {}
    x = {}
    for j in chunks:
        a_abs = a[j] * jnp.exp(c[j] - lw[j])
        vh[j, 0] = jnp.where(m0, 0.0, swap(v[j])).astype(BF16)
        vh[j, 1] = jnp.where(m0, 0.0, v[j]).astype(BF16)
        x[j, 0] = jnp.where(m0, a_abs, _dot(ak[j, 0], vh[j, 0]))
        x[j, 1] = jnp.where(m0, swap(a_abs), _dot(ak[j, 1], vh[j, 1]))
    sh = 1
    while sh < C:
        top = sh if sh % (2 * HALO) == 0 else 0
        nb = {p: n[p].astype(BF16) for p in pairs}
        if 2 * sh < C:
            res = {p: _dot(nb[p][top:], jnp.concatenate([x[p].astype(BF16), nb[p]], axis=1)) for p in pairs}
            n = {p: res[p][:, C:] for p in pairs}
        else:
            res = {p: _dot(nb[p][top:], x[p].astype(BF16)) for p in pairs}
        if top:
            x = {p: jnp.concatenate([x[p][:top], x[p][top:] + res[p][:, :C]], axis=0) for p in pairs}
            if 2 * sh < C:
                n = {p: jnp.concatenate([jnp.zeros((top, C), F32), n[p]], axis=0) for p in pairs}
        else:
            x = {p: x[p] + res[p][:, :C] for p in pairs}
        sh *= 2
    g = {p: _dot(jnp.concatenate([rb[p], rk[p]], axis=1), jnp.concatenate([x[p].astype(BF16), vh[p]], axis=0))
         for p in pairs}
    for j in chunks:
        wa = jnp.where(m0, x[j, 0], swap(x[j, 1]))
        u0 = jnp.where(m0, swap(x[j, 0]), x[j, 1])
        wr_ref[sls[j], :] = (r[j] * jnp.exp(c[j]) + jnp.where(m0, g[j, 0], swap(g[j, 1]))).astype(BF16)
        y0_ref[sls[j], :] = jnp.where(m0, swap(g[j, 0]), g[j, 1]).astype(BF16)
        ct_last = ct[j][:, C - 1:C]
        e_last = jnp.exp(ct_last - ct[j])
        bh = (bt[j] * e_last).astype(BF16)
        kh = (kt[j] * e_last).astype(BF16)
        rhs = jnp.concatenate([jnp.concatenate([wa.astype(BF16), u0.astype(BF16)], axis=1),
                               jnp.concatenate([jnp.zeros((C, C), BF16), vb[j]], axis=1)], axis=0)
        mh = _dot(jnp.concatenate([bh, kh], axis=1), rhs)
        m_ref[j, 0] = jnp.where(bdm, mh[:, :C], 0.0) + jnp.where(ri == ci, jnp.exp(ct_last), 0.0)
        hl_ref[j, 0] = jnp.where(bdm, mh[:, C:], 0.0)


def wkv_local(r, a, v, lw, kt, bt, lwt, tq=1024):
    T = r.shape[0]
    C = RW_CHUNK
    npair = RW_WIDTH // LANE
    tok = pl.BlockSpec((tq, LANE), lambda i, h: (i, h))
    chn = pl.BlockSpec((LANE, tq), lambda i, h: (h, i))
    mat = pl.BlockSpec((tq // C, 1, LANE, LANE), lambda i, h: (i, h, 0, 0))
    mat_shape = jax.ShapeDtypeStruct((T // C, npair, LANE, LANE), F32)
    return pl.pallas_call(
        _wkv_local_kernel,
        out_shape=[jax.ShapeDtypeStruct((T, RW_WIDTH), BF16)] * 2 + [mat_shape, mat_shape],
        grid=(T // tq, npair),
        in_specs=[tok] * 4 + [chn] * 3,
        out_specs=[tok, tok, mat, mat],
        compiler_params=_cparams("parallel", "parallel"),
        name="wkv_local",
    )(r, a, v, lw, kt, bt, lwt)


def _wkv_scan_kernel(wr_ref, y0_ref, m_ref, hl_ref, g_ref, bon_ref, lnw_ref, lnb_ref, o_ref, h_ref):
    @pl.when(pl.program_id(1) == 0)
    def _():
        h_ref[...] = jnp.zeros_like(h_ref)

    C = RW_CHUNK
    nch = wr_ref.shape[0] // C
    npair = h_ref.shape[0]
    bdm = (_iota2((C, C), 0) // RW_HEAD) == (_iota2((C, C), 1) // RW_HEAD)
    gn = jnp.where(bdm, 1.0 / RW_HEAD, 0.0).astype(BF16)
    h = [h_ref[p] for p in range(npair)]
    for j in range(nch):
        sl = slice(j * C, (j + 1) * C)
        ys = []
        for p in range(npair):
            ln = slice(p * LANE, (p + 1) * LANE)
            ys.append(_dot(wr_ref[sl, ln], h[p].astype(BF16)) + y0_ref[sl, ln])
        h = [_dot_x3(m_ref[j, p], h[p]) + hl_ref[j, p] for p in range(npair)]
        for p in range(npair):
            ln = slice(p * LANE, (p + 1) * LANE)
            mean = _dot_x2(ys[p], gn, pack=True)
            yc = ys[p] - mean
            var = _dot_x2(yc * yc, gn, pack=True)
            yn = yc * lax.rsqrt(var + RW_GN_EPS) * lnw_ref[:, ln] + lnb_ref[:, ln]
            o_ref[sl, ln] = (yn + bon_ref[sl, ln]) * g_ref[sl, ln]
    for p in range(npair):
        h_ref[p] = h[p]


def wkv_scan(wr, y0, m, hl, g, bon, lnw, lnb, B, L, tq=512):
    T, W = wr.shape
    C = RW_CHUNK
    nq = L // tq
    npair = W // LANE
    tok = pl.BlockSpec((tq, W), lambda b, c: (b * nq + c, 0))
    mat = pl.BlockSpec((tq // C, npair, LANE, LANE), lambda b, c: (b * nq + c, 0, 0, 0))
    return pl.pallas_call(
        _wkv_scan_kernel,
        out_shape=jax.ShapeDtypeStruct((T, W), F32),
        grid=(B, nq),
        in_specs=[tok, tok, mat, mat, tok, tok, _const_spec((1, W)), _const_spec((1, W))],
        out_specs=tok,
        scratch_shapes=[pltpu.VMEM((npair, LANE, LANE), F32)],
        compiler_params=_cparams("parallel", "arbitrary"),
        name="wkv_scan",
    )(wr, y0, m, hl, g, bon, lnw, lnb)


def wkv(r, a, v, lw, g, bon, kt, bt, lwt, lnw, lnb, B, L):
    wr, y0, m, hl = wkv_local(r, a, v, lw, kt, bt, lwt)
    return wkv_scan(wr, y0, m, hl, g, bon, lnw, lnb, B, L)


def _mamba_kernel(z_ref, xbc_ref, xh_ref, dt_ref, cw_ref, cb_ref, dtb_ref, alog_ref, ex_ref, dsk_ref, nw_ref,
                  o_ref, st_ref):
    @pl.when(pl.program_id(1) == 0)
    def _():
        st_ref[...] = jnp.zeros_like(st_ref)

    C = MB_CHUNK
    W = MB_WIDTH
    N = MB_STATE
    tq = z_ref.shape[0]
    hpg = MB_HEADS // MB_GROUPS
    gw = W // MB_GROUPS
    xbc = xbc_ref[...]
    xa = jnp.concatenate([xh_ref[...], xbc], axis=0)
    halo_dead = jnp.where(pl.program_id(1) == 0, HALO, 0)
    xa = jnp.where(_iota2((HALO + tq, 1), 0) < halo_dead, 0.0, xa)
    cw = cw_ref[...]
    conv = cw[3:4] * xbc + cb_ref[...]
    for kk in range(1, MB_CONV):
        conv = conv + cw[3 - kk:4 - kk] * pltpu.roll(xa, kk, 0)[HALO:]
    xbc = _silu(conv)
    xs = xbc[:, :W]
    dt = _softplus(dt_ref[...] + dtb_ref[...])
    a = dt * -jnp.exp(alog_ref[...])
    ex = ex_ref[...]
    xdt = xs * _dot_x2(dt, ex, pack=True)
    xdt_b = xdt.astype(BF16)
    ri = _iota2((C, C), 0)
    ci = _iota2((C, C), 1)
    causal = ri >= ci
    tril_b = jnp.where(causal, 1.0, 0.0).astype(BF16)
    head_g = _iota2((C, gw), 1) // MB_HEADDIM
    sls = [slice(j * C, (j + 1) * C) for j in range(tq // C)]
    y_loc, upd, cdec, acs_xs, cms = [], [], [], [], []
    for sl in sls:
        acs = _dot_2x(tril_b, a[sl], pack=True)
        acs_t = acs.T
        acs_x = _dot_x2(acs, ex, pack=True)
        ys, cm_g = [], []
        for gi in range(MB_GROUPS):
            bm = xbc[sl, W + gi * N:W + (gi + 1) * N]
            cm = xbc[sl, W + MB_GROUPS * N + gi * N:W + MB_GROUPS * N + (gi + 1) * N].astype(BF16)
            cb = _dot_nt(cm, bm.astype(BF16))
            lm = []
            for e in range(hpg):
                hd = gi * hpg + e
                lmat = jnp.exp(jnp.where(causal, acs[:, hd:hd + 1] - acs_t[hd:hd + 1, :], -jnp.inf))
                lm.append((cb * lmat).astype(BF16))
            res = _dot(jnp.concatenate(lm, axis=0), xdt_b[sl, gi * gw:(gi + 1) * gw])
            yg = res[:C]
            for e in range(1, hpg):
                yg = jnp.where(head_g == e, res[e * C:(e + 1) * C], yg)
            ys.append(yg)
            cm_g.append(cm)
        y_loc.append(ys)
        cms.append(cm_g)
        acs_xs.append(acs_x)
        acs_last = acs_x[C - 1:C]
        xw_t = (xdt[sl] * jnp.exp(acs_last - acs_x)).T
        upd.append(jnp.concatenate(
            [_bdot(xw_t[gi * gw:(gi + 1) * gw], xbc[sl, W + gi * N:W + (gi + 1) * N]) for gi in range(MB_GROUPS)],
            axis=0))
        cdec.append(jnp.exp(acs_x.T[:, C - 1:C]))
    st = st_ref[...]
    sts = []
    for j in range(len(sls)):
        sts.append(st.astype(BF16))
        st = cdec[j] * st + upd[j]
    st_ref[...] = st
    zs = _silu(z_ref[...])
    for j, sl in enumerate(sls):
        e_x = jnp.exp(acs_xs[j])
        parts = []
        for gi in range(MB_GROUPS):
            gl = slice(gi * gw, (gi + 1) * gw)
            off = _dot_nt(cms[j][gi], sts[j][gl])
            yg = (y_loc[j][gi] + off * e_x[:, gl] + xs[sl, gl] * dsk_ref[:, gl]) * zs[sl, gl]
            parts.append(yg * lax.rsqrt(jnp.mean(yg * yg, axis=-1, keepdims=True) + EPS))
        o_ref[sl, :] = jnp.concatenate(parts, axis=1) * nw_ref[...]


def mamba(z, xbc, dt, B, L, cw, cb, dtb, an, ex, dsk, nw, tq=512):
    T = z.shape[0]
    nc = L // tq
    hb = tq // HALO
    return pl.pallas_call(
        _mamba_kernel,
        out_shape=jax.ShapeDtypeStruct((T, MB_WIDTH), F32),
        grid=(B, nc),
        in_specs=[pl.BlockSpec((tq, MB_WIDTH), lambda b, c: (b * nc + c, 0)),
                  pl.BlockSpec((tq, MB_XBC), lambda b, c: (b * nc + c, 0)),
                  pl.BlockSpec((HALO, MB_XBC), lambda b, c: (jnp.maximum((b * nc + c) * hb - 1, 0), 0)),
                  pl.BlockSpec((tq, LANE), lambda b, c: (b * nc + c, 0)),
                  _const_spec((MB_CONV, MB_XBC)), _const_spec((1, MB_XBC)), _const_spec((1, LANE)),
                  _const_spec((1, LANE)), _const_spec((LANE, MB_WIDTH)), _const_spec((1, MB_WIDTH)),
                  _const_spec((1, MB_WIDTH))],
        out_specs=pl.BlockSpec((tq, MB_WIDTH), lambda b, c: (b * nc + c, 0)),
        scratch_shapes=[pltpu.VMEM((MB_WIDTH, MB_STATE), F32)],
        compiler_params=_cparams("parallel", "arbitrary"),
        name="mamba2",
    )(z, xbc, xbc, dt, cw, cb, dtb, an, ex, dsk, nw)


def _pack_even(w_in, wa2, ba, gnorm, w_out):
    nk = GLA_HEADS * GLA_DK
    nv = GLA_HEADS * GLA_DV
    pad_v = lambda w: jnp.pad(w.reshape(D_MODEL, GLA_HEADS, GLA_DV),
                              ((0, 0), (0, 0), (0, GLA_DVP - GLA_DV))).reshape(D_MODEL, GLA_HEADS * GLA_DVP)
    wq = w_in[:, :2 * nk]
    wv = pad_v(w_in[:, 2 * nk:2 * nk + nv])
    wg = pad_v(w_in[:, 2 * nk + nv:2 * nk + 2 * nv])
    walo = jnp.pad(w_in[:, 2 * nk + 2 * nv:2 * nk + 2 * nv + GLA_RANK], ((0, 0), (0, LANE - GLA_RANK)))
    wu = w_in[:, 2 * nk + 2 * nv + GLA_RANK:]
    w_cat = jnp.concatenate([wq, wv, wg, wu, walo], axis=1).astype(BF16)
    wa = jnp.pad(wa2.reshape(GLA_RANK, GLA_HEADS, GLA_DK).transpose(1, 0, 2),
                 ((0, 0), (0, LANE - GLA_RANK), (0, 0)))
    bah = ba.reshape(GLA_HEADS, 1, GLA_DK)
    nw = jnp.pad(gnorm, (0, GLA_DVP - GLA_DV)).reshape(1, GLA_DVP)
    woa = jnp.pad(w_out[:nv].reshape(GLA_HEADS, GLA_DV, D_MODEL),
                  ((0, 0), (0, GLA_DVP - GLA_DV), (0, 0))).reshape(GLA_HEADS * GLA_DVP, D_MODEL).astype(BF16)
    wob = w_out[nv:].astype(BF16)
    return w_cat, wa, bah, nw, woa, wob


def _s5_zoh_kernel(lr_ref, li_ref, ldt_ref, br_ref, bi_ref, lbr_ref, lbi_ref, bbr_ref, bbi_ref):
    lam_re = lr_ref[...]
    lam_im = li_ref[...]
    dt = jnp.exp(ldt_ref[...])
    mag = jnp.exp(lam_re * dt)
    ang = lam_im * dt
    lb_re = mag * jnp.cos(ang)
    lb_im = mag * jnp.sin(ang)
    den = lam_re * lam_re + lam_im * lam_im
    nr = lb_re - 1.0
    f_re = (nr * lam_re + lb_im * lam_im) / den
    f_im = (lb_im * lam_re - nr * lam_im) / den
    b_re = br_ref[...]
    b_im = bi_ref[...]
    lbr_ref[...] = lb_re
    lbi_ref[...] = lb_im
    bbr_ref[...] = f_re * b_re - f_im * b_im
    bbi_ref[...] = f_re * b_im + f_im * b_re


def _pack_s5(lam_re, lam_im, log_dt, b_re, b_im, c_re, c_im):
    NS = S5_NSTATE
    col = lambda t: t.reshape(NS, 1)
    vec = jax.ShapeDtypeStruct((NS, 1), F32)
    mat = jax.ShapeDtypeStruct((NS, S5_GROUP), F32)
    lb_re, lb_im, bb_re, bb_im = pl.pallas_call(
        _s5_zoh_kernel, out_shape=[vec, vec, mat, mat], name="s5_zoh",
    )(col(lam_re), col(lam_im), col(jnp.repeat(log_dt, S5_STATE)), b_re.reshape(NS, S5_GROUP),
      b_im.reshape(NS, S5_GROUP))
    eye = jnp.eye(S5_GROUPS, dtype=F32)
    grp = lambda m: m.reshape(S5_GROUPS, S5_STATE, S5_GROUP)
    bd_in = lambda m: jnp.einsum('gpc,gh->gchp', grp(m), eye).reshape(S5_WIDTH, NS).astype(BF16)
    bd_out = lambda m: jnp.einsum('gcp,gh->gphc', m, eye).reshape(NS, S5_WIDTH).astype(BF16)
    return (lb_re.reshape(1, NS), lb_im.reshape(1, NS), bd_in(bb_re), bd_in(bb_im), bd_out(c_re), bd_out(c_im))


def _pack_odd(w_in, w0, w2, a0, a2):
    w_cat = jnp.pad(w_in, ((0, 0), (0, PROJ_COLS - w_in.shape[1]))).astype(BF16)
    W = RW_WIDTH
    wlr = jnp.zeros((LANE, 2 * W), F32).at[:64, :W].set(w2).at[64:, W:].set(a2)
    w0a0 = jnp.concatenate([w0, a0]).reshape(1, 2 * W)
    return w_cat, wlr, w0a0


def kernel(x, norm_mix, norm_ffn, norm_final, e_w_in, e_gla_wa2, e_gla_ba, e_gla_norm, e_s5_lambda_re, e_s5_lambda_im, e_s5_log_dt, e_s5_b_re, e_s5_b_im, e_s5_c_re, e_s5_c_im, e_s5_d, e_s5_w_glu, e_s5_b_glu, e_w_out, o_w_in, o_rw_mu, o_rw_w0, o_rw_w2, o_rw_a0, o_rw_a2, o_rw_g2, o_rw_k_k, o_rw_k_a, o_rw_r_k, o_rw_ln_w, o_rw_ln_b, o_mb_conv_w, o_mb_conv_b, o_mb_dt_bias, o_mb_a_log, o_mb_d, o_mb_norm, o_w_out, ffn_w_up, ffn_conv_w, ffn_conv_b, ffn_w_down):
    B, L, D = x.shape
    T = B * L
    depth = norm_mix.shape[0]
    x2 = x.reshape(T, D)
    row = lambda t: t.reshape(1, -1)
    for i in range(depth):
        j = i // 2
        if i % 2 == 0:
            w_cat, wa, bah, gnw, woa, wob = _pack_even(e_w_in[j], e_gla_wa2[j], e_gla_ba[j], e_gla_norm[j],
                                                       e_w_out[j])
            nkq = GLA_HEADS * GLA_DK
            nvp = GLA_HEADS * GLA_DVP
            q, k, v, g, u, alo = norm_proj(x2, row(norm_mix[i]), w_cat, (nkq, nkq, nvp, nvp, S5_WIDTH, LANE))
            ya = gla(q, k, v, g, alo, B, L, wa, bah, gnw)
            lr, li, br, bi, cr, ci = _pack_s5(e_s5_lambda_re[j], e_s5_lambda_im[j], e_s5_log_dt[j],
                                              e_s5_b_re[j], e_s5_b_im[j], e_s5_c_re[j], e_s5_c_im[j])
            yb = s5(u, B, L, lr, li, br, bi, cr, ci, row(e_s5_d[j]), e_s5_w_glu[j].astype(BF16),
                    row(e_s5_b_glu[j]))
            mix = (ya, yb, jnp.concatenate([woa, wob], axis=0))
        else:
            W = RW_WIDTH
            w_cat, wlr, w0a0 = _pack_odd(o_w_in[j], o_rw_w0[j], o_rw_w2[j], o_rw_a0[j], o_rw_a2[j])
            p_rw, p_z, p_xbc, p_dt = norm_proj(x2, row(norm_mix[i]), w_cat, (RW_COLS, MB_WIDTH, MB_XBC, LANE))
            head_of = jnp.arange(W) // RW_HEAD
            bd = (head_of[:, None] == head_of[None, :]).astype(BF16)
            r, a, v, lw, g, bon, kt, bt, lwt = rwkv_prep(
                p_rw, L, row(o_rw_mu[j]), wlr, w0a0, o_rw_g2[j], row(o_rw_k_k[j]), row(o_rw_k_a[j]),
                row(o_rw_r_k[j]), bd)
            yc = wkv(r, a, v, lw, g, bon, kt, bt, lwt, row(o_rw_ln_w[j]), row(o_rw_ln_b[j]), B, L)
            an = jnp.pad(o_mb_a_log[j], (0, LANE - MB_HEADS)).reshape(1, LANE)
            dtb = jnp.pad(o_mb_dt_bias[j], (0, LANE - MB_HEADS)).reshape(1, LANE)
            ex = (jnp.arange(LANE)[:, None] == (jnp.arange(MB_WIDTH) // MB_HEADDIM)[None, :]).astype(BF16)
            dsk = jnp.repeat(o_mb_d[j], MB_HEADDIM).reshape(1, MB_WIDTH)
            yd = mamba(p_z, p_xbc, p_dt, B, L, o_mb_conv_w[j], row(o_mb_conv_b[j]), dtb, an, ex, dsk, row(o_mb_norm[j]))
            mix = (yc, yd, o_w_out[j].astype(BF16))
        x2 = ffn(x2, mix[0], mix[1], L, mix[2], row(norm_ffn[i]), ffn_w_up[i].astype(BF16), ffn_conv_w[i],
                 row(ffn_conv_b[i]), ffn_w_down[i].astype(BF16), row(norm_final), final_norm=(i == depth - 1))
    return x2.reshape(B, L, D)
```

```python
import functools
import math

import jax
import jax.numpy as jnp
from jax import lax
from jax.experimental import pallas as pl
from jax.experimental.pallas import tpu as pltpu

F32 = jnp.float32
BF16 = jnp.bfloat16
HI = lax.Precision.HIGHEST

D_MODEL = 1024
D_FF = 2816
EPS = 1e-6
LANE = 128
HALO = 8

GLA_HEADS = 4
GLA_DK = 128
GLA_DV = 192
GLA_DVP = 256
GLA_RANK = 16
GLA_TAU = 16.0
GLA_CHUNK = 128
S5_WIDTH = 256
S5_GROUPS = 16
S5_GROUP = 16
S5_STATE = 64
S5_NSTATE = S5_GROUPS * S5_STATE
S5_CHUNK = 512
RW_WIDTH = 512
RW_HEAD = 64
RW_COLS = 1792
RW_GN_EPS = 64e-5
RW_CHUNK = 128
MB_WIDTH = 512
MB_HEADS = 8
MB_HEADDIM = 64
MB_GROUPS = 2
MB_STATE = 128
MB_CONV = 4
MB_CHUNK = 128
MB_XBC = 1024
PROJ_COLS = 3456

VMEM_LIMIT = 56 * 1024 * 1024


def _cparams(*sem):
    return pltpu.CompilerParams(dimension_semantics=sem, vmem_limit_bytes=VMEM_LIMIT)


def _dot(a, b, precision=None):
    return jnp.dot(a, b, preferred_element_type=F32, precision=precision)


def _dot_nt(a, b, precision=None):
    return lax.dot_general(a, b, (((1,), (1,)), ((), ())), preferred_element_type=F32, precision=precision)


def _bdot(a, b):
    return _dot(a.astype(BF16), b.astype(BF16))


def _bdot_nt(a, b):
    return _dot_nt(a.astype(BF16), b.astype(BF16))


def _split(x):
    hi = x.astype(BF16)
    return hi, (x - hi.astype(F32)).astype(BF16)


def _dot_x2(x, e, pack=False):
    hi, lo = _split(x)
    if pack:
        return _dot(jnp.concatenate([hi, lo], axis=1), jnp.concatenate([e, e], axis=0))
    return _dot(hi, e) + _dot(lo, e)


def _dot_2x(e, x, pack=False):
    hi, lo = _split(x)
    if pack:
        return _dot(jnp.concatenate([e, e], axis=1), jnp.concatenate([hi, lo], axis=0))
    return _dot(e, hi) + _dot(e, lo)


def _dot_x3(a, b):
    ah, al = _split(a)
    bh, bl = _split(b)
    return _dot(ah, bh) + (_dot(al, bh) + _dot(ah, bl))


def _rms(x, w):
    return x * lax.rsqrt(jnp.mean(x * x, axis=-1, keepdims=True) + EPS) * w


def _sigmoid(x):
    return 1.0 / (1.0 + jnp.exp(-x))


def _silu(x):
    return x * _sigmoid(x)


def _softplus(x):
    return jnp.maximum(x, 0.0) + jnp.log(1.0 + jnp.exp(-jnp.abs(x)))


def _iota2(shape, dim):
    return lax.broadcasted_iota(jnp.int32, shape, dim)


def _const_spec(shape):
    nd = len(shape)
    return pl.BlockSpec(shape, lambda *_: (0,) * nd)


def _norm_proj_kernel(x_ref, nw_ref, w_ref, *o_refs):
    h = _rms(x_ref[...], nw_ref[...])
    y = _dot(h.astype(BF16), w_ref[...])
    off = 0
    for o_ref in o_refs:
        n = o_ref.shape[1]
        o_ref[...] = y[:, off:off + n]
        off += n


def norm_proj(x2, nw, w, splits, tm=512):
    T, D = x2.shape
    N = w.shape[1]
    assert sum(splits) == N
    return pl.pallas_call(
        _norm_proj_kernel,
        out_shape=[jax.ShapeDtypeStruct((T, n), F32) for n in splits],
        grid=(T // tm,),
        in_specs=[pl.BlockSpec((tm, D), lambda i: (i, 0)), _const_spec((1, D)), _const_spec((D, N))],
        out_specs=[pl.BlockSpec((tm, n), lambda i: (i, 0)) for n in splits],
        compiler_params=_cparams("parallel"),
        name="norm_proj",
    )(x2, nw, w)


def _ffn_kernel(x_ref, ya_ref, yb_ref, wa_ref, wb_ref, nw_ref, wg_ref, wu_ref, cw_ref, cb_ref, wd_ref, fnw_ref,
                o_ref, gtail_ref, *, tiles_per_seq, final_norm):
    tm = x_ref.shape[0]
    x = (x_ref[...] + _bdot(ya_ref[...], wa_ref[...]) + _bdot(yb_ref[...], wb_ref[...]))
    h = _rms(x, nw_ref[...]).astype(BF16)
    gate = _dot(h, wg_ref[...])
    seq_start = (pl.program_id(0) % tiles_per_seq) == 0
    tail = jnp.where(seq_start, 0.0, gtail_ref[...])
    gtail_ref[...] = gate[tm - HALO:]
    ga = jnp.concatenate([tail, gate], axis=0)
    cw = cw_ref[...]
    conv = (cw[2:3] * gate
            + cw[1:2] * pltpu.roll(ga, 1, 0)[HALO:]
            + cw[0:1] * pltpu.roll(ga, 2, 0)[HALO:]) + cb_ref[...]
    up = _dot(h, wu_ref[...])
    act = (_silu(conv) * up).astype(BF16)
    y = x + _dot(act, wd_ref[...])
    if final_norm:
        y = _rms(y, fnw_ref[...])
    o_ref[...] = y


def ffn(x2, ya, yb, seq_len, w_mix, nw, w_up, cw, cb, wd, fnw, final_norm, tm=256):
    T, D = x2.shape
    FF = w_up.shape[1] // 2
    ka, kb = ya.shape[1], yb.shape[1]
    assert ka % kb == 0 and w_mix.shape[0] == ka + kb
    tile = lambda n: pl.BlockSpec((tm, n), lambda i: (i, 0))
    kern = functools.partial(_ffn_kernel, tiles_per_seq=seq_len // tm, final_norm=final_norm)
    return pl.pallas_call(
        kern,
        out_shape=jax.ShapeDtypeStruct((T, D), F32),
        grid=(T // tm,),
        in_specs=[tile(D), tile(ka), tile(kb),
                  pl.BlockSpec((ka, D), lambda i: (0, 0)), pl.BlockSpec((kb, D), lambda i: (ka // kb, 0)),
                  _const_spec((1, D)),
                  pl.BlockSpec((D, FF), lambda i: (0, 0)), pl.BlockSpec((D, FF), lambda i: (0, 1)),
                  _const_spec((3, FF)), _const_spec((1, FF)), _const_spec((FF, D)), _const_spec((1, D))],
        out_specs=tile(D),
        scratch_shapes=[pltpu.VMEM((HALO, FF), F32)],
        compiler_params=_cparams("arbitrary"),
        name="ffn",
    )(x2, ya, yb, w_mix, w_mix, nw, w_up, w_up, cw, cb, wd, fnw)


def _gla_kernel(q_ref, k_ref, v_ref, g_ref, alo_ref, wa_ref, ba_ref, nw_ref, o_ref, st_ref):
    @pl.when(pl.program_id(2) == 0)
    def _():
        st_ref[...] = jnp.zeros_like(st_ref)

    C = GLA_CHUNK
    tq = q_ref.shape[0]
    nh = st_ref.shape[0]
    ri = _iota2((C, C), 0)
    ci = _iota2((C, C), 1)
    tril = (ri >= ci)
    tril_b = jnp.where(tril, 1.0, 0.0).astype(BF16)
    scale = GLA_DK ** -0.5
    alo = alo_ref[...].astype(BF16)
    sls = [slice(j * C, (j + 1) * C) for j in range(tq // C)]
    o_in, kdv, qd, dec = {}, {}, {}, {}
    for h in range(nh):
        kl = slice(h * GLA_DK, (h + 1) * GLA_DK)
        vl = slice(h * GLA_DVP, (h + 1) * GLA_DVP)
        xg = _dot(alo, wa_ref[h].astype(BF16)) + ba_ref[h]
        la = (jnp.minimum(xg, 0.0) - jnp.log(1.0 + jnp.exp(-jnp.abs(xg)))) * (1.0 / GLA_TAU)
        for j, sl in enumerate(sls):
            b = _dot_2x(tril_b, la[sl])
            b_mid = b[C // 2 - 1:C // 2]
            b_last = b[C - 1:C]
            q = q_ref[sl, kl] * scale
            k = k_ref[sl, kl]
            v = v_ref[sl, vl]
            qe = q * jnp.exp(b - b_mid)
            ke = k * jnp.exp(jnp.minimum(b_mid - b, 80.0))
            att = jnp.where(tril, _bdot_nt(qe, ke), 0.0)
            o_in[h, j] = _bdot(att, v)
            kdv[h, j] = _bdot(v.T, k * jnp.exp(b_last - b))
            qd[h, j] = (q * jnp.exp(b)).astype(BF16)
            dec[h, j] = jnp.exp(b_last)
    sts = {}
    for h in range(nh):
        st = st_ref[h]
        for j in range(len(sls)):
            sts[h, j] = st.astype(BF16)
            st = st * dec[h, j] + kdv[h, j]
        st_ref[h] = st
    for h in range(nh):
        vl = slice(h * GLA_DVP, (h + 1) * GLA_DVP)
        for j, sl in enumerate(sls):
            o = o_in[h, j] + _dot_nt(qd[h, j], sts[h, j])
            ms = jnp.sum(o * o, axis=-1, keepdims=True) * (1.0 / GLA_DV)
            o = o * lax.rsqrt(ms + EPS) * nw_ref[...]
            o_ref[sl, vl] = o * _silu(g_ref[sl, vl])


def gla(q, k, v, g, alo, B, L, wa, ba, nw, tq=512, nh=4):
    T = q.shape[0]
    nq = L // tq
    heads = lambda w: pl.BlockSpec((tq, nh * w), lambda b, h, c: (b * nq + c, h))
    return pl.pallas_call(
        _gla_kernel,
        out_shape=jax.ShapeDtypeStruct((T, GLA_HEADS * GLA_DVP), F32),
        grid=(B, GLA_HEADS // nh, nq),
        in_specs=[heads(GLA_DK), heads(GLA_DK), heads(GLA_DVP), heads(GLA_DVP),
                  pl.BlockSpec((tq, LANE), lambda b, h, c: (b * nq + c, 0)),
                  pl.BlockSpec((nh, LANE, GLA_DK), lambda b, h, c: (h, 0, 0)),
                  pl.BlockSpec((nh, 1, GLA_DK), lambda b, h, c: (h, 0, 0)),
                  _const_spec((1, GLA_DVP))],
        out_specs=heads(GLA_DVP),
        scratch_shapes=[pltpu.VMEM((nh, GLA_DVP, GLA_DK), F32)],
        compiler_params=_cparams("parallel", "parallel", "arbitrary"),
        name="gla",
    )(q, k, v, g, alo, wa, ba, nw)


def _cmul(ar, ai, br, bi):
    return ar * br - ai * bi, ar * bi + ai * br


def _s5_kernel(u_ref, perm_ref, unperm_ref, lr_ref, li_ref, br_ref, bi_ref, cr_ref, ci_ref, d_ref, wg_ref,
               bg_ref, o_ref, sr_ref, si_ref, pr_ref, pi_ref, xr_ref, xi_ref):
    tc = u_ref.shape[0]
    seg = tc // HALO
    ns = lr_ref.shape[1]
    lr = lr_ref[...]
    li = li_ref[...]
    lr8 = jnp.broadcast_to(lr, (HALO, ns))
    li8 = jnp.broadcast_to(li, (HALO, ns))
    rows = lambda j: slice(j * HALO, (j + 1) * HALO)

    @pl.when(pl.program_id(1) == 0)
    def _():
        sr_ref[...] = jnp.zeros_like(sr_ref)
        si_ref[...] = jnp.zeros_like(si_ref)
        zr, zi = lr8, li8
        for j in range(seg):
            pr_ref[rows(j), :] = zr
            pi_ref[rows(j), :] = zi
            zr, zi = _cmul(lr8, li8, zr, zi)

    u = u_ref[...]
    ub = _dot(perm_ref[...], u.astype(BF16)).astype(BF16)
    xr_ref[...] = _dot(ub, br_ref[...])
    xi_ref[...] = _dot(ub, bi_ref[...])
    er = xr_ref[rows(0), :]
    ei = xi_ref[rows(0), :]
    for j in range(1, seg):
        tr, ti = _cmul(lr8, li8, er, ei)
        er = xr_ref[rows(j), :] + tr
        ei = xi_ref[rows(j), :] + ti
        xr_ref[rows(j), :] = er
        xi_ref[rows(j), :] = ei
    mr, mi = lr, li
    for _ in range(seg.bit_length() - 1):
        mr, mi = _cmul(mr, mi, mr, mi)
    row8 = _iota2((HALO, 1), 0)
    cr0, ci0 = _cmul(mr, mi, sr_ref[...], si_ref[...])
    er = er + jnp.where(row8 == 0, cr0, 0.0)
    ei = ei + jnp.where(row8 == 0, ci0, 0.0)
    sh = 1
    while sh < HALO:
        yr, yi = _cmul(mr, mi, jnp.where(row8 < sh, 0.0, pltpu.roll(er, sh, 0)),
                       jnp.where(row8 < sh, 0.0, pltpu.roll(ei, sh, 0)))
        er, ei = er + yr, ei + yi
        mr, mi = _cmul(mr, mi, mr, mi)
        sh *= 2
    inr = jnp.where(row8 == 0, sr_ref[...], pltpu.roll(er, 1, 0))
    ini = jnp.where(row8 == 0, si_ref[...], pltpu.roll(ei, 1, 0))
    sr_ref[...] = er[HALO - 1:]
    si_ref[...] = ei[HALO - 1:]
    ar, ai = _cmul(pr_ref[...].reshape(seg, HALO, ns), pi_ref[...].reshape(seg, HALO, ns), inr[None], ini[None])
    xr = xr_ref[...] + ar.reshape(tc, ns)
    xi = xi_ref[...] + ai.reshape(tc, ns)
    yp = _bdot(xr, cr_ref[...]) - _bdot(xi, ci_ref[...])
    y = _dot_2x(unperm_ref[...], yp) + d_ref[...] * u
    z = 0.5 * y * (1.0 + jnp.tanh(math.sqrt(2.0 / math.pi) * (y + 0.044715 * (y * y * y))))
    o_ref[...] = z * _sigmoid(_bdot(z, wg_ref[...]) + bg_ref[...])


def s5(u, B, L, lr, li, br, bi, cr, ci, d, wg, bg, tc=S5_CHUNK):
    T = u.shape[0]
    nc = L // tc
    NS = S5_NSTATE
    seg = tc // HALO
    rho = jnp.arange(tc)
    perm = (jnp.arange(tc)[None, :] == (seg * (rho % HALO) + rho // HALO)[:, None]).astype(BF16)
    big = pltpu.VMEM((tc, NS), F32)
    return pl.pallas_call(
        _s5_kernel,
        out_shape=jax.ShapeDtypeStruct((T, S5_WIDTH), F32),
        grid=(B, nc),
        in_specs=[pl.BlockSpec((tc, S5_WIDTH), lambda b, c: (b * nc + c, 0)),
                  _const_spec((tc, tc)), _const_spec((tc, tc)),
                  _const_spec((1, NS)), _const_spec((1, NS)),
                  _const_spec((S5_WIDTH, NS)), _const_spec((S5_WIDTH, NS)),
                  _const_spec((NS, S5_WIDTH)), _const_spec((NS, S5_WIDTH)),
                  _const_spec((1, S5_WIDTH)), _const_spec((S5_WIDTH, S5_WIDTH)), _const_spec((1, S5_WIDTH))],
        out_specs=pl.BlockSpec((tc, S5_WIDTH), lambda b, c: (b * nc + c, 0)),
        scratch_shapes=[pltpu.VMEM((1, NS), F32), pltpu.VMEM((1, NS), F32), big, big, big, big],
        compiler_params=_cparams("parallel", "arbitrary"),
        name="s5",
    )(u, perm, perm.T, lr, li, br, bi, cr, ci, d, wg, bg)


def _proj_rwkv_kernel(x_ref, nw_ref, w_ref, mu_ref, wlr_ref, w0a0_ref, g2_ref, kk_ref, ka_ref, rk_ref, bd_ref,
                      z_ref, xbc_ref, dt_ref, r_ref, a_ref, v_ref, lw_ref, g_ref, bon_ref, kt_ref, bt_ref, lwt_ref,
                      tail_ref, *, tiles_per_seq):
    tm = x_ref.shape[0]
    W = RW_WIDTH
    y = _dot(_rms(x_ref[...], nw_ref[...]).astype(BF16), w_ref[...])
    p = y[:, :RW_COLS]
    z_ref[...] = y[:, RW_COLS:RW_COLS + MB_WIDTH]
    xbc_ref[...] = y[:, RW_COLS + MB_WIDTH:RW_COLS + MB_WIDTH + MB_XBC]
    dt_ref[...] = y[:, RW_COLS + MB_WIDTH + MB_XBC:]
    seq_start = (pl.program_id(0) % tiles_per_seq) == 0
    tail = jnp.where(seq_start, 0.0, tail_ref[...])
    tail_ref[...] = p[tm - HALO:]
    prev = pltpu.roll(jnp.concatenate([tail, p], axis=0), 1, 0)[HALO:]
    pm = p + (prev - p) * mu_ref[...]
    r = pm[:, :W]
    k = pm[:, W:2 * W]
    v = pm[:, 2 * W:3 * W]
    xwa = pm[:, 3 * W:3 * W + LANE]
    xg = pm[:, 3 * W + LANE:]
    xwa = jnp.where(_iota2((tm, LANE), 1) < 64, jnp.tanh(xwa), xwa)
    wa = _dot_x3(xwa, wlr_ref[...]) + w0a0_ref[...]
    wlog = -_softplus(-wa[:, :W]) - 0.5
    lw = -jnp.exp(wlog)
    a = _sigmoid(wa[:, W:])
    g = _bdot(_sigmoid(xg), g2_ref[...])
    bd = bd_ref[...]
    kk = k * kk_ref[...]
    kk = kk / jnp.maximum(jnp.sqrt(_dot_x2(kk * kk, bd)), 1e-12)
    k2 = k * (1.0 + (a - 1.0) * ka_ref[...])
    r_ref[...] = r
    a_ref[...] = -kk
    v_ref[...] = v
    lw_ref[...] = lw
    g_ref[...] = g.astype(BF16)
    bon_ref[...] = (_dot_x2(r * k2 * rk_ref[...], bd) * v).astype(BF16)
    kt_ref[...] = k2.T
    bt_ref[...] = (kk * a).T
    lwt_ref[...] = lw.T


def proj_rwkv(x2, L, nw, w, mu, wlr, w0a0, g2, kk, ka, rk, bd, tm=512):
    T, D = x2.shape
    N = w.shape[1]
    W = RW_WIDTH
    tok = lambda n, dt=F32: jax.ShapeDtypeStruct((T, n), dt)
    chn = jax.ShapeDtypeStruct((W, T), F32)
    tok_spec = lambda n: pl.BlockSpec((tm, n), lambda i: (i, 0))
    chn_spec = pl.BlockSpec((W, tm), lambda i: (0, i))
    kern = functools.partial(_proj_rwkv_kernel, tiles_per_seq=L // tm)
    return pl.pallas_call(
        kern,
        out_shape=[tok(MB_WIDTH), tok(MB_XBC), tok(LANE)] + [tok(W)] * 4 + [tok(W, BF16)] * 2 + [chn] * 3,
        grid=(T // tm,),
        in_specs=[tok_spec(D), _const_spec((1, D)), _const_spec((D, N)),
                  _const_spec((1, RW_COLS)), _const_spec((LANE, 2 * W)), _const_spec((1, 2 * W)),
                  _const_spec((LANE, W)), _const_spec((1, W)), _const_spec((1, W)), _const_spec((1, W)),
                  _const_spec((W, W))],
        out_specs=[tok_spec(MB_WIDTH), tok_spec(MB_XBC), tok_spec(LANE)] + [tok_spec(W)] * 6 + [chn_spec] * 3,
        scratch_shapes=[pltpu.VMEM((HALO, RW_COLS), F32)],
        compiler_params=_cparams("arbitrary"),
        name="proj_rwkv",
    )(x2, nw, w, mu, wlr, w0a0, g2, kk, ka, rk, bd)


def _wkv_local_kernel(r_ref, a_ref, v_ref, lw_ref, kt_ref, bt_ref, lwt_ref, wr_ref, y0_ref, m_ref, hl_ref):
    C = RW_CHUNK
    nch = r_ref.shape[0] // C
    ri = _iota2((C, C), 0)
    ci = _iota2((C, C), 1)
    incl = ri >= ci
    strict = ri > ci
    tril_b = jnp.where(incl, 1.0, 0.0).astype(BF16)
    triu_b = jnp.where(ri <= ci, 1.0, 0.0).astype(BF16)
    m0 = ci < RW_HEAD
    bdm = (ri // RW_HEAD) == (ci // RW_HEAD)
    swap = lambda t: pltpu.roll(t, RW_HEAD, 1)
    sls = [slice(j * C, (j + 1) * C) for j in range(nch)]
    chunks = range(nch)
    pairs = [(j, h) for j in chunks for h in range(2)]
    r = [r_ref[sl, :] for sl in sls]
    a = [a_ref[sl, :] for sl in sls]
    lw = [lw_ref[sl, :] for sl in sls]
    kt = [kt_ref[:, sl] for sl in sls]
    bt = [bt_ref[:, sl] for sl in sls]
    v = [v_ref[sl, :] for sl in sls]
    vb = [t.astype(BF16) for t in v]
    c = [_dot_2x(tril_b, lw[j], pack=True) for j in chunks]
    ct = [_dot_x2(lwt_ref[:, sls[j]], triu_b, pack=True) for j in chunks]
    aa = []
    for j in chunks:
        c_mid = c[j][C // 2 - 1:C // 2]
        ct_mid = ct[j][:, C // 2 - 1:C // 2]
        at = a[j] * jnp.exp(c[j] - lw[j] - c_mid)
        rt = r[j] * jnp.exp(c[j] - c_mid)
        e_mid = jnp.exp(ct_mid - ct[j])
        lhs = jnp.concatenate([jnp.where(m0, at, 0.0), jnp.where(m0, 0.0, at),
                               jnp.where(m0, rt, 0.0), jnp.where(m0, 0.0, rt)], axis=0)
        rhs = jnp.concatenate([bt[j] * e_mid, kt[j] * e_mid], axis=1)
        aa.append(_bdot(lhs, rhs))
    n = {(j, h): jnp.where(strict, aa[j][h * C:(h + 1) * C, :C], 0.0) for j, h in pairs}
    ak = {(j, h): jnp.where(strict, aa[j][h * C:(h + 1) * C, C:], 0.0).astype(BF16) for j, h in pairs}
    rb = {(j, h): jnp.where(incl, aa[j][(2 + h) * C:(3 + h) * C, :C], 0.0).astype(BF16) for j, h in pairs}
    rk = {(j, h): jnp.where(incl, aa[j][(2 + h) * C:(3 + h) * C, C:], 0.0).astype(BF16) for j, h in pairs}
    vh = {}
    x = {}
    for j in chunks:
        a_abs = a[j] * jnp.exp(c[j] - lw[j])
        vh[j, 0] = jnp.where(m0, 0.0, swap(v[j])).astype(BF16)
        vh[j, 1] = jnp.where(m0, 0.0, v[j]).astype(BF16)
        x[j, 0] = jnp.where(m0, a_abs, _dot(ak[j, 0], vh[j, 0]))
        x[j, 1] = jnp.where(m0, swap(a_abs), _dot(ak[j, 1], vh[j, 1]))
    sh = 1
    while sh < C:
        top = sh if sh % (2 * HALO) == 0 else 0
        nb = {p: n[p].astype(BF16) for p in pairs}
        if 2 * sh < C:
            res = {p: _dot(nb[p][top:], jnp.concatenate([x[p].astype(BF16), nb[p]], axis=1)) for p in pairs}
            n = {p: res[p][:, C:] for p in pairs}
        else:
            res = {p: _dot(nb[p][top:], x[p].astype(BF16)) for p in pairs}
        if top:
            x = {p: jnp.concatenate([x[p][:top], x[p][top:] + res[p][:, :C]], axis=0) for p in pairs}
            if 2 * sh < C:
                n = {p: jnp.concatenate([jnp.zeros((top, C), F32), n[p]], axis=0) for p in pairs}
        else:
            x = {p: x[p] + res[p][:, :C] for p in pairs}
        sh *= 2
    g = {p: _dot(jnp.concatenate([rb[p], rk[p]], axis=1), jnp.concatenate([x[p].astype(BF16), vh[p]], axis=0))
         for p in pairs}
    for j in chunks:
        wa = jnp.where(m0, x[j, 0], swap(x[j, 1]))
        u0 = jnp.where(m0, swap(x[j, 0]), x[j, 1])
        wr_ref[sls[j], :] = (r[j] * jnp.exp(c[j]) + jnp.where(m0, g[j, 0], swap(g[j, 1]))).astype(BF16)
        y0_ref[sls[j], :] = jnp.where(m0, swap(g[j, 0]), g[j, 1]).astype(BF16)
        ct_last = ct[j][:, C - 1:C]
        e_last = jnp.exp(ct_last - ct[j])
        bh = (bt[j] * e_last).astype(BF16)
        kh = (kt[j] * e_last).astype(BF16)
        rhs = jnp.concatenate([jnp.concatenate([wa.astype(BF16), u0.astype(BF16)], axis=1),
                               jnp.concatenate([jnp.zeros((C, C), BF16), vb[j]], axis=1)], axis=0)
        mh = _dot(jnp.concatenate([bh, kh], axis=1), rhs)
        m_ref[j, 0] = jnp.where(bdm, mh[:, :C], 0.0) + jnp.where(ri == ci, jnp.exp(ct_last), 0.0)
        hl_ref[j, 0] = jnp.where(bdm, mh[:, C:], 0.0)


def wkv_local(r, a, v, lw, kt, bt, lwt, tq=1024):
    T = r.shape[0]
    C = RW_CHUNK
    npair = RW_WIDTH // LANE
    tok = pl.BlockSpec((tq, LANE), lambda i, h: (i, h))
    chn = pl.BlockSpec((LANE, tq), lambda i, h: (h, i))
    mat = pl.BlockSpec((tq // C, 1, LANE, LANE), lambda i, h: (i, h, 0, 0))
    mat_shape = jax.ShapeDtypeStruct((T // C, npair, LANE, LANE), F32)
    return pl.pallas_call(
        _wkv_local_kernel,
        out_shape=[jax.ShapeDtypeStruct((T, RW_WIDTH), BF16)] * 2 + [mat_shape, mat_shape],
        grid=(T // tq, npair),
        in_specs=[tok] * 4 + [chn] * 3,
        out_specs=[tok, tok, mat, mat],
        compiler_params=_cparams("parallel", "parallel"),
        name="wkv_local",
    )(r, a, v, lw, kt, bt, lwt)


def _wkv_scan_kernel(wr_ref, y0_ref, m_ref, hl_ref, g_ref, bon_ref, lnw_ref, lnb_ref, o_ref, h_ref):
    @pl.when(pl.program_id(1) == 0)
    def _():
        h_ref[...] = jnp.zeros_like(h_ref)

    C = RW_CHUNK
    nch = wr_ref.shape[0] // C
    npair = h_ref.shape[0]
    bdm = (_iota2((C, C), 0) // RW_HEAD) == (_iota2((C, C), 1) // RW_HEAD)
    gn = jnp.where(bdm, 1.0 / RW_HEAD, 0.0).astype(BF16)
    h = [h_ref[p] for p in range(npair)]
    for j in range(nch):
        sl = slice(j * C, (j + 1) * C)
        ys = []
        for p in range(npair):
            ln = slice(p * LANE, (p + 1) * LANE)
            ys.append(_dot(wr_ref[sl, ln], h[p].astype(BF16)) + y0_ref[sl, ln])
        h = [_dot_x3(m_ref[j, p], h[p]) + hl_ref[j, p] for p in range(npair)]
        for p in range(npair):
            ln = slice(p * LANE, (p + 1) * LANE)
            mean = _dot_x2(ys[p], gn, pack=True)
            yc = ys[p] - mean
            var = _dot_x2(yc * yc, gn, pack=True)
            yn = yc * lax.rsqrt(var + RW_GN_EPS) * lnw_ref[:, ln] + lnb_ref[:, ln]
            o_ref[sl, ln] = (yn + bon_ref[sl, ln]) * g_ref[sl, ln]
    for p in range(npair):
        h_ref[p] = h[p]


def wkv_scan(wr, y0, m, hl, g, bon, lnw, lnb, B, L, tq=512):
    T, W = wr.shape
    C = RW_CHUNK
    nq = L // tq
    npair = W // LANE
    tok = pl.BlockSpec((tq, W), lambda b, c: (b * nq + c, 0))
    mat = pl.BlockSpec((tq // C, npair, LANE, LANE), lambda b, c: (b * nq + c, 0, 0, 0))
    return pl.pallas_call(
        _wkv_scan_kernel,
        out_shape=jax.ShapeDtypeStruct((T, W), F32),
        grid=(B, nq),
        in_specs=[tok, tok, mat, mat, tok, tok, _const_spec((1, W)), _const_spec((1, W))],
        out_specs=tok,
        scratch_shapes=[pltpu.VMEM((npair, LANE, LANE), F32)],
        compiler_params=_cparams("parallel", "arbitrary"),
        name="wkv_scan",
    )(wr, y0, m, hl, g, bon, lnw, lnb)


def wkv(r, a, v, lw, g, bon, kt, bt, lwt, lnw, lnb, B, L):
    wr, y0, m, hl = wkv_local(r, a, v, lw, kt, bt, lwt)
    return wkv_scan(wr, y0, m, hl, g, bon, lnw, lnb, B, L)


def _mamba_kernel(z_ref, xbc_ref, xh_ref, dt_ref, cw_ref, cb_ref, dtb_ref, alog_ref, ex_ref, dsk_ref, nw_ref,
                  o_ref, st_ref):
    @pl.when(pl.program_id(1) == 0)
    def _():
        st_ref[...] = jnp.zeros_like(st_ref)

    C = MB_CHUNK
    W = MB_WIDTH
    N = MB_STATE
    tq = z_ref.shape[0]
    hpg = MB_HEADS // MB_GROUPS
    gw = W // MB_GROUPS
    xbc = xbc_ref[...]
    xa = jnp.concatenate([xh_ref[...], xbc], axis=0)
    halo_dead = jnp.where(pl.program_id(1) == 0, HALO, 0)
    xa = jnp.where(_iota2((HALO + tq, 1), 0) < halo_dead, 0.0, xa)
    cw = cw_ref[...]
    conv = cw[3:4] * xbc + cb_ref[...]
    for kk in range(1, MB_CONV):
        conv = conv + cw[3 - kk:4 - kk] * pltpu.roll(xa, kk, 0)[HALO:]
    xbc = _silu(conv)
    xs = xbc[:, :W]
    dt = _softplus(dt_ref[...] + dtb_ref[...])
    a = dt * -jnp.exp(alog_ref[...])
    ex = ex_ref[...]
    xdt = xs * _dot_x2(dt, ex, pack=True)
    xdt_b = xdt.astype(BF16)
    ri = _iota2((C, C), 0)
    ci = _iota2((C, C), 1)
    causal = ri >= ci
    tril_b = jnp.where(causal, 1.0, 0.0).astype(BF16)
    head_g = _iota2((C, gw), 1) // MB_HEADDIM
    sls = [slice(j * C, (j + 1) * C) for j in range(tq // C)]
    y_loc, upd, cdec, acs_xs, cms = [], [], [], [], []
    for sl in sls:
        acs = _dot_2x(tril_b, a[sl], pack=True)
        acs_t = acs.T
        acs_x = _dot_x2(acs, ex, pack=True)
        ys, cm_g = [], []
        for gi in range(MB_GROUPS):
            bm = xbc[sl, W + gi * N:W + (gi + 1) * N]
            cm = xbc[sl, W + MB_GROUPS * N + gi * N:W + MB_GROUPS * N + (gi + 1) * N].astype(BF16)
            cb = _dot_nt(cm, bm.astype(BF16))
            lm = []
            for e in range(hpg):
                hd = gi * hpg + e
                lmat = jnp.exp(jnp.where(causal, acs[:, hd:hd + 1] - acs_t[hd:hd + 1, :], -jnp.inf))
                lm.append((cb * lmat).astype(BF16))
            res = _dot(jnp.concatenate(lm, axis=0), xdt_b[sl, gi * gw:(gi + 1) * gw])
            yg = res[:C]
            for e in range(1, hpg):
                yg = jnp.where(head_g == e, res[e * C:(e + 1) * C], yg)
            ys.append(yg)
            cm_g.append(cm)
        y_loc.append(ys)
        cms.append(cm_g)
        acs_xs.append(acs_x)
        acs_last = acs_x[C - 1:C]
        xw_t = (xdt[sl] * jnp.exp(acs_last - acs_x)).T
        upd.append(jnp.concatenate(
            [_bdot(xw_t[gi * gw:(gi + 1) * gw], xbc[sl, W + gi * N:W + (gi + 1) * N]) for gi in range(MB_GROUPS)],
            axis=0))
        cdec.append(jnp.exp(acs_x.T[:, C - 1:C]))
    st = st_ref[...]
    sts = []
    for j in range(len(sls)):
        sts.append(st.astype(BF16))
        st = cdec[j] * st + upd[j]
    st_ref[...] = st
    zs = _silu(z_ref[...])
    for j, sl in enumerate(sls):
        e_x = jnp.exp(acs_xs[j])
        parts = []
        for gi in range(MB_GROUPS):
            gl = slice(gi * gw, (gi + 1) * gw)
            off = _dot_nt(cms[j][gi], sts[j][gl])
            yg = (y_loc[j][gi] + off * e_x[:, gl] + xs[sl, gl] * dsk_ref[:, gl]) * zs[sl, gl]
            parts.append(yg * lax.rsqrt(jnp.mean(yg * yg, axis=-1, keepdims=True) + EPS))
        o_ref[sl, :] = jnp.concatenate(parts, axis=1) * nw_ref[...]


def mamba(z, xbc, dt, B, L, cw, cb, dtb, an, ex, dsk, nw, tq=512):
    T = z.shape[0]
    nc = L // tq
    hb = tq // HALO
    return pl.pallas_call(
        _mamba_kernel,
        out_shape=jax.ShapeDtypeStruct((T, MB_WIDTH), F32),
        grid=(B, nc),
        in_specs=[pl.BlockSpec((tq, MB_WIDTH), lambda b, c: (b * nc + c, 0)),
                  pl.BlockSpec((tq, MB_XBC), lambda b, c: (b * nc + c, 0)),
                  pl.BlockSpec((HALO, MB_XBC), lambda b, c: (jnp.maximum((b * nc + c) * hb - 1, 0), 0)),
                  pl.BlockSpec((tq, LANE), lambda b, c: (b * nc + c, 0)),
                  _const_spec((MB_CONV, MB_XBC)), _const_spec((1, MB_XBC)), _const_spec((1, LANE)),
                  _const_spec((1, LANE)), _const_spec((LANE, MB_WIDTH)), _const_spec((1, MB_WIDTH)),
                  _const_spec((1, MB_WIDTH))],
        out_specs=pl.BlockSpec((tq, MB_WIDTH), lambda b, c: (b * nc + c, 0)),
        scratch_shapes=[pltpu.VMEM((MB_WIDTH, MB_STATE), F32)],
        compiler_params=_cparams("parallel", "arbitrary"),
        name="mamba2",
    )(z, xbc, xbc, dt, cw, cb, dtb, an, ex, dsk, nw)


def _pack_even(w_in, wa2, ba, gnorm, w_out):
    nk = GLA_HEADS * GLA_DK
    nv = GLA_HEADS * GLA_DV
    pad_v = lambda w: jnp.pad(w.reshape(D_MODEL, GLA_HEADS, GLA_DV),
                              ((0, 0), (0, 0), (0, GLA_DVP - GLA_DV))).reshape(D_MODEL, GLA_HEADS * GLA_DVP)
    wq = w_in[:, :2 * nk]
    wv = pad_v(w_in[:, 2 * nk:2 * nk + nv])
    wg = pad_v(w_in[:, 2 * nk + nv:2 * nk + 2 * nv])
    walo = jnp.pad(w_in[:, 2 * nk + 2 * nv:2 * nk + 2 * nv + GLA_RANK], ((0, 0), (0, LANE - GLA_RANK)))
    wu = w_in[:, 2 * nk + 2 * nv + GLA_RANK:]
    w_cat = jnp.concatenate([wq, wv, wg, wu, walo], axis=1).astype(BF16)
    wa = jnp.pad(wa2.reshape(GLA_RANK, GLA_HEADS, GLA_DK).transpose(1, 0, 2),
                 ((0, 0), (0, LANE - GLA_RANK), (0, 0)))
    bah = ba.reshape(GLA_HEADS, 1, GLA_DK)
    nw = jnp.pad(gnorm, (0, GLA_DVP - GLA_DV)).reshape(1, GLA_DVP)
    woa = jnp.pad(w_out[:nv].reshape(GLA_HEADS, GLA_DV, D_MODEL),
                  ((0, 0), (0, GLA_DVP - GLA_DV), (0, 0))).reshape(GLA_HEADS * GLA_DVP, D_MODEL).astype(BF16)
    wob = w_out[nv:].astype(BF16)
    return w_cat, wa, bah, nw, woa, wob


def _s5_zoh_kernel(lr_ref, li_ref, ldt_ref, br_ref, bi_ref, lbr_ref, lbi_ref, bbr_ref, bbi_ref):
    lam_re = lr_ref[...]
    lam_im = li_ref[...]
    dt = jnp.exp(ldt_ref[...])
    mag = jnp.exp(lam_re * dt)
    ang = lam_im * dt
    lb_re = mag * jnp.cos(ang)
    lb_im = mag * jnp.sin(ang)
    den = lam_re * lam_re + lam_im * lam_im
    nr = lb_re - 1.0
    f_re = (nr * lam_re + lb_im * lam_im) / den
    f_im = (lb_im * lam_re - nr * lam_im) / den
    b_re = br_ref[...]
    b_im = bi_ref[...]
    lbr_ref[...] = lb_re
    lbi_ref[...] = lb_im
    bbr_ref[...] = f_re * b_re - f_im * b_im
    bbi_ref[...] = f_re * b_im + f_im * b_re


def _pack_s5(lam_re, lam_im, log_dt, b_re, b_im, c_re, c_im):
    NS = S5_NSTATE
    col = lambda t: t.reshape(NS, 1)
    vec = jax.ShapeDtypeStruct((NS, 1), F32)
    mat = jax.ShapeDtypeStruct((NS, S5_GROUP), F32)
    lb_re, lb_im, bb_re, bb_im = pl.pallas_call(
        _s5_zoh_kernel, out_shape=[vec, vec, mat, mat], name="s5_zoh",
    )(col(lam_re), col(lam_im), col(jnp.repeat(log_dt, S5_STATE)), b_re.reshape(NS, S5_GROUP),
      b_im.reshape(NS, S5_GROUP))
    eye = jnp.eye(S5_GROUPS, dtype=F32)
    grp = lambda m: m.reshape(S5_GROUPS, S5_STATE, S5_GROUP)
    bd_in = lambda m: jnp.einsum('gpc,gh->gchp', grp(m), eye).reshape(S5_WIDTH, NS).astype(BF16)
    bd_out = lambda m: jnp.einsum('gcp,gh->gphc', m, eye).reshape(NS, S5_WIDTH).astype(BF16)
    return (lb_re.reshape(1, NS), lb_im.reshape(1, NS), bd_in(bb_re), bd_in(bb_im), bd_out(c_re), bd_out(c_im))


def _pack_odd(w_in, w0, w2, a0, a2):
    w_cat = jnp.pad(w_in, ((0, 0), (0, PROJ_COLS - w_in.shape[1]))).astype(BF16)
    W = RW_WIDTH
    wlr = jnp.zeros((LANE, 2 * W), F32).at[:64, :W].set(w2).at[64:, W:].set(a2)
    w0a0 = jnp.concatenate([w0, a0]).reshape(1, 2 * W)
    return w_cat, wlr, w0a0


def kernel(x, norm_mix, norm_ffn, norm_final, e_w_in, e_gla_wa2, e_gla_ba, e_gla_norm, e_s5_lambda_re, e_s5_lambda_im, e_s5_log_dt, e_s5_b_re, e_s5_b_im, e_s5_c_re, e_s5_c_im, e_s5_d, e_s5_w_glu, e_s5_b_glu, e_w_out, o_w_in, o_rw_mu, o_rw_w0, o_rw_w2, o_rw_a0, o_rw_a2, o_rw_g2, o_rw_k_k, o_rw_k_a, o_rw_r_k, o_rw_ln_w, o_rw_ln_b, o_mb_conv_w, o_mb_conv_b, o_mb_dt_bias, o_mb_a_log, o_mb_d, o_mb_norm, o_w_out, ffn_w_up, ffn_conv_w, ffn_conv_b, ffn_w_down):
    B, L, D = x.shape
    T = B * L
    depth = norm_mix.shape[0]
    x2 = x.reshape(T, D)
    row = lambda t: t.reshape(1, -1)
    for i in range(depth):
        j = i // 2
        if i % 2 == 0:
            w_cat, wa, bah, gnw, woa, wob = _pack_even(e_w_in[j], e_gla_wa2[j], e_gla_ba[j], e_gla_norm[j],
                                                       e_w_out[j])
            nkq = GLA_HEADS * GLA_DK
            nvp = GLA_HEADS * GLA_DVP
            q, k, v, g, u, alo = norm_proj(x2, row(norm_mix[i]), w_cat, (nkq, nkq, nvp, nvp, S5_WIDTH, LANE))
            ya = gla(q, k, v, g, alo, B, L, wa, bah, gnw)
            lr, li, br, bi, cr, ci = _pack_s5(e_s5_lambda_re[j], e_s5_lambda_im[j], e_s5_log_dt[j],
                                              e_s5_b_re[j], e_s5_b_im[j], e_s5_c_re[j], e_s5_c_im[j])
            yb = s5(u, B, L, lr, li, br, bi, cr, ci, row(e_s5_d[j]), e_s5_w_glu[j].astype(BF16),
                    row(e_s5_b_glu[j]))
            mix = (ya, yb, jnp.concatenate([woa, wob], axis=0))
        else:
            W = RW_WIDTH
            w_cat, wlr, w0a0 = _pack_odd(o_w_in[j], o_rw_w0[j], o_rw_w2[j], o_rw_a0[j], o_rw_a2[j])
            head_of = jnp.arange(W) // RW_HEAD
            bd = (head_of[:, None] == head_of[None, :]).astype(BF16)
            p_z, p_xbc, p_dt, r, a, v, lw, g, bon, kt, bt, lwt = proj_rwkv(
                x2, L, row(norm_mix[i]), w_cat, row(o_rw_mu[j]), wlr, w0a0, o_rw_g2[j], row(o_rw_k_k[j]),
                row(o_rw_k_a[j]), row(o_rw_r_k[j]), bd)
            yc = wkv(r, a, v, lw, g, bon, kt, bt, lwt, row(o_rw_ln_w[j]), row(o_rw_ln_b[j]), B, L)
            an = jnp.pad(o_mb_a_log[j], (0, LANE - MB_HEADS)).reshape(1, LANE)
            dtb = jnp.pad(o_mb_dt_bias[j], (0, LANE - MB_HEADS)).reshape(1, LANE)
            ex = (jnp.arange(LANE)[:, None] == (jnp.arange(MB_WIDTH) // MB_HEADDIM)[None, :]).astype(BF16)
            dsk = jnp.repeat(o_mb_d[j], MB_HEADDIM).reshape(1, MB_WIDTH)
            yd = mamba(p_z, p_xbc, p_dt, B, L, o_mb_conv_w[j], row(o_mb_conv_b[j]), dtb, an, ex, dsk, row(o_mb_norm[j]))
            mix = (yc, yd, o_w_out[j].astype(BF16))
        x2 = ffn(x2, mix[0], mix[1], L, mix[2], row(norm_ffn[i]), ffn_w_up[i].astype(BF16), ffn_conv_w[i],
                 row(ffn_conv_b[i]), ffn_w_down[i].astype(BF16), row(norm_final), final_norm=(i == depth - 1))
    return x2.reshape(B, L, D)
```

```python
import functools
import math

import jax
import jax.numpy as jnp
from jax import lax
from jax.experimental import pallas as pl
from jax.experimental.pallas import tpu as pltpu

F32 = jnp.float32
BF16 = jnp.bfloat16
HI = lax.Precision.HIGHEST

D_MODEL = 1024
D_FF = 2816
EPS = 1e-6
LANE = 128
HALO = 8

GLA_HEADS = 4
GLA_DK = 128
GLA_DV = 192
GLA_DVP = 256
GLA_RANK = 16
GLA_TAU = 16.0
GLA_CHUNK = 128
S5_WIDTH = 256
S5_GROUPS = 16
S5_GROUP = 16
S5_STATE = 64
S5_NSTATE = S5_GROUPS * S5_STATE
S5_CHUNK = 512
RW_WIDTH = 512
RW_HEAD = 64
RW_COLS = 1792
RW_GN_EPS = 64e-5
RW_CHUNK = 128
MB_WIDTH = 512
MB_HEADS = 8
MB_HEADDIM = 64
MB_GROUPS = 2
MB_STATE = 128
MB_CONV = 4
MB_CHUNK = 128
MB_XBC = 1024
PROJ_COLS = 3456

VMEM_LIMIT = 56 * 1024 * 1024


def _cparams(*sem):
    return pltpu.CompilerParams(dimension_semantics=sem, vmem_limit_bytes=VMEM_LIMIT)


def _dot(a, b, precision=None):
    return jnp.dot(a, b, preferred_element_type=F32, precision=precision)


def _dot_nt(a, b, precision=None):
    return lax.dot_general(a, b, (((1,), (1,)), ((), ())), preferred_element_type=F32, precision=precision)


def _bdot(a, b):
    return _dot(a.astype(BF16), b.astype(BF16))


def _bdot_nt(a, b):
    return _dot_nt(a.astype(BF16), b.astype(BF16))


def _split(x):
    hi = x.astype(BF16)
    return hi, (x - hi.astype(F32)).astype(BF16)


def _dot_x2(x, e, pack=False):
    hi, lo = _split(x)
    if pack:
        return _dot(jnp.concatenate([hi, lo], axis=1), jnp.concatenate([e, e], axis=0))
    return _dot(hi, e) + _dot(lo, e)


def _dot_2x(e, x, pack=False):
    hi, lo = _split(x)
    if pack:
        return _dot(jnp.concatenate([e, e], axis=1), jnp.concatenate([hi, lo], axis=0))
    return _dot(e, hi) + _dot(e, lo)


def _dot_x3(a, b):
    ah, al = _split(a)
    bh, bl = _split(b)
    return _dot(ah, bh) + (_dot(al, bh) + _dot(ah, bl))


def _rms(x, w):
    return x * lax.rsqrt(jnp.mean(x * x, axis=-1, keepdims=True) + EPS) * w


def _sigmoid(x):
    return 1.0 / (1.0 + jnp.exp(-x))


def _silu(x):
    return x * _sigmoid(x)


def _softplus(x):
    return jnp.maximum(x, 0.0) + jnp.log(1.0 + jnp.exp(-jnp.abs(x)))


def _iota2(shape, dim):
    return lax.broadcasted_iota(jnp.int32, shape, dim)


def _const_spec(shape):
    nd = len(shape)
    return pl.BlockSpec(shape, lambda *_: (0,) * nd)


def _norm_proj_kernel(x_ref, nw_ref, w_ref, *o_refs):
    h = _rms(x_ref[...], nw_ref[...])
    y = _dot(h.astype(BF16), w_ref[...])
    off = 0
    for o_ref in o_refs:
        n = o_ref.shape[1]
        o_ref[...] = y[:, off:off + n]
        off += n


def norm_proj(x2, nw, w, splits, tm=512):
    T, D = x2.shape
    N = w.shape[1]
    assert sum(splits) == N
    return pl.pallas_call(
        _norm_proj_kernel,
        out_shape=[jax.ShapeDtypeStruct((T, n), F32) for n in splits],
        grid=(T // tm,),
        in_specs=[pl.BlockSpec((tm, D), lambda i: (i, 0)), _const_spec((1, D)), _const_spec((D, N))],
        out_specs=[pl.BlockSpec((tm, n), lambda i: (i, 0)) for n in splits],
        compiler_params=_cparams("parallel"),
        name="norm_proj",
    )(x2, nw, w)


def _ffn_kernel(x_ref, ya_ref, yb_ref, wa_ref, wb_ref, nw_ref, wg_ref, wu_ref, cw_ref, cb_ref, wd_ref, fnw_ref,
                o_ref, gtail_ref, *, tiles_per_seq, final_norm):
    tm = x_ref.shape[0]
    x = (x_ref[...] + _bdot(ya_ref[...], wa_ref[...]) + _bdot(yb_ref[...], wb_ref[...]))
    h = _rms(x, nw_ref[...]).astype(BF16)
    gate = _dot(h, wg_ref[...])
    seq_start = (pl.program_id(0) % tiles_per_seq) == 0
    tail = jnp.where(seq_start, 0.0, gtail_ref[...])
    gtail_ref[...] = gate[tm - HALO:]
    ga = jnp.concatenate([tail, gate], axis=0)
    cw = cw_ref[...]
    conv = (cw[2:3] * gate
            + cw[1:2] * pltpu.roll(ga, 1, 0)[HALO:]
            + cw[0:1] * pltpu.roll(ga, 2, 0)[HALO:]) + cb_ref[...]
    up = _dot(h, wu_ref[...])
    act = (_silu(conv) * up).astype(BF16)
    y = x + _dot(act, wd_ref[...])
    if final_norm:
        y = _rms(y, fnw_ref[...])
    o_ref[...] = y


def ffn(x2, ya, yb, seq_len, w_mix, nw, w_up, cw, cb, wd, fnw, final_norm, tm=256):
    T, D = x2.shape
    FF = w_up.shape[1] // 2
    ka, kb = ya.shape[1], yb.shape[1]
    assert ka % kb == 0 and w_mix.shape[0] == ka + kb
    tile = lambda n: pl.BlockSpec((tm, n), lambda i: (i, 0))
    kern = functools.partial(_ffn_kernel, tiles_per_seq=seq_len // tm, final_norm=final_norm)
    return pl.pallas_call(
        kern,
        out_shape=jax.ShapeDtypeStruct((T, D), F32),
        grid=(T // tm,),
        in_specs=[tile(D), tile(ka), tile(kb),
                  pl.BlockSpec((ka, D), lambda i: (0, 0)), pl.BlockSpec((kb, D), lambda i: (ka // kb, 0)),
                  _const_spec((1, D)),
                  pl.BlockSpec((D, FF), lambda i: (0, 0)), pl.BlockSpec((D, FF), lambda i: (0, 1)),
                  _const_spec((3, FF)), _const_spec((1, FF)), _const_spec((FF, D)), _const_spec((1, D))],
        out_specs=tile(D),
        scratch_shapes=[pltpu.VMEM((HALO, FF), F32)],
        compiler_params=_cparams("arbitrary"),
        name="ffn",
    )(x2, ya, yb, w_mix, w_mix, nw, w_up, w_up, cw, cb, wd, fnw)


def _gla_kernel(q_ref, k_ref, v_ref, g_ref, alo_ref, wa_ref, ba_ref, nw_ref, o_ref, st_ref):
    @pl.when(pl.program_id(2) == 0)
    def _():
        st_ref[...] = jnp.zeros_like(st_ref)

    C = GLA_CHUNK
    tq = q_ref.shape[0]
    nh = st_ref.shape[0]
    ri = _iota2((C, C), 0)
    ci = _iota2((C, C), 1)
    tril = (ri >= ci)
    tril_b = jnp.where(tril, 1.0, 0.0).astype(BF16)
    scale = GLA_DK ** -0.5
    alo = alo_ref[...].astype(BF16)
    sls = [slice(j * C, (j + 1) * C) for j in range(tq // C)]
    o_in, kdv, qd, dec = {}, {}, {}, {}
    for h in range(nh):
        kl = slice(h * GLA_DK, (h + 1) * GLA_DK)
        vl = slice(h * GLA_DVP, (h + 1) * GLA_DVP)
        xg = _dot(alo, wa_ref[h].astype(BF16)) + ba_ref[h]
        la = (jnp.minimum(xg, 0.0) - jnp.log(1.0 + jnp.exp(-jnp.abs(xg)))) * (1.0 / GLA_TAU)
        for j, sl in enumerate(sls):
            b = _dot_2x(tril_b, la[sl])
            b_mid = b[C // 2 - 1:C // 2]
            b_last = b[C - 1:C]
            q = q_ref[sl, kl] * scale
            k = k_ref[sl, kl]
            v = v_ref[sl, vl]
            qe = q * jnp.exp(b - b_mid)
            ke = k * jnp.exp(jnp.minimum(b_mid - b, 80.0))
            att = jnp.where(tril, _bdot_nt(qe, ke), 0.0)
            o_in[h, j] = _bdot(att, v)
            kdv[h, j] = _bdot(v.T, k * jnp.exp(b_last - b))
            qd[h, j] = (q * jnp.exp(b)).astype(BF16)
            dec[h, j] = jnp.exp(b_last)
    sts = {}
    for h in range(nh):
        st = st_ref[h]
        for j in range(len(sls)):
            sts[h, j] = st.astype(BF16)
            st = st * dec[h, j] + kdv[h, j]
        st_ref[h] = st
    for h in range(nh):
        vl = slice(h * GLA_DVP, (h + 1) * GLA_DVP)
        for j, sl in enumerate(sls):
            o = o_in[h, j] + _dot_nt(qd[h, j], sts[h, j])
            ms = jnp.sum(o * o, axis=-1, keepdims=True) * (1.0 / GLA_DV)
            o = o * lax.rsqrt(ms + EPS) * nw_ref[...]
            o_ref[sl, vl] = o * _silu(g_ref[sl, vl])


def gla(q, k, v, g, alo, B, L, wa, ba, nw, tq=512, nh=4):
    T = q.shape[0]
    nq = L // tq
    heads = lambda w: pl.BlockSpec((tq, nh * w), lambda b, h, c: (b * nq + c, h))
    return pl.pallas_call(
        _gla_kernel,
        out_shape=jax.ShapeDtypeStruct((T, GLA_HEADS * GLA_DVP), F32),
        grid=(B, GLA_HEADS // nh, nq),
        in_specs=[heads(GLA_DK), heads(GLA_DK), heads(GLA_DVP), heads(GLA_DVP),
                  pl.BlockSpec((tq, LANE), lambda b, h, c: (b * nq + c, 0)),
                  pl.BlockSpec((nh, LANE, GLA_DK), lambda b, h, c: (h, 0, 0)),
                  pl.BlockSpec((nh, 1, GLA_DK), lambda b, h, c: (h, 0, 0)),
                  _const_spec((1, GLA_DVP))],
        out_specs=heads(GLA_DVP),
        scratch_shapes=[pltpu.VMEM((nh, GLA_DVP, GLA_DK), F32)],
        compiler_params=_cparams("parallel", "parallel", "arbitrary"),
        name="gla",
    )(q, k, v, g, alo, wa, ba, nw)


def _cmul(ar, ai, br, bi):
    return ar * br - ai * bi, ar * bi + ai * br


def _s5_kernel(u_ref, perm_ref, unperm_ref, lr_ref, li_ref, br_ref, bi_ref, cr_ref, ci_ref, d_ref, wg_ref,
               bg_ref, o_ref, sr_ref, si_ref, pr_ref, pi_ref, xr_ref, xi_ref):
    tc = u_ref.shape[0]
    seg = tc // HALO
    ns = lr_ref.shape[1]
    lr = lr_ref[...]
    li = li_ref[...]
    lr8 = jnp.broadcast_to(lr, (HALO, ns))
    li8 = jnp.broadcast_to(li, (HALO, ns))
    rows = lambda j: slice(j * HALO, (j + 1) * HALO)

    @pl.when(pl.program_id(1) == 0)
    def _():
        sr_ref[...] = jnp.zeros_like(sr_ref)
        si_ref[...] = jnp.zeros_like(si_ref)
        zr, zi = lr8, li8
        for j in range(seg):
            pr_ref[rows(j), :] = zr
            pi_ref[rows(j), :] = zi
            zr, zi = _cmul(lr8, li8, zr, zi)

    u = u_ref[...]
    ub = _dot(perm_ref[...], u.astype(BF16)).astype(BF16)
    xr_ref[...] = _dot(ub, br_ref[...])
    xi_ref[...] = _dot(ub, bi_ref[...])
    er = xr_ref[rows(0), :]
    ei = xi_ref[rows(0), :]
    for j in range(1, seg):
        tr, ti = _cmul(lr8, li8, er, ei)
        er = xr_ref[rows(j), :] + tr
        ei = xi_ref[rows(j), :] + ti
        xr_ref[rows(j), :] = er
        xi_ref[rows(j), :] = ei
    mr, mi = lr, li
    for _ in range(seg.bit_length() - 1):
        mr, mi = _cmul(mr, mi, mr, mi)
    row8 = _iota2((HALO, 1), 0)
    cr0, ci0 = _cmul(mr, mi, sr_ref[...], si_ref[...])
    er = er + jnp.where(row8 == 0, cr0, 0.0)
    ei = ei + jnp.where(row8 == 0, ci0, 0.0)
    sh = 1
    while sh < HALO:
        yr, yi = _cmul(mr, mi, jnp.where(row8 < sh, 0.0, pltpu.roll(er, sh, 0)),
                       jnp.where(row8 < sh, 0.0, pltpu.roll(ei, sh, 0)))
        er, ei = er + yr, ei + yi
        mr, mi = _cmul(mr, mi, mr, mi)
        sh *= 2
    inr = jnp.where(row8 == 0, sr_ref[...], pltpu.roll(er, 1, 0))
    ini = jnp.where(row8 == 0, si_ref[...], pltpu.roll(ei, 1, 0))
    sr_ref[...] = er[HALO - 1:]
    si_ref[...] = ei[HALO - 1:]
    ar, ai = _cmul(pr_ref[...].reshape(seg, HALO, ns), pi_ref[...].reshape(seg, HALO, ns), inr[None], ini[None])
    xr = xr_ref[...] + ar.reshape(tc, ns)
    xi = xi_ref[...] + ai.reshape(tc, ns)
    yp = _bdot(xr, cr_ref[...]) - _bdot(xi, ci_ref[...])
    y = _dot_2x(unperm_ref[...], yp) + d_ref[...] * u
    z = 0.5 * y * (1.0 + jnp.tanh(math.sqrt(2.0 / math.pi) * (y + 0.044715 * (y * y * y))))
    o_ref[...] = z * _sigmoid(_bdot(z, wg_ref[...]) + bg_ref[...])


def s5(u, B, L, lr, li, br, bi, cr, ci, d, wg, bg, tc=S5_CHUNK):
    T = u.shape[0]
    nc = L // tc
    NS = S5_NSTATE
    seg = tc // HALO
    rho = jnp.arange(tc)
    perm = (jnp.arange(tc)[None, :] == (seg * (rho % HALO) + rho // HALO)[:, None]).astype(BF16)
    big = pltpu.VMEM((tc, NS), F32)
    return pl.pallas_call(
        _s5_kernel,
        out_shape=jax.ShapeDtypeStruct((T, S5_WIDTH), F32),
        grid=(B, nc),
        in_specs=[pl.BlockSpec((tc, S5_WIDTH), lambda b, c: (b * nc + c, 0)),
                  _const_spec((tc, tc)), _const_spec((tc, tc)),
                  _const_spec((1, NS)), _const_spec((1, NS)),
                  _const_spec((S5_WIDTH, NS)), _const_spec((S5_WIDTH, NS)),
                  _const_spec((NS, S5_WIDTH)), _const_spec((NS, S5_WIDTH)),
                  _const_spec((1, S5_WIDTH)), _const_spec((S5_WIDTH, S5_WIDTH)), _const_spec((1, S5_WIDTH))],
        out_specs=pl.BlockSpec((tc, S5_WIDTH), lambda b, c: (b * nc + c, 0)),
        scratch_shapes=[pltpu.VMEM((1, NS), F32), pltpu.VMEM((1, NS), F32), big, big, big, big],
        compiler_params=_cparams("parallel", "arbitrary"),
        name="s5",
    )(u, perm, perm.T, lr, li, br, bi, cr, ci, d, wg, bg)


def _proj_rwkv_kernel(x_ref, nw_ref, w_ref, mu_ref, wlr_ref, w0a0_ref, g2_ref, kk_ref, ka_ref, rk_ref, bd_ref,
                      z_ref, xbc_ref, dt_ref, r_ref, a_ref, v_ref, lw_ref, g_ref, bon_ref, kt_ref, bt_ref, lwt_ref,
                      tail_ref, *, tiles_per_seq):
    tm = x_ref.shape[0]
    W = RW_WIDTH
    y = _dot(_rms(x_ref[...], nw_ref[...]).astype(BF16), w_ref[...])
    p = y[:, :RW_COLS]
    z_ref[...] = y[:, RW_COLS:RW_COLS + MB_WIDTH]
    xbc_ref[...] = y[:, RW_COLS + MB_WIDTH:RW_COLS + MB_WIDTH + MB_XBC]
    dt_ref[...] = y[:, RW_COLS + MB_WIDTH + MB_XBC:]
    seq_start = (pl.program_id(0) % tiles_per_seq) == 0
    tail = jnp.where(seq_start, 0.0, tail_ref[...])
    tail_ref[...] = p[tm - HALO:]
    prev = pltpu.roll(jnp.concatenate([tail, p], axis=0), 1, 0)[HALO:]
    pm = p + (prev - p) * mu_ref[...]
    r = pm[:, :W]
    k = pm[:, W:2 * W]
    v = pm[:, 2 * W:3 * W]
    xwa = pm[:, 3 * W:3 * W + LANE]
    xg = pm[:, 3 * W + LANE:]
    xwa = jnp.where(_iota2((tm, LANE), 1) < 64, jnp.tanh(xwa), xwa)
    wa = _dot_x3(xwa, wlr_ref[...]) + w0a0_ref[...]
    wlog = -_softplus(-wa[:, :W]) - 0.5
    lw = -jnp.exp(wlog)
    a = _sigmoid(wa[:, W:])
    g = _bdot(_sigmoid(xg), g2_ref[...])
    bd = bd_ref[...]
    kk = k * kk_ref[...]
    kk = kk / jnp.maximum(jnp.sqrt(_dot_x2(kk * kk, bd)), 1e-12)
    k2 = k * (1.0 + (a - 1.0) * ka_ref[...])
    r_ref[...] = r
    a_ref[...] = -kk
    v_ref[...] = v
    lw_ref[...] = lw
    g_ref[...] = g.astype(BF16)
    bon_ref[...] = (_dot_x2(r * k2 * rk_ref[...], bd) * v).astype(BF16)
    kt_ref[...] = k2.T
    bt_ref[...] = (kk * a).T
    lwt_ref[...] = lw.T


def proj_rwkv(x2, L, nw, w, mu, wlr, w0a0, g2, kk, ka, rk, bd, tm=512):
    T, D = x2.shape
    N = w.shape[1]
    W = RW_WIDTH
    tok = lambda n, dt=F32: jax.ShapeDtypeStruct((T, n), dt)
    chn = jax.ShapeDtypeStruct((W, T), F32)
    tok_spec = lambda n: pl.BlockSpec((tm, n), lambda i: (i, 0))
    chn_spec = pl.BlockSpec((W, tm), lambda i: (0, i))
    kern = functools.partial(_proj_rwkv_kernel, tiles_per_seq=L // tm)
    return pl.pallas_call(
        kern,
        out_shape=[tok(MB_WIDTH), tok(MB_XBC), tok(LANE)] + [tok(W)] * 4 + [tok(W, BF16)] * 2 + [chn] * 3,
        grid=(T // tm,),
        in_specs=[tok_spec(D), _const_spec((1, D)), _const_spec((D, N)),
                  _const_spec((1, RW_COLS)), _const_spec((LANE, 2 * W)), _const_spec((1, 2 * W)),
                  _const_spec((LANE, W)), _const_spec((1, W)), _const_spec((1, W)), _const_spec((1, W)),
                  _const_spec((W, W))],
        out_specs=[tok_spec(MB_WIDTH), tok_spec(MB_XBC), tok_spec(LANE)] + [tok_spec(W)] * 6 + [chn_spec] * 3,
        scratch_shapes=[pltpu.VMEM((HALO, RW_COLS), F32)],
        compiler_params=_cparams("arbitrary"),
        name="proj_rwkv",
    )(x2, nw, w, mu, wlr, w0a0, g2, kk, ka, rk, bd)


def _wkv_local_kernel(r_ref, a_ref, v_ref, lw_ref, kt_ref, bt_ref, lwt_ref, wr_ref, y0_ref, m_ref, hl_ref):
    C = RW_CHUNK
    nch = r_ref.shape[0] // C
    ri = _iota2((C, C), 0)
    ci = _iota2((C, C), 1)
    incl = ri >= ci
    strict = ri > ci
    tril_b = jnp.where(incl, 1.0, 0.0).astype(BF16)
    triu_b = jnp.where(ri <= ci, 1.0, 0.0).astype(BF16)
    m0 = ci < RW_HEAD
    bdm = (ri // RW_HEAD) == (ci // RW_HEAD)
    swap = lambda t: pltpu.roll(t, RW_HEAD, 1)
    sls = [slice(j * C, (j + 1) * C) for j in range(nch)]
    chunks = range(nch)
    pairs = [(j, h) for j in chunks for h in range(2)]
    r = [r_ref[sl, :] for sl in sls]
    a = [a_ref[sl, :] for sl in sls]
    lw = [lw_ref[sl, :] for sl in sls]
    kt = [kt_ref[:, sl] for sl in sls]
    bt = [bt_ref[:, sl] for sl in sls]
    v = [v_ref[sl, :] for sl in sls]
    vb = [t.astype(BF16) for t in v]
    c = [_dot_2x(tril_b, lw[j], pack=True) for j in chunks]
    ct = [_dot_x2(lwt_ref[:, sls[j]], triu_b, pack=True) for j in chunks]
    aa = []
    for j in chunks:
        c_mid = c[j][C // 2 - 1:C // 2]
        ct_mid = ct[j][:, C // 2 - 1:C // 2]
        at = a[j] * jnp.exp(c[j] - lw[j] - c_mid)
        rt = r[j] * jnp.exp(c[j] - c_mid)
        e_mid = jnp.exp(ct_mid - ct[j])
        lhs = jnp.concatenate([jnp.where(m0, at, 0.0), jnp.where(m0, 0.0, at),
                               jnp.where(m0, rt, 0.0), jnp.where(m0, 0.0, rt)], axis=0)
        rhs = jnp.concatenate([bt[j] * e_mid, kt[j] * e_mid], axis=1)
        aa.append(_bdot(lhs, rhs))
    n = {(j, h): jnp.where(strict, aa[j][h * C:(h + 1) * C, :C], 0.0) for j, h in pairs}
    ak = {(j, h): jnp.where(strict, aa[j][h * C:(h + 1) * C, C:], 0.0).astype(BF16) for j, h in pairs}
    rb = {(j, h): jnp.where(incl, aa[j][(2 + h) * C:(3 + h) * C, :C], 0.0).astype(BF16) for j, h in pairs}
    rk = {(j, h): jnp.where(incl, aa[j][(2 + h) * C:(3 + h) * C, C:], 0.0).astype(BF16) for j, h in pairs}
    vh = {}
    x = {}
    for j in chunks:
        a_abs = a[j] * jnp.exp(c[j] - lw[j])
        vh[j, 0] = jnp.where(m0, 0.0, swap(v[j])).astype(BF16)
        vh[j, 1] = jnp.where(m0, 0.0, v[j]).astype(BF16)
        x[j, 0] = jnp.where(m0, a_abs, _dot(ak[j, 0], vh[j, 0]))
        x[j, 1] = jnp.where(m0, swap(a_abs), _dot(ak[j, 1], vh[j, 1]))
    sh = 1
    while sh < C:
        top = sh if sh % (2 * HALO) == 0 else 0
        nb = {p: n[p].astype(BF16) for p in pairs}
        if 2 * sh < C:
            res = {p: _dot(nb[p][top:], jnp.concatenate([x[p].astype(BF16), nb[p]], axis=1)) for p in pairs}
            n = {p: res[p][:, C:] for p in pairs}
        else:
            res = {p: _dot(nb[p][top:], x[p].astype(BF16)) for p in pairs}
        if top:
            x = {p: jnp.concatenate([x[p][:top], x[p][top:] + res[p][:, :C]], axis=0) for p in pairs}
            if 2 * sh < C:
                n = {p: jnp.concatenate([jnp.zeros((top, C), F32), n[p]], axis=0) for p in pairs}
        else:
            x = {p: x[p] + res[p][:, :C] for p in pairs}
        sh *= 2
    g = {p: _dot(jnp.concatenate([rb[p], rk[p]], axis=1), jnp.concatenate([x[p].astype(BF16), vh[p]], axis=0))
         for p in pairs}
    for j in chunks:
        wa = jnp.where(m0, x[j, 0], swap(x[j, 1]))
        u0 = jnp.where(m0, swap(x[j, 0]), x[j, 1])
        wr_ref[sls[j], :] = (r[j] * jnp.exp(c[j]) + jnp.where(m0, g[j, 0], swap(g[j, 1]))).astype(BF16)
        y0_ref[sls[j], :] = jnp.where(m0, swap(g[j, 0]), g[j, 1]).astype(BF16)
        ct_last = ct[j][:, C - 1:C]
        e_last = jnp.exp(ct_last - ct[j])
        bh = (bt[j] * e_last).astype(BF16)
        kh = (kt[j] * e_last).astype(BF16)
        rhs = jnp.concatenate([jnp.concatenate([wa.astype(BF16), u0.astype(BF16)], axis=1),
                               jnp.concatenate([jnp.zeros((C, C), BF16), vb[j]], axis=1)], axis=0)
        mh = _dot(jnp.concatenate([bh, kh], axis=1), rhs)
        m = jnp.where(bdm, mh[:, :C], 0.0) + jnp.where(ri == ci, jnp.exp(ct_last), 0.0)
        hl = jnp.where(bdm, mh[:, C:], 0.0)
        m_ref[j, 0] = m[:RW_HEAD] + m[RW_HEAD:]
        hl_ref[j, 0] = hl[:RW_HEAD] + hl[RW_HEAD:]


def wkv_local(r, a, v, lw, kt, bt, lwt, tq=1024):
    T = r.shape[0]
    C = RW_CHUNK
    npair = RW_WIDTH // LANE
    tok = pl.BlockSpec((tq, LANE), lambda i, h: (i, h))
    chn = pl.BlockSpec((LANE, tq), lambda i, h: (h, i))
    mat = pl.BlockSpec((tq // C, 1, RW_HEAD, LANE), lambda i, h: (i, h, 0, 0))
    mat_shape = jax.ShapeDtypeStruct((T // C, npair, RW_HEAD, LANE), F32)
    return pl.pallas_call(
        _wkv_local_kernel,
        out_shape=[jax.ShapeDtypeStruct((T, RW_WIDTH), BF16)] * 2 + [mat_shape, mat_shape],
        grid=(T // tq, npair),
        in_specs=[tok] * 4 + [chn] * 3,
        out_specs=[tok, tok, mat, mat],
        compiler_params=_cparams("parallel", "parallel"),
        name="wkv_local",
    )(r, a, v, lw, kt, bt, lwt)


def _wkv_scan_kernel(wr_ref, y0_ref, m_ref, hl_ref, g_ref, bon_ref, lnw_ref, lnb_ref, o_ref, h_ref):
    @pl.when(pl.program_id(1) == 0)
    def _():
        h_ref[...] = jnp.zeros_like(h_ref)

    C = RW_CHUNK
    nch = wr_ref.shape[0] // C
    npair = h_ref.shape[0]
    bdm = (_iota2((C, C), 0) // RW_HEAD) == (_iota2((C, C), 1) // RW_HEAD)
    gn = jnp.where(bdm, 1.0 / RW_HEAD, 0.0).astype(BF16)
    first = _iota2((RW_HEAD, C), 1) < RW_HEAD

    def unfold(t):
        return jnp.concatenate([jnp.where(first, t, 0.0), jnp.where(first, 0.0, t)], axis=0)

    h = [h_ref[p] for p in range(npair)]
    for j in range(nch):
        sl = slice(j * C, (j + 1) * C)
        ys = []
        for p in range(npair):
            ln = slice(p * LANE, (p + 1) * LANE)
            ys.append(_dot(wr_ref[sl, ln], h[p].astype(BF16)) + y0_ref[sl, ln])
        h = [_dot_x3(unfold(m_ref[j, p]), h[p]) + unfold(hl_ref[j, p]) for p in range(npair)]
        for p in range(npair):
            ln = slice(p * LANE, (p + 1) * LANE)
            mean = _dot_x2(ys[p], gn, pack=True)
            yc = ys[p] - mean
            var = _dot_x2(yc * yc, gn, pack=True)
            yn = yc * lax.rsqrt(var + RW_GN_EPS) * lnw_ref[:, ln] + lnb_ref[:, ln]
            o_ref[sl, ln] = (yn + bon_ref[sl, ln]) * g_ref[sl, ln]
    for p in range(npair):
        h_ref[p] = h[p]


def wkv_scan(wr, y0, m, hl, g, bon, lnw, lnb, B, L, tq=512):
    T, W = wr.shape
    C = RW_CHUNK
    nq = L // tq
    npair = W // LANE
    tok = pl.BlockSpec((tq, W), lambda b, c: (b * nq + c, 0))
    mat = pl.BlockSpec((tq // C, npair, RW_HEAD, LANE), lambda b, c: (b * nq + c, 0, 0, 0))
    return pl.pallas_call(
        _wkv_scan_kernel,
        out_shape=jax.ShapeDtypeStruct((T, W), F32),
        grid=(B, nq),
        in_specs=[tok, tok, mat, mat, tok, tok, _const_spec((1, W)), _const_spec((1, W))],
        out_specs=tok,
        scratch_shapes=[pltpu.VMEM((npair, LANE, LANE), F32)],
        compiler_params=_cparams("parallel", "arbitrary"),
        name="wkv_scan",
    )(wr, y0, m, hl, g, bon, lnw, lnb)


def wkv(r, a, v, lw, g, bon, kt, bt, lwt, lnw, lnb, B, L):
    wr, y0, m, hl = wkv_local(r, a, v, lw, kt, bt, lwt)
    return wkv_scan(wr, y0, m, hl, g, bon, lnw, lnb, B, L)


def _mamba_kernel(z_ref, xbc_ref, xh_ref, dt_ref, cw_ref, cb_ref, dtb_ref, alog_ref, ex_ref, dsk_ref, nw_ref,
                  o_ref, st_ref):
    @pl.when(pl.program_id(1) == 0)
    def _():
        st_ref[...] = jnp.zeros_like(st_ref)

    C = MB_CHUNK
    W = MB_WIDTH
    N = MB_STATE
    tq = z_ref.shape[0]
    hpg = MB_HEADS // MB_GROUPS
    gw = W // MB_GROUPS
    xbc = xbc_ref[...]
    xa = jnp.concatenate([xh_ref[...], xbc], axis=0)
    halo_dead = jnp.where(pl.program_id(1) == 0, HALO, 0)
    xa = jnp.where(_iota2((HALO + tq, 1), 0) < halo_dead, 0.0, xa)
    cw = cw_ref[...]
    conv = cw[3:4] * xbc + cb_ref[...]
    for kk in range(1, MB_CONV):
        conv = conv + cw[3 - kk:4 - kk] * pltpu.roll(xa, kk, 0)[HALO:]
    xbc = _silu(conv)
    xs = xbc[:, :W]
    dt = _softplus(dt_ref[...] + dtb_ref[...])
    a = dt * -jnp.exp(alog_ref[...])
    ex = ex_ref[...]
    xdt = xs * _dot_x2(dt, ex, pack=True)
    xdt_b = xdt.astype(BF16)
    ri = _iota2((C, C), 0)
    ci = _iota2((C, C), 1)
    causal = ri >= ci
    tril_b = jnp.where(causal, 1.0, 0.0).astype(BF16)
    head_g = _iota2((C, gw), 1) // MB_HEADDIM
    sls = [slice(j * C, (j + 1) * C) for j in range(tq // C)]
    y_loc, upd, cdec, acs_xs, cms = [], [], [], [], []
    for sl in sls:
        acs = _dot_2x(tril_b, a[sl], pack=True)
        acs_t = acs.T
        acs_x = _dot_x2(acs, ex, pack=True)
        ys, cm_g = [], []
        for gi in range(MB_GROUPS):
            bm = xbc[sl, W + gi * N:W + (gi + 1) * N]
            cm = xbc[sl, W + MB_GROUPS * N + gi * N:W + MB_GROUPS * N + (gi + 1) * N].astype(BF16)
            cb = _dot_nt(cm, bm.astype(BF16))
            lm = []
            for e in range(hpg):
                hd = gi * hpg + e
                lmat = jnp.exp(jnp.where(causal, acs[:, hd:hd + 1] - acs_t[hd:hd + 1, :], -jnp.inf))
                lm.append((cb * lmat).astype(BF16))
            res = _dot(jnp.concatenate(lm, axis=0), xdt_b[sl, gi * gw:(gi + 1) * gw])
            yg = res[:C]
            for e in range(1, hpg):
                yg = jnp.where(head_g == e, res[e * C:(e + 1) * C], yg)
            ys.append(yg)
            cm_g.append(cm)
        y_loc.append(ys)
        cms.append(cm_g)
        acs_xs.append(acs_x)
        acs_last = acs_x[C - 1:C]
        xw_t = (xdt[sl] * jnp.exp(acs_last - acs_x)).T
        upd.append(jnp.concatenate(
            [_bdot(xw_t[gi * gw:(gi + 1) * gw], xbc[sl, W + gi * N:W + (gi + 1) * N]) for gi in range(MB_GROUPS)],
            axis=0))
        cdec.append(jnp.exp(acs_x.T[:, C - 1:C]))
    st = st_ref[...]
    sts = []
    for j in range(len(sls)):
        sts.append(st.astype(BF16))
        st = cdec[j] * st + upd[j]
    st_ref[...] = st
    zs = _silu(z_ref[...])
    for j, sl in enumerate(sls):
        e_x = jnp.exp(acs_xs[j])
        parts = []
        for gi in range(MB_GROUPS):
            gl = slice(gi * gw, (gi + 1) * gw)
            off = _dot_nt(cms[j][gi], sts[j][gl])
            yg = (y_loc[j][gi] + off * e_x[:, gl] + xs[sl, gl] * dsk_ref[:, gl]) * zs[sl, gl]
            parts.append(yg * lax.rsqrt(jnp.mean(yg * yg, axis=-1, keepdims=True) + EPS))
        o_ref[sl, :] = jnp.concatenate(parts, axis=1) * nw_ref[...]


def mamba(z, xbc, dt, B, L, cw, cb, dtb, an, ex, dsk, nw, tq=512):
    T = z.shape[0]
    nc = L // tq
    hb = tq // HALO
    return pl.pallas_call(
        _mamba_kernel,
        out_shape=jax.ShapeDtypeStruct((T, MB_WIDTH), F32),
        grid=(B, nc),
        in_specs=[pl.BlockSpec((tq, MB_WIDTH), lambda b, c: (b * nc + c, 0)),
                  pl.BlockSpec((tq, MB_XBC), lambda b, c: (b * nc + c, 0)),
                  pl.BlockSpec((HALO, MB_XBC), lambda b, c: (jnp.maximum((b * nc + c) * hb - 1, 0), 0)),
                  pl.BlockSpec((tq, LANE), lambda b, c: (b * nc + c, 0)),
                  _const_spec((MB_CONV, MB_XBC)), _const_spec((1, MB_XBC)), _const_spec((1, LANE)),
                  _const_spec((1, LANE)), _const_spec((LANE, MB_WIDTH)), _const_spec((1, MB_WIDTH)),
                  _const_spec((1, MB_WIDTH))],
        out_specs=pl.BlockSpec((tq, MB_WIDTH), lambda b, c: (b * nc + c, 0)),
        scratch_shapes=[pltpu.VMEM((MB_WIDTH, MB_STATE), F32)],
        compiler_params=_cparams("parallel", "arbitrary"),
        name="mamba2",
    )(z, xbc, xbc, dt, cw, cb, dtb, an, ex, dsk, nw)


def _pack_even(w_in, wa2, ba, gnorm, w_out):
    nk = GLA_HEADS * GLA_DK
    nv = GLA_HEADS * GLA_DV
    pad_v = lambda w: jnp.pad(w.reshape(D_MODEL, GLA_HEADS, GLA_DV),
                              ((0, 0), (0, 0), (0, GLA_DVP - GLA_DV))).reshape(D_MODEL, GLA_HEADS * GLA_DVP)
    wq = w_in[:, :2 * nk]
    wv = pad_v(w_in[:, 2 * nk:2 * nk + nv])
    wg = pad_v(w_in[:, 2 * nk + nv:2 * nk + 2 * nv])
    walo = jnp.pad(w_in[:, 2 * nk + 2 * nv:2 * nk + 2 * nv + GLA_RANK], ((0, 0), (0, LANE - GLA_RANK)))
    wu = w_in[:, 2 * nk + 2 * nv + GLA_RANK:]
    w_cat = jnp.concatenate([wq, wv, wg, wu, walo], axis=1).astype(BF16)
    wa = jnp.pad(wa2.reshape(GLA_RANK, GLA_HEADS, GLA_DK).transpose(1, 0, 2),
                 ((0, 0), (0, LANE - GLA_RANK), (0, 0)))
    bah = ba.reshape(GLA_HEADS, 1, GLA_DK)
    nw = jnp.pad(gnorm, (0, GLA_DVP - GLA_DV)).reshape(1, GLA_DVP)
    woa = jnp.pad(w_out[:nv].reshape(GLA_HEADS, GLA_DV, D_MODEL),
                  ((0, 0), (0, GLA_DVP - GLA_DV), (0, 0))).reshape(GLA_HEADS * GLA_DVP, D_MODEL).astype(BF16)
    wob = w_out[nv:].astype(BF16)
    return w_cat, wa, bah, nw, woa, wob


def _s5_zoh_kernel(lr_ref, li_ref, ldt_ref, br_ref, bi_ref, lbr_ref, lbi_ref, bbr_ref, bbi_ref):
    lam_re = lr_ref[...]
    lam_im = li_ref[...]
    dt = jnp.exp(ldt_ref[...])
    mag = jnp.exp(lam_re * dt)
    ang = lam_im * dt
    lb_re = mag * jnp.cos(ang)
    lb_im = mag * jnp.sin(ang)
    den = lam_re * lam_re + lam_im * lam_im
    nr = lb_re - 1.0
    f_re = (nr * lam_re + lb_im * lam_im) / den
    f_im = (lb_im * lam_re - nr * lam_im) / den
    b_re = br_ref[...]
    b_im = bi_ref[...]
    lbr_ref[...] = lb_re
    lbi_ref[...] = lb_im
    bbr_ref[...] = f_re * b_re - f_im * b_im
    bbi_ref[...] = f_re * b_im + f_im * b_re


def _pack_s5(lam_re, lam_im, log_dt, b_re, b_im, c_re, c_im):
    NS = S5_NSTATE
    col = lambda t: t.reshape(NS, 1)
    vec = jax.ShapeDtypeStruct((NS, 1), F32)
    mat = jax.ShapeDtypeStruct((NS, S5_GROUP), F32)
    lb_re, lb_im, bb_re, bb_im = pl.pallas_call(
        _s5_zoh_kernel, out_shape=[vec, vec, mat, mat], name="s5_zoh",
    )(col(lam_re), col(lam_im), col(jnp.repeat(log_dt, S5_STATE)), b_re.reshape(NS, S5_GROUP),
      b_im.reshape(NS, S5_GROUP))
    eye = jnp.eye(S5_GROUPS, dtype=F32)
    grp = lambda m: m.reshape(S5_GROUPS, S5_STATE, S5_GROUP)
    bd_in = lambda m: jnp.einsum('gpc,gh->gchp', grp(m), eye).reshape(S5_WIDTH, NS).astype(BF16)
    bd_out = lambda m: jnp.einsum('gcp,gh->gphc', m, eye).reshape(NS, S5_WIDTH).astype(BF16)
    return (lb_re.reshape(1, NS), lb_im.reshape(1, NS), bd_in(bb_re), bd_in(bb_im), bd_out(c_re), bd_out(c_im))


def _pack_odd(w_in, w0, w2, a0, a2):
    w_cat = jnp.pad(w_in, ((0, 0), (0, PROJ_COLS - w_in.shape[1]))).astype(BF16)
    W = RW_WIDTH
    wlr = jnp.zeros((LANE, 2 * W), F32).at[:64, :W].set(w2).at[64:, W:].set(a2)
    w0a0 = jnp.concatenate([w0, a0]).reshape(1, 2 * W)
    return w_cat, wlr, w0a0


def kernel(x, norm_mix, norm_ffn, norm_final, e_w_in, e_gla_wa2, e_gla_ba, e_gla_norm, e_s5_lambda_re, e_s5_lambda_im, e_s5_log_dt, e_s5_b_re, e_s5_b_im, e_s5_c_re, e_s5_c_im, e_s5_d, e_s5_w_glu, e_s5_b_glu, e_w_out, o_w_in, o_rw_mu, o_rw_w0, o_rw_w2, o_rw_a0, o_rw_a2, o_rw_g2, o_rw_k_k, o_rw_k_a, o_rw_r_k, o_rw_ln_w, o_rw_ln_b, o_mb_conv_w, o_mb_conv_b, o_mb_dt_bias, o_mb_a_log, o_mb_d, o_mb_norm, o_w_out, ffn_w_up, ffn_conv_w, ffn_conv_b, ffn_w_down):
    B, L, D = x.shape
    T = B * L
    depth = norm_mix.shape[0]
    x2 = x.reshape(T, D)
    row = lambda t: t.reshape(1, -1)
    for i in range(depth):
        j = i // 2
        if i % 2 == 0:
            w_cat, wa, bah, gnw, woa, wob = _pack_even(e_w_in[j], e_gla_wa2[j], e_gla_ba[j], e_gla_norm[j],
                                                       e_w_out[j])
            nkq = GLA_HEADS * GLA_DK
            nvp = GLA_HEADS * GLA_DVP
            q, k, v, g, u, alo = norm_proj(x2, row(norm_mix[i]), w_cat, (nkq, nkq, nvp, nvp, S5_WIDTH, LANE))
            ya = gla(q, k, v, g, alo, B, L, wa, bah, gnw)
            lr, li, br, bi, cr, ci = _pack_s5(e_s5_lambda_re[j], e_s5_lambda_im[j], e_s5_log_dt[j],
                                              e_s5_b_re[j], e_s5_b_im[j], e_s5_c_re[j], e_s5_c_im[j])
            yb = s5(u, B, L, lr, li, br, bi, cr, ci, row(e_s5_d[j]), e_s5_w_glu[j].astype(BF16),
                    row(e_s5_b_glu[j]))
            mix = (ya, yb, jnp.concatenate([woa, wob], axis=0))
        else:
            W = RW_WIDTH
            w_cat, wlr, w0a0 = _pack_odd(o_w_in[j], o_rw_w0[j], o_rw_w2[j], o_rw_a0[j], o_rw_a2[j])
            head_of = jnp.arange(W) // RW_HEAD
            bd = (head_of[:, None] == head_of[None, :]).astype(BF16)
            p_z, p_xbc, p_dt, r, a, v, lw, g, bon, kt, bt, lwt = proj_rwkv(
                x2, L, row(norm_mix[i]), w_cat, row(o_rw_mu[j]), wlr, w0a0, o_rw_g2[j], row(o_rw_k_k[j]),
                row(o_rw_k_a[j]), row(o_rw_r_k[j]), bd)
            yc = wkv(r, a, v, lw, g, bon, kt, bt, lwt, row(o_rw_ln_w[j]), row(o_rw_ln_b[j]), B, L)
            an = jnp.pad(o_mb_a_log[j], (0, LANE - MB_HEADS)).reshape(1, LANE)
            dtb = jnp.pad(o_mb_dt_bias[j], (0, LANE - MB_HEADS)).reshape(1, LANE)
            ex = (jnp.arange(LANE)[:, None] == (jnp.arange(MB_WIDTH) // MB_HEADDIM)[None, :]).astype(BF16)
            dsk = jnp.repeat(o_mb_d[j], MB_HEADDIM).reshape(1, MB_WIDTH)
            yd = mamba(p_z, p_xbc, p_dt, B, L, o_mb_conv_w[j], row(o_mb_conv_b[j]), dtb, an, ex, dsk, row(o_mb_norm[j]))
            mix = (yc, yd, o_w_out[j].astype(BF16))
        x2 = ffn(x2, mix[0], mix[1], L, mix[2], row(norm_ffn[i]), ffn_w_up[i].astype(BF16), ffn_conv_w[i],
                 row(ffn_conv_b[i]), ffn_w_down[i].astype(BF16), row(norm_final), final_norm=(i == depth - 1))
    return x2.reshape(B, L, D)
```
